```python
import math, functools
import jax
import jax.numpy as jnp
from jax import lax
import numpy as np

D_MODEL = 2048
BATCH = 4
SEQ = 4096
DEPTH = 2

GRID_W = 64
CTX_LEN = 256
N_GROUPS = 4
MIX_W = D_MODEL
GROUP_W = MIX_W // N_GROUPS
N_HEADS = 4
HEAD_DIM = GROUP_W // N_HEADS
KV_HEADS = N_HEADS // 2
CHUNK = 64
Q_BLOCK = 128
CONV_K = 5
ROPE_BASE = 10000.0
NORM_EPS = 1e-6
F32 = jnp.float32

IN_LAYOUT = (
    ('m_q', GROUP_W), ('m_k', GROUP_W), ('m_v', GROUP_W), ('m_o', GROUP_W), ('m_z', GROUP_W),
    ('m_i', 2 * N_HEADS), ('m_f', 2 * N_HEADS),
    ('r_q', GROUP_W), ('r_k', GROUP_W), ('r_v', GROUP_W), ('r_z', GROUP_W),
    ('a_q', GROUP_W), ('a_k', KV_HEADS * HEAD_DIM), ('a_v', KV_HEADS * HEAD_DIM), ('a_z', GROUP_W),
    ('d_qkv', 3 * GROUP_W), ('d_z', GROUP_W), ('d_a', 2 * N_HEADS), ('d_b', 2 * N_HEADS),
)
IN_W = sum(size for _, size in IN_LAYOUT)

kernel_name = 'hybrid_mlstm_retention_gqa_deltanet_dit'


def _rms_norm(x, w):
    xf = x.astype(F32)
    y = xf * lax.rsqrt(jnp.mean(xf * xf, axis=-1, keepdims=True) + NORM_EPS)
    return (y * w.astype(F32)).astype(x.dtype)


def _l2_norm(t):
    tf = t.astype(F32)
    return (tf * lax.rsqrt(jnp.sum(tf * tf, axis=-1, keepdims=True) + NORM_EPS)).astype(t.dtype)


def _head_rms(y, w):
    b, n, _ = y.shape
    yf = y.astype(F32).reshape(b, n, -1, HEAD_DIM)
    yf = yf * lax.rsqrt(jnp.mean(yf * yf, axis=-1, keepdims=True) + NORM_EPS)
    return (yf.reshape(b, n, -1) * w.astype(F32)).astype(y.dtype)


def _heads(t):
    b, n, _ = t.shape
    return t.reshape(b, n, -1, HEAD_DIM).transpose(0, 2, 1, 3)


def _unheads(t):
    b, h, n, dh = t.shape
    return t.transpose(0, 2, 1, 3).reshape(b, n, h * dh)


def _gate_heads(t):
    return jnp.swapaxes(t, 1, 2).astype(F32)


def _split_cols(u):
    parts, start = {}, 0
    for name, size in IN_LAYOUT:
        parts[name] = u[..., start:start + size]
        start += size
    return parts


def _axial_rope_tables(rows):
    row = jnp.repeat(jnp.arange(rows), GRID_W)
    col = jnp.tile(jnp.arange(GRID_W), rows)
    n_freq = HEAD_DIM // 4
    inv_freq = ROPE_BASE ** (-jnp.arange(n_freq, dtype=F32) / n_freq)
    ang = jnp.stack([row[:, None] * inv_freq, col[:, None] * inv_freq], axis=1)
    return jnp.cos(ang), jnp.sin(ang)


def _apply_rope(t, cos, sin):
    b, h, n, dh = t.shape
    tr = t.astype(F32).reshape(b, h, n, 2, 2, dh // 4)
    t1, t2 = tr[..., 0, :], tr[..., 1, :]
    out = jnp.stack([t1 * cos - t2 * sin, t2 * cos + t1 * sin], axis=-2)
    return out.reshape(b, h, n, dh).astype(t.dtype)


def _centred_dwconv(t, w):
    pad = CONV_K // 2
    n = t.shape[1]
    tp = jnp.pad(t, ((0, 0), (pad, pad), (0, 0)))
    out = tp[:, 0:n] * w[0]
    for j in range(1, CONV_K):
        out = out + tp[:, j:j + n] * w[j]
    return out


def _to_chunks(t):
    b, h, n = t.shape[:3]
    return t.astype(F32).reshape(b, h, n // CHUNK, CHUNK, *t.shape[3:])


def _scan_major(t):
    return jnp.moveaxis(t, 2, 0)


def _from_chunks(oc):
    nc, b, h, cl, d = oc.shape
    return jnp.moveaxis(oc, 0, 2).reshape(b, h, nc * cl, d)


def _mlstm_scan(q, k, v, log_i, log_f, state):
    tril = jnp.tril(jnp.ones((CHUNK, CHUNK), dtype=bool))
    xs = tuple(_scan_major(_to_chunks(t)) for t in (q, k, v, log_i, log_f))

    def step(carry, inp):
        c_mat, n_vec, m = carry
        qj, kj, vj, li, lf = inp
        b_cum = jnp.cumsum(lf, axis=-1)
        d_mat = jnp.where(tril, b_cum[..., :, None] - b_cum[..., None, :] + li[..., None, :], -jnp.inf)
        inter = m[..., None] + b_cum
        m_t = jnp.maximum(inter, jnp.max(d_mat, axis=-1))
        w_inter = jnp.exp(inter - m_t)
        s = jnp.einsum('bhtd,bhsd->bhts', qj, kj) * jnp.exp(d_mat - m_t[..., None])
        num = w_inter[..., None] * jnp.einsum('bhtd,bhde->bhte', qj, c_mat) + jnp.einsum('bhts,bhse->bhte', s, vj)
        den = w_inter * jnp.einsum('bhtd,bhd->bht', qj, n_vec) + jnp.sum(s, axis=-1)
        h_out = num / jnp.maximum(jnp.abs(den), jnp.exp(-m_t))[..., None]
        b_end = b_cum[..., -1]
        dec = b_end[..., None] - b_cum + li
        m_new = jnp.maximum(m + b_end, jnp.max(dec, axis=-1))
        w_state = jnp.exp(dec - m_new[..., None])
        carry_decay = jnp.exp(m + b_end - m_new)
        c_new = carry_decay[..., None, None] * c_mat + jnp.einsum('bhs,bhsd,bhse->bhde', w_state, kj, vj)
        n_new = carry_decay[..., None] * n_vec + jnp.einsum('bhs,bhsd->bhd', w_state, kj)
        return (c_new, n_new, m_new), h_out

    state, hc = lax.scan(step, state, xs)
    return _from_chunks(hc), state


def _retention_scan(q, k, v, state, log_gamma):
    lg = log_gamma.astype(F32)[:, None]
    idx = jnp.arange(CHUNK, dtype=F32)
    rel = idx[:, None] - idx[None, :]
    decay_mask = jnp.where(rel >= 0, jnp.exp(lg[..., None] * jnp.maximum(rel, 0.0)), 0.0)
    q_decay = jnp.exp(lg * (idx + 1.0))
    k_decay = jnp.exp(lg * (CHUNK - 1.0 - idx))
    chunk_decay = jnp.exp(lg[:, 0] * CHUNK)
    xs = tuple(_scan_major(_to_chunks(t)) for t in (q, k, v))

    def step(r_mat, inp):
        qj, kj, vj = inp
        s = jnp.einsum('bhtd,bhsd->bhts', qj, kj) * decay_mask
        o = jnp.einsum('bhts,bhse->bhte', s, vj) + q_decay[..., None] * jnp.einsum('bhtd,bhde->bhte', qj, r_mat)
        r_new = chunk_decay[:, None, None] * r_mat + jnp.einsum('bhsd,hs,bhse->bhde', kj, k_decay, vj)
        return r_new, o

    state, oc = lax.scan(step, state, xs)
    return _from_chunks(oc), state


def _gated_delta_scan(q, k, v, log_a, beta, state):
    qc, kc, vc = _to_chunks(q), _to_chunks(k), _to_chunks(v)
    g = jnp.cumsum(_to_chunks(log_a), axis=-1)
    bc = _to_chunks(beta)
    incl = jnp.tril(jnp.ones((CHUNK, CHUNK), dtype=bool))
    strict = jnp.tril(jnp.ones((CHUNK, CHUNK), dtype=bool), -1)
    decay = jnp.where(incl, jnp.exp(jnp.where(incl, g[..., :, None] - g[..., None, :], 0.0)), 0.0)
    kk = jnp.einsum('bhntd,bhnsd->bhnts', kc, kc)
    lower = jnp.where(strict, bc[..., :, None] * decay * kk, 0.0) + jnp.eye(CHUNK, dtype=F32)
    eg = jnp.exp(g)
    w_v = lax.linalg.triangular_solve(lower, bc[..., None] * vc, left_side=True, lower=True, unit_diagonal=True)
    w_k = lax.linalg.triangular_solve(lower, (bc * eg)[..., None] * kc, left_side=True, lower=True, unit_diagonal=True)
    qk = jnp.einsum('bhntd,bhnsd->bhnts', qc, kc) * decay
    k_end = jnp.exp(g[..., -1:] - g)
    g_end = eg[..., -1]
    xs = tuple(_scan_major(t) for t in (qc, kc, w_v, w_k, qk, eg, k_end, g_end))

    def step(s_mat, inp):
        qj, kj, wvj, wkj, qkj, egj, kej, gej = inp
        u = wvj - jnp.einsum('bhtd,bhde->bhte', wkj, s_mat)
        o = egj[..., None] * jnp.einsum('bhtd,bhde->bhte', qj, s_mat) + jnp.einsum('bhts,bhse->bhte', qkj, u)
        s_new = gej[..., None, None] * s_mat + jnp.einsum('bhs,bhsd,bhse->bhde', kej, kj, u)
        return s_new, o

    state, oc = lax.scan(step, state, xs)
    return _from_chunks(oc), state


def _flip_time(ts):
    return tuple(jnp.flip(t, axis=2) for t in ts)


def _bidir(scan_fns, ctx_dirs, lat_dirs, init):
    y_lat, y_ctx = None, None
    for d in range(2):
        ci, li = ctx_dirs[d], lat_dirs[d]
        if d == 1:
            ci, li = _flip_time(ci), _flip_time(li)
        h_c, state = scan_fns[d](*ci, init)
        h_l, _ = scan_fns[d](*li, state)
        if d == 1:
            h_c, h_l = jnp.flip(h_c, axis=2), jnp.flip(h_l, axis=2)
        y_lat = h_l if y_lat is None else y_lat + h_l
        y_ctx = h_c if y_ctx is None else y_ctx + h_c
    return y_lat, y_ctx


def _mlstm_branch(uc, ul, i_bias, f_bias):
    def prep(u):
        return (_heads(u['m_q']) * (HEAD_DIM ** -0.5), _heads(u['m_k']), _heads(u['m_v']))

    def gates(u, d):
        sl = slice(d * N_HEADS, (d + 1) * N_HEADS)
        log_i = _gate_heads(u['m_i'][..., sl]) + i_bias[d].astype(F32)[:, None]
        log_f = jax.nn.log_sigmoid(_gate_heads(u['m_f'][..., sl]) + f_bias[d].astype(F32)[:, None])
        return (log_i, log_f)

    b = ul['m_q'].shape[0]
    init = (jnp.zeros((b, N_HEADS, HEAD_DIM, HEAD_DIM), F32), jnp.zeros((b, N_HEADS, HEAD_DIM), F32),
            jnp.zeros((b, N_HEADS), F32))
    pc, pl = prep(uc), prep(ul)
    h_l, h_c = _bidir([_mlstm_scan, _mlstm_scan], [pc + gates(uc, d) for d in range(2)],
                      [pl + gates(ul, d) for d in range(2)], init)
    dt = ul['m_q'].dtype
    y_l = jax.nn.sigmoid(ul['m_o']) * _unheads(h_l).astype(dt)
    y_c = jax.nn.sigmoid(uc['m_o']) * _unheads(h_c).astype(dt)
    return y_l, y_c


def _retention_branch(uc, ul, log_gamma, rope):
    cos, sin = rope
    scale = HEAD_DIM ** -0.5
    pc = (_heads(uc['r_q']), _heads(uc['r_k']) * scale, _heads(uc['r_v']))
    pl = (_apply_rope(_heads(ul['r_q']), cos, sin), _apply_rope(_heads(ul['r_k']), cos, sin) * scale, _heads(ul['r_v']))
    b = ul['r_q'].shape[0]
    init = jnp.zeros((b, N_HEADS, HEAD_DIM, HEAD_DIM), F32)
    fns = [functools.partial(_retention_scan, log_gamma=log_gamma[d]) for d in range(2)]
    h_l, h_c = _bidir(fns, [pc, pc], [pl, pl], init)
    dt = ul['r_q'].dtype
    return _unheads(h_l).astype(dt), _unheads(h_c).astype(dt)


def _attention_branch(uc, ul, q_norm_w, k_norm_w, rope, need_ctx):
    cos, sin = rope
    scale = HEAD_DIM ** -0.5

    def prep(u):
        return (_rms_norm(_heads(u['a_q']), q_norm_w), _rms_norm(_heads(u['a_k']), k_norm_w), _heads(u['a_v']))

    q_l, k_l, v_l = prep(ul)
    q_l, k_l = _apply_rope(q_l, cos, sin), _apply_rope(k_l, cos, sin)
    q_c, k_c, v_c = prep(uc)
    b, _, n, dh = q_l.shape
    groups = N_HEADS // KV_HEADS
    keys = jnp.concatenate([k_c, k_l], axis=2)
    vals = jnp.concatenate([v_c, v_l], axis=2)

    def attend(q_blk, k_all, v_all):
        s = jnp.einsum('bkgqd,bksd->bkgqs', q_blk, k_all).astype(F32) * scale
        p = jax.nn.softmax(s, axis=-1).astype(v_all.dtype)
        return jnp.einsum('bkgqs,bksd->bkgqd', p, v_all)

    n_blk = n // Q_BLOCK
    q_blocks = q_l.reshape(b, KV_HEADS, groups, n_blk, Q_BLOCK, dh).transpose(3, 0, 1, 2, 4, 5)
    o = lax.map(lambda qb: attend(qb, keys, vals), q_blocks)
    y_l = _unheads(o.transpose(1, 2, 3, 0, 4, 5).reshape(b, N_HEADS, n, dh))
    y_c = None
    if need_ctx:
        n_c = q_c.shape[2]
        o_c = attend(q_c.reshape(b, KV_HEADS, groups, n_c, dh), k_c, v_c)
        y_c = _unheads(o_c.reshape(b, N_HEADS, n_c, dh))
    return y_l, y_c


def _deltanet_branch(uc, ul, conv_w, a_log, dt_bias):
    def prep(u):
        qkv = jax.nn.silu(_centred_dwconv(u['d_qkv'], conv_w))
        q, k, v = jnp.split(qkv, 3, axis=-1)
        return (_l2_norm(_heads(q)) * (HEAD_DIM ** -0.5), _l2_norm(_heads(k)), _heads(v))

    def gates(u, d):
        sl = slice(d * N_HEADS, (d + 1) * N_HEADS)
        dt_soft = jax.nn.softplus(_gate_heads(u['d_a'][..., sl]) + dt_bias[d].astype(F32)[:, None])
        log_a = -jnp.exp(a_log[d].astype(F32))[:, None] * dt_soft
        beta = jax.nn.sigmoid(_gate_heads(u['d_b'][..., sl]))
        return (log_a, beta)

    b = ul['d_qkv'].shape[0]
    init = jnp.zeros((b, N_HEADS, HEAD_DIM, HEAD_DIM), F32)
    pc, pl = prep(uc), prep(ul)
    h_l, h_c = _bidir([_gated_delta_scan, _gated_delta_scan], [pc + gates(uc, d) for d in range(2)],
                      [pl + gates(ul, d) for d in range(2)], init)
    dt = ul['d_qkv'].dtype
    return _unheads(h_l).astype(dt), _unheads(h_c).astype(dt)


def _merge(u, y_m, y_r, y_a, y_d, head_norm_w):
    return jnp.concatenate([
        _head_rms(y_m, head_norm_w[:GROUP_W]) * jax.nn.silu(u['m_z']),
        _head_rms(y_r, head_norm_w[GROUP_W:2 * GROUP_W]) * jax.nn.silu(u['r_z']),
        y_a * jax.nn.silu(u['a_z']),
        _head_rms(y_d, head_norm_w[2 * GROUP_W:]) * jax.nn.silu(u['d_z']),
    ], axis=-1)


def _layer(x, xc, c_silu, cc_silu, ada_w, ada_b, pre_w, post_w, w_in, w_out, m_ib, m_fb, r_lg,
           qn_w, kn_w, conv_w, a_log, dt_bias, hn_w, rope, need_ctx):
    mod_l = (c_silu @ ada_w + ada_b)[:, None, :]
    mod_c = cc_silu @ ada_w + ada_b
    sh_l, sc_l, gt_l = jnp.split(mod_l, 3, axis=-1)
    sh_c, sc_c, gt_c = jnp.split(mod_c, 3, axis=-1)
    h_l = _rms_norm(x, pre_w) * (1.0 + sc_l) + sh_l
    h_c = _rms_norm(xc, pre_w) * (1.0 + sc_c) + sh_c
    ul = _split_cols(h_l @ w_in)
    uc = _split_cols(h_c @ w_in)
    m_l, m_c = _mlstm_branch(uc, ul, m_ib, m_fb)
    r_l, r_c = _retention_branch(uc, ul, r_lg, rope)
    a_l, a_c = _attention_branch(uc, ul, qn_w, kn_w, rope, need_ctx)
    d_l, d_c = _deltanet_branch(uc, ul, conv_w, a_log, dt_bias)
    x = x + gt_l * _rms_norm(_merge(ul, m_l, r_l, a_l, d_l, hn_w) @ w_out, post_w)
    if need_ctx:
        xc = xc + gt_c * _rms_norm(_merge(uc, m_c, r_c, a_c, d_c, hn_w) @ w_out, post_w)
    return x, xc


def setup_inputs(seed: int = 0) -> dict:
    key = jax.random.key(seed)
    ks = jax.random.split(key, 20)

    def nrm(k, shape, s):
        return jax.random.normal(k, shape, F32) * s

    x = nrm(ks[0], (BATCH, SEQ, D_MODEL), 1.0)
    c = nrm(ks[1], (BATCH, D_MODEL), 1.0)
    ctx = nrm(ks[2], (BATCH, CTX_LEN, D_MODEL), 1.0)
    c_ctx = nrm(ks[3], (D_MODEL,), 1.0)
    ada_w = nrm(ks[4], (DEPTH, D_MODEL, 3 * D_MODEL), D_MODEL ** -0.5)
    ada_b = nrm(ks[5], (DEPTH, 3 * D_MODEL), 0.02)
    pre_norm_w = 1.0 + nrm(ks[6], (DEPTH, D_MODEL), 0.05)
    post_norm_w = 1.0 + nrm(ks[7], (DEPTH, D_MODEL), 0.05)
    w_in = nrm(ks[8], (DEPTH, D_MODEL, IN_W), D_MODEL ** -0.5)
    w_out = nrm(ks[9], (DEPTH, MIX_W, D_MODEL), MIX_W ** -0.5)
    mlstm_i_bias = nrm(ks[10], (DEPTH, 2, N_HEADS), 0.1)
    mlstm_f_bias = jnp.linspace(3.0, 6.0, N_HEADS, dtype=F32) + nrm(ks[11], (DEPTH, 2, N_HEADS), 0.1)
    base_lg = jnp.log(1.0 - 2.0 ** (-5.0 - jnp.arange(N_HEADS, dtype=F32)))
    ret_log_gamma = base_lg * jnp.exp(nrm(ks[12], (DEPTH, 2, N_HEADS), 0.1))
    attn_q_norm_w = 1.0 + nrm(ks[13], (DEPTH, HEAD_DIM), 0.05)
    attn_k_norm_w = 1.0 + nrm(ks[14], (DEPTH, HEAD_DIM), 0.05)
    dn_conv_w = nrm(ks[15], (DEPTH, CONV_K, 3 * GROUP_W), CONV_K ** -0.5)
    dn_a_log = jnp.log(jax.random.uniform(ks[16], (DEPTH, 2, N_HEADS), F32, 1.0, 16.0))
    dt0 = jnp.exp(jax.random.uniform(ks[17], (DEPTH, 2, N_HEADS), F32, math.log(1e-3), math.log(1e-1)))
    dn_dt_bias = dt0 + jnp.log(-jnp.expm1(-dt0))
    head_norm_w = 1.0 + nrm(ks[18], (DEPTH, 3 * GROUP_W), 0.05)
    return {'x': x, 'c': c, 'ctx': ctx, 'c_ctx': c_ctx, 'ada_w': ada_w, 'ada_b': ada_b,
            'pre_norm_w': pre_norm_w, 'post_norm_w': post_norm_w, 'w_in': w_in, 'w_out': w_out,
            'mlstm_i_bias': mlstm_i_bias, 'mlstm_f_bias': mlstm_f_bias, 'ret_log_gamma': ret_log_gamma,
            'attn_q_norm_w': attn_q_norm_w, 'attn_k_norm_w': attn_k_norm_w, 'dn_conv_w': dn_conv_w,
            'dn_a_log': dn_a_log, 'dn_dt_bias': dn_dt_bias, 'head_norm_w': head_norm_w}


def reference(x, c, ctx, c_ctx, ada_w, ada_b, pre_norm_w, post_norm_w, w_in, w_out, mlstm_i_bias,
              mlstm_f_bias, ret_log_gamma, attn_q_norm_w, attn_k_norm_w, dn_conv_w, dn_a_log,
              dn_dt_bias, head_norm_w):
    n_tokens = x.shape[1]
    rows = n_tokens // GRID_W
    rope = _axial_rope_tables(rows)
    c_silu = jax.nn.silu(c)
    cc_silu = jax.nn.silu(c_ctx)
    xc = ctx
    for layer in range(DEPTH):
        x, xc = _layer(x, xc, c_silu, cc_silu, ada_w[layer], ada_b[layer], pre_norm_w[layer],
                       post_norm_w[layer], w_in[layer], w_out[layer], mlstm_i_bias[layer],
                       mlstm_f_bias[layer], ret_log_gamma[layer], attn_q_norm_w[layer],
                       attn_k_norm_w[layer], dn_conv_w[layer], dn_a_log[layer], dn_dt_bias[layer],
                       head_norm_w[layer], rope, layer < DEPTH - 1)
    return x
```

```python
import functools

import jax
import jax.numpy as jnp
from jax import lax
from jax.experimental import pallas as pl
from jax.experimental.pallas import tpu as pltpu

F32 = jnp.float32
BF16 = jnp.bfloat16
HI = lax.Precision.HIGHEST

D_MODEL = 2048
GROUP_W = 512
N_HEADS = 4
HEAD_DIM = 128
KV_HEADS = 2
GRID_W = 64
CONV_K = 5
ROPE_BASE = 10000.0
EPS = 1e-6
SCALE = HEAD_DIM ** -0.5
NEG = -1e30

LANES = 128
CH = 256
DC = 64
N_MAIN = 64 * LANES
VMEM_LIMIT = 56 * 1024 * 1024

CB = dict(m_q=0, m_k=4, m_v=8, m_o=12, m_z=16, r_q=20, r_k=24, r_v=28, r_z=32,
          a_q=36, a_k=40, a_v=42, a_z=44, d_q=48, d_k=52, d_v=56, d_z=60)


def _dot(a, b, prec=None):
    return jnp.dot(a, b, preferred_element_type=F32, precision=prec)


def _dot_nt(a, b, prec=None):
    return lax.dot_general(a, b, (((1,), (1,)), ((), ())), preferred_element_type=F32, precision=prec)


def _dot_tn(a, b, prec=None):
    return lax.dot_general(a, b, (((0,), (0,)), ((), ())), preferred_element_type=F32, precision=prec)


def _sigmoid(x):
    return 1.0 / (1.0 + jnp.exp(-x))


def _lane_col(blk, idx):
    lane = lax.broadcasted_iota(jnp.int32, blk.shape, 1)
    return jnp.sum(jnp.where(lane == idx, blk, 0.0), axis=1, keepdims=True)


def _sub_row(blk, idx):
    sub = lax.broadcasted_iota(jnp.int32, blk.shape, 0)
    return jnp.sum(jnp.where(sub == idx, blk, 0.0), axis=0, keepdims=True)


def _rope(t, cos, sin_signed):
    lane = lax.broadcasted_iota(jnp.int32, t.shape, 1)
    partner = jnp.where((lane // 32) % 2 == 0, pltpu.roll(t, 96, 1), pltpu.roll(t, 32, 1))
    return t * cos + partner * sin_signed


def _params(sem):
    return pltpu.CompilerParams(dimension_semantics=sem, vmem_limit_bytes=VMEM_LIMIT)


def _mod_kernel(c_ref, w_ref, b_ref, o_ref):
    c = c_ref[...]
    o_ref[...] = _dot(c * _sigmoid(c), w_ref[...], HI) + b_ref[...]


def _modulation(c8, ada_w, ada_b):
    depth, d, n3 = ada_w.shape
    tn = 768
    return pl.pallas_call(
        _mod_kernel,
        grid=(depth, n3 // tn),
        in_specs=[pl.BlockSpec((8, d), lambda l, n: (0, 0)),
                  pl.BlockSpec((None, d, tn), lambda l, n: (l, 0, n)),
                  pl.BlockSpec((None, 1, tn), lambda l, n: (l, 0, n))],
        out_specs=pl.BlockSpec((None, 8, tn), lambda l, n: (l, 0, n)),
        out_shape=jax.ShapeDtypeStruct((depth, 8, n3), F32),
        compiler_params=_params(("parallel", "parallel")),
        name="adaln_mod",
    )(c8, ada_w, ada_b.reshape(depth, 1, n3))


def _inproj_kernel(x_ref, mod_ref, pw_ref, wm_ref, wg_ref, u_ref, g_ref, h_scr, *, tm, ctx_len):
    b = pl.program_id(0)
    i = pl.program_id(1)
    n = pl.program_id(2)

    @pl.when(n == 0)
    def _():
        x = x_ref[...]
        y = x * lax.rsqrt(jnp.mean(x * x, axis=-1, keepdims=True) + EPS) * pw_ref[...]
        row = i * tm + lax.broadcasted_iota(jnp.int32, (tm, 1), 0)
        is_ctx = row < ctx_len
        d = D_MODEL
        sh = jnp.where(is_ctx, mod_ref[4:5, 0:d], mod_ref[pl.ds(b, 1), 0:d])
        sc = jnp.where(is_ctx, mod_ref[4:5, d:2 * d], mod_ref[pl.ds(b, 1), d:2 * d])
        hh = (y * (1.0 + sc) + sh).astype(BF16)
        h_scr[...] = hh
        g_ref[...] = _dot(hh, wg_ref[...])

    u_ref[...] = _dot(h_scr[...], wm_ref[...]).astype(BF16)


def _in_projection(xa, mod, pre_w, w_main, w_gate, ctx_len):
    bsz, t, d = xa.shape
    tm = t // 4 if (t % 4 == 0 and (t // 4) % 16 == 0) else CH
    tn = 1024
    kern = functools.partial(_inproj_kernel, tm=tm, ctx_len=ctx_len)
    return pl.pallas_call(
        kern,
        grid=(bsz, t // tm, N_MAIN // tn),
        in_specs=[pl.BlockSpec((None, tm, d), lambda b, i, n: (b, i, 0)),
                  pl.BlockSpec((8, 3 * d), lambda b, i, n: (0, 0)),
                  pl.BlockSpec((1, d), lambda b, i, n: (0, 0)),
                  pl.BlockSpec((d, tn), lambda b, i, n: (0, n)),
                  pl.BlockSpec((d, LANES), lambda b, i, n: (0, 0))],
        out_specs=[pl.BlockSpec((None, tm, tn), lambda b, i, n: (b, i, n)),
                   pl.BlockSpec((None, tm, LANES), lambda b, i, n: (b, i, 0))],
        out_shape=[jax.ShapeDtypeStruct((bsz, t, N_MAIN), BF16),
                   jax.ShapeDtypeStruct((bsz, t, LANES), F32)],
        scratch_shapes=[pltpu.VMEM((tm, d), BF16)],
        compiler_params=_params(("parallel", "parallel", "arbitrary")),
        name="in_proj",
    )(xa, mod, pre_w.reshape(1, d), w_main, w_gate)


def _gate_kernel(g_ref, bias_ref, alog_ref, a_ref):
    x = g_ref[...] + bias_ref[...]
    lane = lax.broadcasted_iota(jnp.int32, x.shape, 1)
    l1p = jnp.log(1.0 + jnp.exp(-jnp.abs(x)))
    log_f = jnp.minimum(x, 0.0) - l1p
    log_a = -jnp.exp(alog_ref[...]) * (jnp.maximum(x, 0.0) + l1p)
    v = jnp.where(lane < 8, x, jnp.where(lane < 16, log_f, jnp.where(lane < 24, log_a, _sigmoid(x))))
    r = lax.broadcasted_iota(jnp.int32, (CH, CH), 0)
    c = lax.broadcasted_iota(jnp.int32, (CH, CH), 1)
    same = (r // DC) == (c // DC)
    pre = (c <= r).astype(F32)
    suf = (c >= r).astype(F32)
    pre_dc = jnp.where(same, pre, 0.0)
    suf_dc = jnp.where(same, suf, 0.0)
    fwd = (lane % 8) < 4
    cum_ch = jnp.where(fwd, _dot(pre, v, HI), _dot(suf, v, HI))
    cum_dc = jnp.where(fwd, _dot(pre_dc, v, HI), _dot(suf_dc, v, HI))
    a_ref[...] = jnp.where((lane >= 8) & (lane < 16), cum_ch, jnp.where((lane >= 16) & (lane < 24), cum_dc, v))


def _gate_prep(g, bias_row, alog_row):
    bsz, t, _ = g.shape
    return pl.pallas_call(
        _gate_kernel,
        grid=(bsz, t // CH),
        in_specs=[pl.BlockSpec((None, CH, LANES), lambda b, i: (b, i, 0)),
                  pl.BlockSpec((1, LANES), lambda b, i: (0, 0)),
                  pl.BlockSpec((1, LANES), lambda b, i: (0, 0))],
        out_specs=pl.BlockSpec((None, CH, LANES), lambda b, i: (b, i, 0)),
        out_shape=jax.ShapeDtypeStruct((bsz, t, LANES), F32),
        compiler_params=_params(("parallel", "parallel")),
        name="gate_prep",
    )(g, bias_row, alog_row)


def _chunk_order(j, d, n_chunks, n_ctx_chunks):
    if d == 0:
        return j
    return jnp.where(j < n_ctx_chunks, n_ctx_chunks - 1 - j, n_chunks + n_ctx_chunks - 1 - j)


def _head_spec(t, name):
    base = CB[name]
    return pl.BlockSpec((None, t, LANES), lambda b, h: (b, 0, base + h))


def _mlstm_kernel(q_ref, k_ref, v_ref, o_ref, a_ref, at_ref, y_ref, c_scr, *, n_chunks, n_ctx_chunks):
    h = pl.program_id(1)
    r = lax.broadcasted_iota(jnp.int32, (CH, CH), 0)
    c = lax.broadcasted_iota(jnp.int32, (CH, CH), 1)
    one_col = (lax.broadcasted_iota(jnp.int32, (CH, LANES), 1) == 0).astype(F32)

    for d in range(2):
        mask = (c <= r) if d == 0 else (c >= r)
        last = CH - 1 if d == 0 else 0
        c_scr[...] = jnp.zeros_like(c_scr)

        def step(j, m, d=d, mask=mask, last=last):
            ci = _chunk_order(j, d, n_chunks, n_ctx_chunks)
            rows = pl.ds(pl.multiple_of(ci * CH, CH), CH)
            q = q_ref[rows, :]
            k = k_ref[rows, :]
            v = v_ref[rows, :].astype(F32)
            a_blk = a_ref[rows, :]
            at_blk = at_ref[ci]
            li_col = _lane_col(a_blk, d * 4 + h)
            b_col = _lane_col(a_blk, 8 + d * 4 + h)
            li_row = _sub_row(at_blk, d * 4 + h)
            b_row = _sub_row(at_blk, 8 + d * 4 + h)

            dm = jnp.where(mask, b_col - b_row + li_row, NEG)
            inter = m + b_col
            m_t = jnp.maximum(inter, jnp.max(dm, axis=1, keepdims=True))
            w_inter = jnp.exp(inter - m_t)
            s = _dot_nt(q, k) * (jnp.exp(dm - m_t) * SCALE)
            v_aug = jnp.concatenate([v, one_col], axis=1)
            nd = (w_inter * SCALE) * _dot(q, c_scr[...].astype(BF16)) + _dot(s.astype(BF16), v_aug.astype(BF16))
            num = nd[:, :LANES]
            den = nd[:, LANES:LANES + 1]
            h_out = num / jnp.maximum(jnp.abs(den), jnp.exp(-m_t))

            b_end = b_col[last:last + 1, :]
            dec = b_end - b_col + li_col
            m_new = jnp.maximum(m + b_end, jnp.max(dec, axis=0, keepdims=True))
            w_state = jnp.exp(dec - m_new)
            c_scr[...] = jnp.exp(m + b_end - m_new) * c_scr[...] + _dot_tn(k, (w_state * v_aug).astype(BF16))

            if d == 0:
                y_ref[rows, :] = h_out
            else:
                y_ref[rows, :] = (y_ref[rows, :] + h_out) * _sigmoid(o_ref[rows, :].astype(F32))
            return m_new

        lax.fori_loop(0, n_chunks, step, jnp.zeros((1, 1), F32))


def _mlstm(u, a, at_ch, ctx_len):
    bsz, t, _ = u.shape
    n_chunks = t // CH
    kern = functools.partial(_mlstm_kernel, n_chunks=n_chunks, n_ctx_chunks=ctx_len // CH)
    return pl.pallas_call(
        kern,
        grid=(bsz, N_HEADS),
        in_specs=[_head_spec(t, "m_q"), _head_spec(t, "m_k"), _head_spec(t, "m_v"), _head_spec(t, "m_o"),
                  pl.BlockSpec((None, t, LANES), lambda b, h: (b, 0, 0)),
                  pl.BlockSpec((None, n_chunks, 32, CH), lambda b, h: (b, 0, 0, 0))],
        out_specs=pl.BlockSpec((None, t, LANES), lambda b, h: (b, 0, h)),
        out_shape=jax.ShapeDtypeStruct((bsz, t, GROUP_W), F32),
        scratch_shapes=[pltpu.VMEM((HEAD_DIM, 2 * LANES), F32)],
        compiler_params=_params(("parallel", "parallel")),
        name="mlstm",
    )(u, u, u, u, a, at_ch)


def _retention_kernel(lg_ref, q_ref, k_ref, v_ref, cos_ref, sin_ref, y_ref, qp_scr, kp_scr, r_scr, *,
                      n_chunks, n_ctx_chunks):
    h = pl.program_id(1)

    def prep(ci, carry):
        rows = pl.ds(pl.multiple_of(ci * CH, CH), CH)
        cos = cos_ref[rows, :]
        sin = sin_ref[rows, :]
        qp_scr[rows, :] = _rope(q_ref[rows, :].astype(F32), cos, sin).astype(BF16)
        kp_scr[rows, :] = (_rope(k_ref[rows, :].astype(F32), cos, sin) * SCALE).astype(BF16)
        return carry

    lax.fori_loop(0, n_chunks, prep, 0)

    r = lax.broadcasted_iota(jnp.int32, (CH, CH), 0).astype(F32)
    c = lax.broadcasted_iota(jnp.int32, (CH, CH), 1).astype(F32)
    pos = lax.broadcasted_iota(jnp.int32, (CH, 1), 0).astype(F32)

    for d in range(2):
        lg = lg_ref[d * N_HEADS + h]
        rel = (r - c) if d == 0 else (c - r)
        decay = jnp.where(rel >= 0, jnp.exp(lg * jnp.maximum(rel, 0.0)), 0.0)
        p_vis = pos if d == 0 else (CH - 1.0) - pos
        q_decay = jnp.exp(lg * (p_vis + 1.0))
        k_decay = jnp.exp(lg * ((CH - 1.0) - p_vis))
        chunk_decay = jnp.exp(lg * CH)
        r_scr[...] = jnp.zeros_like(r_scr)

        def step(j, carry, d=d, decay=decay, q_decay=q_decay, k_decay=k_decay, chunk_decay=chunk_decay):
            ci = _chunk_order(j, d, n_chunks, n_ctx_chunks)
            rows = pl.ds(pl.multiple_of(ci * CH, CH), CH)
            q = qp_scr[rows, :]
            k = kp_scr[rows, :]
            v = v_ref[rows, :]
            s = _dot_nt(q, k) * decay
            o = _dot(s.astype(BF16), v) + q_decay * _dot(q, r_scr[...].astype(BF16))
            kd = (k.astype(F32) * k_decay).astype(BF16)
            r_scr[...] = chunk_decay * r_scr[...] + _dot_tn(kd, v)
            if d == 0:
                y_ref[rows, :] = o
            else:
                y_ref[rows, :] = y_ref[rows, :] + o
            return carry

        lax.fori_loop(0, n_chunks, step, 0)


def _retention(u, log_gamma, cos_t, sin_t, ctx_len):
    bsz, t, _ = u.shape
    n_chunks = t // CH
    kern = functools.partial(_retention_kernel, n_chunks=n_chunks, n_ctx_chunks=ctx_len // CH)
    tab = pl.BlockSpec((t, LANES), lambda b, h: (0, 0))
    return pl.pallas_call(
        kern,
        grid=(bsz, N_HEADS),
        in_specs=[pl.BlockSpec(memory_space=pltpu.SMEM),
                  _head_spec(t, "r_q"), _head_spec(t, "r_k"), _head_spec(t, "r_v"), tab, tab],
        out_specs=pl.BlockSpec((None, t, LANES), lambda b, h: (b, 0, h)),
        out_shape=jax.ShapeDtypeStruct((bsz, t, GROUP_W), F32),
        scratch_shapes=[pltpu.VMEM((t, LANES), BF16), pltpu.VMEM((t, LANES), BF16),
                        pltpu.VMEM((HEAD_DIM, HEAD_DIM), F32)],
        compiler_params=_params(("parallel", "parallel")),
        name="retention",
    )(log_gamma.reshape(2 * N_HEADS), u, u, u, cos_t, sin_t)


def _attn_kernel(q_ref, k_ref, v_ref, cos_ref, sin_ref, cosq_ref, sinq_ref, qw_ref, kw_ref, o_ref, kp_scr, *,
                 t, ctx_len):
    qi = pl.program_id(2)

    @pl.when(qi == 0)
    def _():
        def prep(ci, carry):
            rows = pl.ds(pl.multiple_of(ci * CH, CH), CH)
            k = k_ref[rows, :].astype(F32)
            k = k * lax.rsqrt(jnp.mean(k * k, axis=-1, keepdims=True) + EPS) * kw_ref[...]
            kp_scr[rows, :] = _rope(k, cos_ref[rows, :], sin_ref[rows, :]).astype(BF16)
            return carry

        lax.fori_loop(0, t // CH, prep, 0)

    qs = []
    for g in range(2):
        q = q_ref[:, g * LANES:(g + 1) * LANES].astype(F32)
        q = q * lax.rsqrt(jnp.mean(q * q, axis=-1, keepdims=True) + EPS) * qw_ref[...]
        qs.append((_rope(q, cosq_ref[...], sinq_ref[...]) * SCALE).astype(BF16))
    qq = jnp.concatenate(qs, axis=0)

    def attend(n_keys):
        s = _dot_nt(qq, kp_scr[0:n_keys, :])
        p = jnp.exp(s - jnp.max(s, axis=-1, keepdims=True))
        o = _dot(p.astype(BF16), v_ref[0:n_keys, :]) / jnp.sum(p, axis=-1, keepdims=True)
        o_ref[:, 0:LANES] = o[:CH]
        o_ref[:, LANES:2 * LANES] = o[CH:]

    @pl.when(qi == 0)
    def _():
        attend(ctx_len)

    @pl.when(qi > 0)
    def _():
        attend(t)


def _attention(u, cos_t, sin_t, qn_w, kn_w, ctx_len):
    bsz, t, _ = u.shape
    assert ctx_len == CH
    kern = functools.partial(_attn_kernel, t=t, ctx_len=ctx_len)
    qb = CB["a_q"] // 2
    kb = CB["a_k"]
    vb = CB["a_v"]
    tab = pl.BlockSpec((t, LANES), lambda b, kv, i: (0, 0))
    tabq = pl.BlockSpec((CH, LANES), lambda b, kv, i: (i, 0))
    vec = pl.BlockSpec((1, LANES), lambda b, kv, i: (0, 0))
    return pl.pallas_call(
        kern,
        grid=(bsz, KV_HEADS, t // CH),
        in_specs=[pl.BlockSpec((None, CH, 2 * LANES), lambda b, kv, i: (b, i, qb + kv)),
                  pl.BlockSpec((None, t, LANES), lambda b, kv, i: (b, 0, kb + kv)),
                  pl.BlockSpec((None, t, LANES), lambda b, kv, i: (b, 0, vb + kv)),
                  tab, tab, tabq, tabq, vec, vec],
        out_specs=pl.BlockSpec((None, CH, 2 * LANES), lambda b, kv, i: (b, i, kv)),
        out_shape=jax.ShapeDtypeStruct((bsz, t, GROUP_W), F32),
        scratch_shapes=[pltpu.VMEM((t, LANES), BF16)],
        compiler_params=_params(("parallel", "parallel", "arbitrary")),
        name="attention",
    )(u, u, u, cos_t, sin_t, cos_t, sin_t, qn_w.reshape(1, LANES), kn_w.reshape(1, LANES))


def _tri_inverse(n_mat):
    r = lax.broadcasted_iota(jnp.int32, (DC, DC), 0)
    c = lax.broadcasted_iota(jnp.int32, (DC, DC), 1)
    eye = (r == c).astype(F32)
    m = -jnp.where((r // 16) == (c // 16), n_mat, 0.0)
    p = eye + m
    mp = m
    for _ in range(3):
        mp = _dot(mp, mp, HI)
        p = p + _dot(p, mp, HI)
    for w in (32, 64):
        off = ((r // w) == (c // w)) & ((r // (w // 2)) != (c // (w // 2)))
        p = p - _dot(p, _dot(jnp.where(off, n_mat, 0.0), p, HI), HI)
    return p


def _deltanet_kernel(q_ref, k_ref, v_ref, wq_ref, wk_ref, wv_ref, a_ref, at_ref, y_ref,
                     xs_scr, qd_scr, kd_scr, vd_scr, qw_scr, wv_scr, ke_scr, qk_scr, ge_scr, *,
                     t, ctx_len):
    h = pl.program_id(1)
    n_chunks = t // DC
    n_ctx_chunks = ctx_len // DC
    pad = 8

    def prep_stream(src_ref, w_ref, dst_scr, l2, scale):
        zeros = jnp.zeros((pad, LANES), F32)
        xs_scr[0:pad, :] = zeros
        xs_scr[t + pad:t + 2 * pad, :] = zeros

        def load(ci, carry):
            rows = pl.ds(pl.multiple_of(ci * CH, CH), CH)
            xs_scr[pl.ds(pl.multiple_of(ci * CH + pad, 8), CH), :] = src_ref[rows, :].astype(F32)
            return carry

        lax.fori_loop(0, t // CH, load, 0)
        w = w_ref[...]

        def conv(ci, carry):
            win = xs_scr[pl.ds(pl.multiple_of(ci * CH, CH), CH + 2 * pad), :]
            tok = ci * CH - pad + lax.broadcasted_iota(jnp.int32, (CH + 2 * pad, 1), 0)
            win = jnp.where((tok < ctx_len) == (ci * CH < ctx_len), win, 0.0)
            acc = win[pad - 2:pad - 2 + CH, :] * w[0:1, :]
            for j in range(1, CONV_K):
                acc = acc + win[pad - 2 + j:pad - 2 + j + CH, :] * w[j:j + 1, :]
            acc = acc * _sigmoid(acc)
            if l2:
                acc = acc * lax.rsqrt(jnp.sum(acc * acc, axis=-1, keepdims=True) + EPS) * scale
            dst_scr[pl.ds(pl.multiple_of(ci * CH, CH), CH), :] = acc
            return carry

        lax.fori_loop(0, t // CH, conv, 0)

    prep_stream(q_ref, wq_ref, qd_scr, True, SCALE)
    prep_stream(k_ref, wk_ref, kd_scr, True, 1.0)
    prep_stream(v_ref, wv_ref, vd_scr, False, 1.0)

    r = lax.broadcasted_iota(jnp.int32, (DC, DC), 0)
    c = lax.broadcasted_iota(jnp.int32, (DC, DC), 1)

    for d in range(2):
        incl = (c <= r) if d == 0 else (c >= r)
        strict = (c < r) if d == 0 else (c > r)
        last = DC - 1 if d == 0 else 0

        def pre(ci, carry, d=d, incl=incl, strict=strict, last=last):
            rows = pl.ds(pl.multiple_of(ci * DC, DC), DC)
            q = qd_scr[rows, :]
            k = kd_scr[rows, :]
            v = vd_scr[rows, :]
            a_blk = a_ref[rows, :]
            g_col = _lane_col(a_blk, 16 + d * 4 + h)
            beta = _lane_col(a_blk, 24 + d * 4 + h)
            g_row = _sub_row(at_ref[ci], 16 + d * 4 + h)
            decay = jnp.exp(jnp.where(incl, g_col - g_row, NEG))
            kb = k.astype(BF16)
            n_mat = jnp.where(strict, beta * decay * _dot_nt(kb, kb), 0.0)
            t_inv = _tri_inverse(n_mat)
            eg = jnp.exp(g_col)
            g_end = g_col[last:last + 1, :]
            w = _dot(t_inv, jnp.concatenate([beta * v, (beta * eg) * k], axis=1), HI)
            qw_scr[ci, 0:DC, :] = (eg * q).astype(BF16)
            qw_scr[ci, DC:2 * DC, :] = w[:, LANES:].astype(BF16)
            wv_scr[ci] = w[:, :LANES]
            ke_scr[ci] = (jnp.exp(g_end - g_col) * k).astype(BF16)
            qk_scr[ci] = (_dot_nt(q.astype(BF16), kb) * decay).astype(BF16)
            ge_scr[ci] = jnp.broadcast_to(jnp.exp(g_end), (8, LANES))
            return carry

        lax.fori_loop(0, n_chunks, pre, 0)

        def scan(j, s_mat, d=d):
            ci = _chunk_order(j, d, n_chunks, n_ctx_chunks)
            rows = pl.ds(pl.multiple_of(ci * DC, DC), DC)
            x = _dot(qw_scr[ci], s_mat.astype(BF16))
            ub = (wv_scr[ci] - x[DC:, :]).astype(BF16)
            o = x[:DC, :] + _dot(qk_scr[ci], ub)
            if d == 0:
                y_ref[rows, :] = o
            else:
                y_ref[rows, :] = y_ref[rows, :] + o
            return ge_scr[ci][0:1, :] * s_mat + _dot_tn(ke_scr[ci], ub)

        lax.fori_loop(0, n_chunks, scan, jnp.zeros((HEAD_DIM, HEAD_DIM), F32))


def _deltanet(u, conv_w, a, at_dc, ctx_len):
    bsz, t, _ = u.shape
    n_chunks = t // DC
    kern = functools.partial(_deltanet_kernel, t=t, ctx_len=ctx_len)

    def wspec(off):
        return pl.BlockSpec((CONV_K, LANES), lambda b, h: (0, off + h))

    return pl.pallas_call(
        kern,
        grid=(bsz, N_HEADS),
        in_specs=[_head_spec(t, "d_q"), _head_spec(t, "d_k"), _head_spec(t, "d_v"),
                  wspec(0), wspec(N_HEADS), wspec(2 * N_HEADS),
                  pl.BlockSpec((None, t, LANES), lambda b, h: (b, 0, 0)),
                  pl.BlockSpec((None, n_chunks, 32, DC), lambda b, h: (b, 0, 0, 0))],
        out_specs=pl.BlockSpec((None, t, LANES), lambda b, h: (b, 0, h)),
        out_shape=jax.ShapeDtypeStruct((bsz, t, GROUP_W), F32),
        scratch_shapes=[pltpu.VMEM((t + 16, LANES), F32),
                        pltpu.VMEM((t, LANES), F32), pltpu.VMEM((t, LANES), F32), pltpu.VMEM((t, LANES), F32),
                        pltpu.VMEM((n_chunks, 2 * DC, LANES), BF16),
                        pltpu.VMEM((n_chunks, DC, LANES), F32),
                        pltpu.VMEM((n_chunks, DC, LANES), BF16),
                        pltpu.VMEM((n_chunks, DC, DC), BF16),
                        pltpu.VMEM((n_chunks, 8, LANES), F32)],
        compiler_params=_params(("parallel", "parallel")),
        name="deltanet",
    )(u, u, u, conv_w, conv_w, conv_w, a, at_dc)


def _out_kernel(x_ref, ym_ref, yr_ref, ya_ref, yd_ref, zm_ref, zr_ref, za_ref, zd_ref, hn_ref, w_ref,
                mod_ref, pw_ref, o_ref):
    b = pl.program_id(0)
    i = pl.program_id(1)

    def silu(z_ref):
        z = z_ref[...].astype(F32)
        return z * _sigmoid(z)

    def head_rms(y_ref, g):
        y = y_ref[...]
        parts = []
        for hh in range(N_HEADS):
            p = y[:, hh * LANES:(hh + 1) * LANES]
            parts.append(p * lax.rsqrt(jnp.mean(p * p, axis=-1, keepdims=True) + EPS))
        return jnp.concatenate(parts, axis=1) * hn_ref[:, g * GROUP_W:(g + 1) * GROUP_W]

    merged = jnp.concatenate([
        (head_rms(ym_ref, 0) * silu(zm_ref)).astype(BF16),
        (head_rms(yr_ref, 1) * silu(zr_ref)).astype(BF16),
        (ya_ref[...] * silu(za_ref)).astype(BF16),
        (head_rms(yd_ref, 2) * silu(zd_ref)).astype(BF16)], axis=1)
    o = _dot(merged, w_ref[...])
    o = o * lax.rsqrt(jnp.mean(o * o, axis=-1, keepdims=True) + EPS) * pw_ref[...]
    row = jnp.where(i == 0, 4, b)
    gate = mod_ref[pl.ds(row, 1), 2 * D_MODEL:3 * D_MODEL]
    o_ref[...] = x_ref[...] + gate * o


def _out_projection(xa, ys, u, hn_w, w_out, mod, post_w, ctx_len):
    bsz, t, d = xa.shape
    assert ctx_len == CH
    tok = lambda b, i: (b, i, 0)
    yspec = pl.BlockSpec((None, CH, GROUP_W), tok)

    def zspec(name):
        blk = CB[name] // N_HEADS
        return pl.BlockSpec((None, CH, GROUP_W), lambda b, i: (b, i, blk))

    return pl.pallas_call(
        _out_kernel,
        grid=(bsz, t // CH),
        in_specs=[pl.BlockSpec((None, CH, d), tok), yspec, yspec, yspec, yspec,
                  zspec("m_z"), zspec("r_z"), zspec("a_z"), zspec("d_z"),
                  pl.BlockSpec((1, 3 * GROUP_W), lambda b, i: (0, 0)),
                  pl.BlockSpec((d, d), lambda b, i: (0, 0)),
                  pl.BlockSpec((8, 3 * d), lambda b, i: (0, 0)),
                  pl.BlockSpec((1, d), lambda b, i: (0, 0))],
        out_specs=pl.BlockSpec((None, CH, d), tok),
        out_shape=jax.ShapeDtypeStruct((bsz, t, d), F32),
        compiler_params=_params(("parallel", "parallel")),
        name="out_proj",
    )(xa, *ys, u, u, u, u, hn_w.reshape(1, 3 * GROUP_W), w_out, mod, post_w.reshape(1, d))


def _reorder_w_in(w):
    g = GROUP_W
    m_gate = 5 * g
    r_start = m_gate + 4 * N_HEADS
    d_gate = r_start + 4 * g + (2 * g + 2 * KV_HEADS * HEAD_DIM) + 4 * g
    w_main = jnp.concatenate([w[:, :m_gate], w[:, r_start:d_gate]], axis=1).astype(BF16)
    gates = jnp.concatenate([w[:, m_gate:r_start], w[:, d_gate:]], axis=1)
    w_gate = jnp.pad(gates, ((0, 0), (0, LANES - gates.shape[1]))).astype(BF16)
    return w_main, w_gate


def _rope_tables(seq, ctx_len):
    rows = seq // GRID_W
    row = jnp.repeat(jnp.arange(rows), GRID_W)
    col = jnp.tile(jnp.arange(GRID_W), rows)
    n_freq = HEAD_DIM // 4
    inv_freq = ROPE_BASE ** (-jnp.arange(n_freq, dtype=F32) / n_freq)
    ar = row[:, None] * inv_freq
    ac = col[:, None] * inv_freq
    cos = jnp.concatenate([jnp.cos(ar), jnp.cos(ar), jnp.cos(ac), jnp.cos(ac)], axis=1)
    sin = jnp.concatenate([-jnp.sin(ar), jnp.sin(ar), -jnp.sin(ac), jnp.sin(ac)], axis=1)
    cos = jnp.concatenate([jnp.ones((ctx_len, HEAD_DIM), F32), cos], axis=0)
    sin = jnp.concatenate([jnp.zeros((ctx_len, HEAD_DIM), F32), sin], axis=0)
    return cos, sin


def kernel(x, c, ctx, c_ctx, ada_w, ada_b, pre_norm_w, post_norm_w, w_in, w_out, mlstm_i_bias, mlstm_f_bias,
           ret_log_gamma, attn_q_norm_w, attn_k_norm_w, dn_conv_w, dn_a_log, dn_dt_bias, head_norm_w):
    bsz, seq, d = x.shape
    ctx_len = ctx.shape[1]
    t = ctx_len + seq
    depth = ada_w.shape[0]
    assert bsz <= 4 and ctx_len % CH == 0 and seq % CH == 0

    xa = jnp.concatenate([ctx, x], axis=1)
    c8 = jnp.zeros((8, d), F32).at[:bsz].set(c).at[4].set(c_ctx)
    mod = _modulation(c8, ada_w, ada_b)
    cos_t, sin_t = _rope_tables(seq, ctx_len)

    for l in range(depth):
        w_main, w_gate = _reorder_w_in(w_in[l])
        u, g = _in_projection(xa, mod[l], pre_norm_w[l], w_main, w_gate, ctx_len)
        zeros8 = jnp.zeros((8,), F32)
        bias_row = jnp.pad(jnp.concatenate([mlstm_i_bias[l].reshape(-1), mlstm_f_bias[l].reshape(-1),
                                            dn_dt_bias[l].reshape(-1), zeros8]), (0, LANES - 32))
        alog_row = jnp.pad(jnp.concatenate([zeros8, zeros8, dn_a_log[l].reshape(-1), zeros8]), (0, LANES - 32))
        a = _gate_prep(g, bias_row.reshape(1, LANES), alog_row.reshape(1, LANES))
        at = jnp.swapaxes(a[:, :, :32], 1, 2)
        at_ch = at.reshape(bsz, 32, t // CH, CH).transpose(0, 2, 1, 3)
        at_dc = at.reshape(bsz, 32, t // DC, DC).transpose(0, 2, 1, 3)
        y_m = _mlstm(u, a, at_ch, ctx_len)
        y_r = _retention(u, ret_log_gamma[l], cos_t, sin_t, ctx_len)
        y_a = _attention(u, cos_t, sin_t, attn_q_norm_w[l], attn_k_norm_w[l], ctx_len)
        y_d = _deltanet(u, dn_conv_w[l], a, at_dc, ctx_len)
        xa = _out_projection(xa, (y_m, y_r, y_a, y_d), u, head_norm_w[l], w_out[l].astype(BF16), mod[l],
                             post_norm_w[l], ctx_len)
    return xa[:, ctx_len:]
```

```python
import functools

import jax
import jax.numpy as jnp
from jax import lax
from jax.experimental import pallas as pl
from jax.experimental.pallas import tpu as pltpu

F32 = jnp.float32
BF16 = jnp.bfloat16
HI = lax.Precision.HIGHEST

D_MODEL = 2048
GROUP_W = 512
N_HEADS = 4
HEAD_DIM = 128
KV_HEADS = 2
GRID_W = 64
CONV_K = 5
ROPE_BASE = 10000.0
EPS = 1e-6
SCALE = HEAD_DIM ** -0.5
NEG = -1e30

LANES = 128
CH = 256
DC = 64
PRE_UNROLL = 4
N_MAIN = 64 * LANES
VMEM_LIMIT = 56 * 1024 * 1024

CB = dict(m_q=0, m_k=4, m_v=8, m_o=12, m_z=16, r_q=20, r_k=24, r_v=28, r_z=32,
          a_q=36, a_k=40, a_v=42, a_z=44, d_q=48, d_k=52, d_v=56, d_z=60)


def _dot(a, b, prec=None):
    return jnp.dot(a, b, preferred_element_type=F32, precision=prec)


def _dot_nt(a, b, prec=None):
    return lax.dot_general(a, b, (((1,), (1,)), ((), ())), preferred_element_type=F32, precision=prec)


def _dot_tn(a, b, prec=None):
    return lax.dot_general(a, b, (((0,), (0,)), ((), ())), preferred_element_type=F32, precision=prec)


def _split(a):
    hi = a.astype(BF16)
    return hi, (a - hi.astype(F32)).astype(BF16)


def _dot3(a, b):
    return _dot(a[0], b[0]) + (_dot(a[1], b[0]) + _dot(a[0], b[1]))


def _sigmoid(x):
    return 1.0 / (1.0 + jnp.exp(-x))


def _lane_col(blk, idx):
    lane = lax.broadcasted_iota(jnp.int32, blk.shape, 1)
    return jnp.sum(jnp.where(lane == idx, blk, 0.0), axis=1, keepdims=True)


def _sub_row(blk, idx):
    sub = lax.broadcasted_iota(jnp.int32, blk.shape, 0)
    return jnp.sum(jnp.where(sub == idx, blk, 0.0), axis=0, keepdims=True)


def _rope(t, cos, sin_signed):
    lane = lax.broadcasted_iota(jnp.int32, t.shape, 1)
    partner = jnp.where((lane // 32) % 2 == 0, pltpu.roll(t, 96, 1), pltpu.roll(t, 32, 1))
    return t * cos + partner * sin_signed


def _params(sem):
    return pltpu.CompilerParams(dimension_semantics=sem, vmem_limit_bytes=VMEM_LIMIT)


def _mod_kernel(c_ref, w_ref, b_ref, o_ref):
    c = c_ref[...]
    o_ref[...] = _dot(c * _sigmoid(c), w_ref[...], HI) + b_ref[...]


def _modulation(c8, ada_w, ada_b):
    depth, d, n3 = ada_w.shape
    tn = 768
    return pl.pallas_call(
        _mod_kernel,
        grid=(depth, n3 // tn),
        in_specs=[pl.BlockSpec((8, d), lambda l, n: (0, 0)),
                  pl.BlockSpec((None, d, tn), lambda l, n: (l, 0, n)),
                  pl.BlockSpec((None, 1, tn), lambda l, n: (l, 0, n))],
        out_specs=pl.BlockSpec((None, 8, tn), lambda l, n: (l, 0, n)),
        out_shape=jax.ShapeDtypeStruct((depth, 8, n3), F32),
        compiler_params=_params(("parallel", "parallel")),
        name="adaln_mod",
    )(c8, ada_w, ada_b.reshape(depth, 1, n3))


def _inproj_kernel(x_ref, mod_ref, pw_ref, wm_ref, wg_ref, u_ref, g_ref, h_scr, *, tm, ctx_len):
    b = pl.program_id(0)
    i = pl.program_id(1)
    n = pl.program_id(2)

    @pl.when(n == 0)
    def _():
        x = x_ref[...]
        y = x * lax.rsqrt(jnp.mean(x * x, axis=-1, keepdims=True) + EPS) * pw_ref[...]
        row = i * tm + lax.broadcasted_iota(jnp.int32, (tm, 1), 0)
        is_ctx = row < ctx_len
        d = D_MODEL
        sh = jnp.where(is_ctx, mod_ref[4:5, 0:d], mod_ref[pl.ds(b, 1), 0:d])
        sc = jnp.where(is_ctx, mod_ref[4:5, d:2 * d], mod_ref[pl.ds(b, 1), d:2 * d])
        hh = (y * (1.0 + sc) + sh).astype(BF16)
        h_scr[...] = hh
        g_ref[...] = _dot(hh, wg_ref[...])

    u_ref[...] = _dot(h_scr[...], wm_ref[...]).astype(BF16)


def _in_projection(xa, mod, pre_w, w_main, w_gate, ctx_len):
    bsz, t, d = xa.shape
    tm = t // 4 if (t % 4 == 0 and (t // 4) % 16 == 0) else CH
    tn = 1024
    kern = functools.partial(_inproj_kernel, tm=tm, ctx_len=ctx_len)
    return pl.pallas_call(
        kern,
        grid=(bsz, t // tm, N_MAIN // tn),
        in_specs=[pl.BlockSpec((None, tm, d), lambda b, i, n: (b, i, 0)),
                  pl.BlockSpec((8, 3 * d), lambda b, i, n: (0, 0)),
                  pl.BlockSpec((1, d), lambda b, i, n: (0, 0)),
                  pl.BlockSpec((d, tn), lambda b, i, n: (0, n)),
                  pl.BlockSpec((d, LANES), lambda b, i, n: (0, 0))],
        out_specs=[pl.BlockSpec((None, tm, tn), lambda b, i, n: (b, i, n)),
                   pl.BlockSpec((None, tm, LANES), lambda b, i, n: (b, i, 0))],
        out_shape=[jax.ShapeDtypeStruct((bsz, t, N_MAIN), BF16),
                   jax.ShapeDtypeStruct((bsz, t, LANES), F32)],
        scratch_shapes=[pltpu.VMEM((tm, d), BF16)],
        compiler_params=_params(("parallel", "parallel", "arbitrary")),
        name="in_proj",
    )(xa, mod, pre_w.reshape(1, d), w_main, w_gate)


def _gate_kernel(g_ref, bias_ref, alog_ref, a_ref):
    x = g_ref[...] + bias_ref[...]
    lane = lax.broadcasted_iota(jnp.int32, x.shape, 1)
    l1p = jnp.log(1.0 + jnp.exp(-jnp.abs(x)))
    log_f = jnp.minimum(x, 0.0) - l1p
    log_a = -jnp.exp(alog_ref[...]) * (jnp.maximum(x, 0.0) + l1p)
    v = jnp.where(lane < 8, x, jnp.where(lane < 16, log_f, jnp.where(lane < 24, log_a, _sigmoid(x))))
    r = lax.broadcasted_iota(jnp.int32, (CH, CH), 0)
    c = lax.broadcasted_iota(jnp.int32, (CH, CH), 1)
    same = (r // DC) == (c // DC)
    pre = (c <= r).astype(F32)
    suf = (c >= r).astype(F32)
    pre_dc = jnp.where(same, pre, 0.0)
    suf_dc = jnp.where(same, suf, 0.0)
    fwd = (lane % 8) < 4
    cum_ch = jnp.where(fwd, _dot(pre, v, HI), _dot(suf, v, HI))
    cum_dc = jnp.where(fwd, _dot(pre_dc, v, HI), _dot(suf_dc, v, HI))
    a_ref[...] = jnp.where((lane >= 8) & (lane < 16), cum_ch, jnp.where((lane >= 16) & (lane < 24), cum_dc, v))


def _gate_prep(g, bias_row, alog_row):
    bsz, t, _ = g.shape
    return pl.pallas_call(
        _gate_kernel,
        grid=(bsz, t // CH),
        in_specs=[pl.BlockSpec((None, CH, LANES), lambda b, i: (b, i, 0)),
                  pl.BlockSpec((1, LANES), lambda b, i: (0, 0)),
                  pl.BlockSpec((1, LANES), lambda b, i: (0, 0))],
        out_specs=pl.BlockSpec((None, CH, LANES), lambda b, i: (b, i, 0)),
        out_shape=jax.ShapeDtypeStruct((bsz, t, LANES), F32),
        compiler_params=_params(("parallel", "parallel")),
        name="gate_prep",
    )(g, bias_row, alog_row)


def _chunk_order(j, d, n_chunks, n_ctx_chunks):
    if d == 0:
        return j
    return jnp.where(j < n_ctx_chunks, n_ctx_chunks - 1 - j, n_chunks + n_ctx_chunks - 1 - j)


def _head_spec(t, name):
    base = CB[name]
    return pl.BlockSpec((None, t, LANES), lambda b, h: (b, 0, base + h))


def _mlstm_kernel(q_ref, k_ref, v_ref, o_ref, a_ref, at_ref, y_ref, c_scr, *, n_chunks, n_ctx_chunks):
    h = pl.program_id(1)
    r = lax.broadcasted_iota(jnp.int32, (CH, CH), 0)
    c = lax.broadcasted_iota(jnp.int32, (CH, CH), 1)
    one_col = (lax.broadcasted_iota(jnp.int32, (CH, LANES), 1) == 0).astype(F32)

    for d in range(2):
        mask = (c <= r) if d == 0 else (c >= r)
        last = CH - 1 if d == 0 else 0
        c_scr[...] = jnp.zeros_like(c_scr)

        def step(j, m, d=d, mask=mask, last=last):
            ci = _chunk_order(j, d, n_chunks, n_ctx_chunks)
            rows = pl.ds(pl.multiple_of(ci * CH, CH), CH)
            q = q_ref[rows, :]
            k = k_ref[rows, :]
            v = v_ref[rows, :].astype(F32)
            a_blk = a_ref[rows, :]
            at_blk = at_ref[ci]
            li_col = _lane_col(a_blk, d * 4 + h)
            b_col = _lane_col(a_blk, 8 + d * 4 + h)
            li_row = _sub_row(at_blk, d * 4 + h)
            b_row = _sub_row(at_blk, 8 + d * 4 + h)

            dm = jnp.where(mask, b_col - b_row + li_row, NEG)
            inter = m + b_col
            m_t = jnp.maximum(inter, jnp.max(dm, axis=1, keepdims=True))
            w_inter = jnp.exp(inter - m_t)
            s = _dot_nt(q, k) * (jnp.exp(dm - m_t) * SCALE)
            v_aug = jnp.concatenate([v, one_col], axis=1)
            nd = (w_inter * SCALE) * _dot(q, c_scr[...].astype(BF16)) + _dot(s.astype(BF16), v_aug.astype(BF16))
            num = nd[:, :LANES]
            den = nd[:, LANES:LANES + 1]
            h_out = num / jnp.maximum(jnp.abs(den), jnp.exp(-m_t))

            b_end = b_col[last:last + 1, :]
            dec = b_end - b_col + li_col
            m_new = jnp.maximum(m + b_end, jnp.max(dec, axis=0, keepdims=True))
            w_state = jnp.exp(dec - m_new)
            c_scr[...] = jnp.exp(m + b_end - m_new) * c_scr[...] + _dot_tn(k, (w_state * v_aug).astype(BF16))

            if d == 0:
                y_ref[rows, :] = h_out
            else:
                y_ref[rows, :] = (y_ref[rows, :] + h_out) * _sigmoid(o_ref[rows, :].astype(F32))
            return m_new

        lax.fori_loop(0, n_chunks, step, jnp.zeros((1, 1), F32))


def _mlstm(u, a, at_ch, ctx_len):
    bsz, t, _ = u.shape
    n_chunks = t // CH
    kern = functools.partial(_mlstm_kernel, n_chunks=n_chunks, n_ctx_chunks=ctx_len // CH)
    return pl.pallas_call(
        kern,
        grid=(bsz, N_HEADS),
        in_specs=[_head_spec(t, "m_q"), _head_spec(t, "m_k"), _head_spec(t, "m_v"), _head_spec(t, "m_o"),
                  pl.BlockSpec((None, t, LANES), lambda b, h: (b, 0, 0)),
                  pl.BlockSpec((None, n_chunks, 32, CH), lambda b, h: (b, 0, 0, 0))],
        out_specs=pl.BlockSpec((None, t, LANES), lambda b, h: (b, 0, h)),
        out_shape=jax.ShapeDtypeStruct((bsz, t, GROUP_W), F32),
        scratch_shapes=[pltpu.VMEM((HEAD_DIM, 2 * LANES), F32)],
        compiler_params=_params(("parallel", "parallel")),
        name="mlstm",
    )(u, u, u, u, a, at_ch)


def _retention_kernel(lg_ref, q_ref, k_ref, v_ref, cos_ref, sin_ref, y_ref, qp_scr, kp_scr, r_scr, *,
                      n_chunks, n_ctx_chunks):
    h = pl.program_id(1)

    def prep(ci, carry):
        rows = pl.ds(pl.multiple_of(ci * CH, CH), CH)
        cos = cos_ref[rows, :]
        sin = sin_ref[rows, :]
        qp_scr[rows, :] = _rope(q_ref[rows, :].astype(F32), cos, sin).astype(BF16)
        kp_scr[rows, :] = (_rope(k_ref[rows, :].astype(F32), cos, sin) * SCALE).astype(BF16)
        return carry

    lax.fori_loop(0, n_chunks, prep, 0)

    r = lax.broadcasted_iota(jnp.int32, (CH, CH), 0).astype(F32)
    c = lax.broadcasted_iota(jnp.int32, (CH, CH), 1).astype(F32)
    pos = lax.broadcasted_iota(jnp.int32, (CH, 1), 0).astype(F32)

    for d in range(2):
        lg = lg_ref[d * N_HEADS + h]
        rel = (r - c) if d == 0 else (c - r)
        decay = jnp.where(rel >= 0, jnp.exp(lg * jnp.maximum(rel, 0.0)), 0.0)
        p_vis = pos if d == 0 else (CH - 1.0) - pos
        q_decay = jnp.exp(lg * (p_vis + 1.0))
        k_decay = jnp.exp(lg * ((CH - 1.0) - p_vis))
        chunk_decay = jnp.exp(lg * CH)
        r_scr[...] = jnp.zeros_like(r_scr)

        def step(j, carry, d=d, decay=decay, q_decay=q_decay, k_decay=k_decay, chunk_decay=chunk_decay):
            ci = _chunk_order(j, d, n_chunks, n_ctx_chunks)
            rows = pl.ds(pl.multiple_of(ci * CH, CH), CH)
            q = qp_scr[rows, :]
            k = kp_scr[rows, :]
            v = v_ref[rows, :]
            s = _dot_nt(q, k) * decay
            o = _dot(s.astype(BF16), v) + q_decay * _dot(q, r_scr[...].astype(BF16))
            kd = (k.astype(F32) * k_decay).astype(BF16)
            r_scr[...] = chunk_decay * r_scr[...] + _dot_tn(kd, v)
            if d == 0:
                y_ref[rows, :] = o
            else:
                y_ref[rows, :] = y_ref[rows, :] + o
            return carry

        lax.fori_loop(0, n_chunks, step, 0)


def _retention(u, log_gamma, cos_t, sin_t, ctx_len):
    bsz, t, _ = u.shape
    n_chunks = t // CH
    kern = functools.partial(_retention_kernel, n_chunks=n_chunks, n_ctx_chunks=ctx_len // CH)
    tab = pl.BlockSpec((t, LANES), lambda b, h: (0, 0))
    return pl.pallas_call(
        kern,
        grid=(bsz, N_HEADS),
        in_specs=[pl.BlockSpec(memory_space=pltpu.SMEM),
                  _head_spec(t, "r_q"), _head_spec(t, "r_k"), _head_spec(t, "r_v"), tab, tab],
        out_specs=pl.BlockSpec((None, t, LANES), lambda b, h: (b, 0, h)),
        out_shape=jax.ShapeDtypeStruct((bsz, t, GROUP_W), F32),
        scratch_shapes=[pltpu.VMEM((t, LANES), BF16), pltpu.VMEM((t, LANES), BF16),
                        pltpu.VMEM((HEAD_DIM, HEAD_DIM), F32)],
        compiler_params=_params(("parallel", "parallel")),
        name="retention",
    )(log_gamma.reshape(2 * N_HEADS), u, u, u, cos_t, sin_t)


def _attn_kernel(q_ref, k_ref, v_ref, cos_ref, sin_ref, cosq_ref, sinq_ref, qw_ref, kw_ref, o_ref, kp_scr, *,
                 t, ctx_len):
    qi = pl.program_id(2)

    @pl.when(qi == 0)
    def _():
        def prep(ci, carry):
            rows = pl.ds(pl.multiple_of(ci * CH, CH), CH)
            k = k_ref[rows, :].astype(F32)
            k = k * lax.rsqrt(jnp.mean(k * k, axis=-1, keepdims=True) + EPS) * kw_ref[...]
            kp_scr[rows, :] = _rope(k, cos_ref[rows, :], sin_ref[rows, :]).astype(BF16)
            return carry

        lax.fori_loop(0, t // CH, prep, 0)

    qs = []
    for g in range(2):
        q = q_ref[:, g * LANES:(g + 1) * LANES].astype(F32)
        q = q * lax.rsqrt(jnp.mean(q * q, axis=-1, keepdims=True) + EPS) * qw_ref[...]
        qs.append((_rope(q, cosq_ref[...], sinq_ref[...]) * SCALE).astype(BF16))
    qq = jnp.concatenate(qs, axis=0)

    def attend(n_keys):
        s = _dot_nt(qq, kp_scr[0:n_keys, :])
        p = jnp.exp(s - jnp.max(s, axis=-1, keepdims=True))
        o = _dot(p.astype(BF16), v_ref[0:n_keys, :]) / jnp.sum(p, axis=-1, keepdims=True)
        o_ref[:, 0:LANES] = o[:CH]
        o_ref[:, LANES:2 * LANES] = o[CH:]

    @pl.when(qi == 0)
    def _():
        attend(ctx_len)

    @pl.when(qi > 0)
    def _():
        attend(t)


def _attention(u, cos_t, sin_t, qn_w, kn_w, ctx_len):
    bsz, t, _ = u.shape
    assert ctx_len == CH
    kern = functools.partial(_attn_kernel, t=t, ctx_len=ctx_len)
    qb = CB["a_q"] // 2
    kb = CB["a_k"]
    vb = CB["a_v"]
    tab = pl.BlockSpec((t, LANES), lambda b, kv, i: (0, 0))
    tabq = pl.BlockSpec((CH, LANES), lambda b, kv, i: (i, 0))
    vec = pl.BlockSpec((1, LANES), lambda b, kv, i: (0, 0))
    return pl.pallas_call(
        kern,
        grid=(bsz, KV_HEADS, t // CH),
        in_specs=[pl.BlockSpec((None, CH, 2 * LANES), lambda b, kv, i: (b, i, qb + kv)),
                  pl.BlockSpec((None, t, LANES), lambda b, kv, i: (b, 0, kb + kv)),
                  pl.BlockSpec((None, t, LANES), lambda b, kv, i: (b, 0, vb + kv)),
                  tab, tab, tabq, tabq, vec, vec],
        out_specs=pl.BlockSpec((None, CH, 2 * LANES), lambda b, kv, i: (b, i, kv)),
        out_shape=jax.ShapeDtypeStruct((bsz, t, GROUP_W), F32),
        scratch_shapes=[pltpu.VMEM((t, LANES), BF16)],
        compiler_params=_params(("parallel", "parallel", "arbitrary")),
        name="attention",
    )(u, u, u, cos_t, sin_t, cos_t, sin_t, qn_w.reshape(1, LANES), kn_w.reshape(1, LANES))


def _tri_inverse(n_mats):
    r = lax.broadcasted_iota(jnp.int32, (DC, DC), 0)
    c = lax.broadcasted_iota(jnp.int32, (DC, DC), 1)
    eye = (r == c).astype(F32)
    diag16 = (r // 16) == (c // 16)
    ms = [-jnp.where(diag16, n, 0.0) for n in n_mats]
    ps = [eye + m for m in ms]
    mps = [_split(m) for m in ms]
    for _ in range(3):
        mps = [_split(_dot3(mp, mp)) for mp in mps]
        ps = [p + _dot3(_split(p), mp) for p, mp in zip(ps, mps)]
    for w in (32, 64):
        off = ((r // w) == (c // w)) & ((r // (w // 2)) != (c // (w // 2)))
        pss = [_split(p) for p in ps]
        tmp = [_split(_dot3(_split(jnp.where(off, n, 0.0)), p2)) for n, p2 in zip(n_mats, pss)]
        ps = [p - _dot3(p2, a) for p, p2, a in zip(ps, pss, tmp)]
    return ps


def _deltanet_kernel(q_ref, k_ref, v_ref, wq_ref, wk_ref, wv_ref, a_ref, at_ref, y_ref,
                     xs_scr, qd_scr, kd_scr, vd_scr, qw_scr, wv_scr, ke_scr, qk_scr, ge_scr, y1_scr, *,
                     t, ctx_len):
    h = pl.program_id(1)
    n_chunks = t // DC
    n_ctx_chunks = ctx_len // DC
    pad = 8

    def prep_stream(src_ref, w_ref, dst_scr, l2, scale):
        zeros = jnp.zeros((pad, LANES), F32)
        xs_scr[0:pad, :] = zeros
        xs_scr[t + pad:t + 2 * pad, :] = zeros

        def load(ci, carry):
            rows = pl.ds(pl.multiple_of(ci * CH, CH), CH)
            xs_scr[pl.ds(pl.multiple_of(ci * CH + pad, 8), CH), :] = src_ref[rows, :].astype(F32)
            return carry

        lax.fori_loop(0, t // CH, load, 0)
        w = w_ref[...]

        def conv(ci, carry):
            win = xs_scr[pl.ds(pl.multiple_of(ci * CH, CH), CH + 2 * pad), :]
            tok = ci * CH - pad + lax.broadcasted_iota(jnp.int32, (CH + 2 * pad, 1), 0)
            win = jnp.where((tok < ctx_len) == (ci * CH < ctx_len), win, 0.0)
            acc = win[pad - 2:pad - 2 + CH, :] * w[0:1, :]
            for j in range(1, CONV_K):
                acc = acc + win[pad - 2 + j:pad - 2 + j + CH, :] * w[j:j + 1, :]
            acc = acc * _sigmoid(acc)
            if l2:
                acc = acc * lax.rsqrt(jnp.sum(acc * acc, axis=-1, keepdims=True) + EPS) * scale
            dst_scr[pl.ds(pl.multiple_of(ci * CH, CH), CH), :] = acc
            return carry

        lax.fori_loop(0, t // CH, conv, 0)

    prep_stream(q_ref, wq_ref, qd_scr, True, SCALE)
    prep_stream(k_ref, wk_ref, kd_scr, True, 1.0)
    prep_stream(v_ref, wv_ref, vd_scr, False, 1.0)

    r = lax.broadcasted_iota(jnp.int32, (DC, DC), 0)
    c = lax.broadcasted_iota(jnp.int32, (DC, DC), 1)

    def pre(i, carry):
        chains = []
        n_mats = []
        for uu in range(PRE_UNROLL):
            ci = i * PRE_UNROLL + uu
            rows = pl.ds(pl.multiple_of(ci * DC, DC), DC)
            q = qd_scr[rows, :]
            k = kd_scr[rows, :]
            v = vd_scr[rows, :]
            a_blk = a_ref[rows, :]
            at_blk = at_ref[ci]
            kb = k.astype(BF16)
            kk = _dot_nt(kb, kb)
            qk = _dot_nt(q.astype(BF16), kb)
            for d in range(2):
                incl = (c <= r) if d == 0 else (c >= r)
                strict = (c < r) if d == 0 else (c > r)
                last = DC - 1 if d == 0 else 0
                g_col = _lane_col(a_blk, 16 + d * 4 + h)
                beta = _lane_col(a_blk, 24 + d * 4 + h)
                g_row = _sub_row(at_blk, 16 + d * 4 + h)
                decay = jnp.exp(jnp.where(incl, g_col - g_row, NEG))
                eg = jnp.exp(g_col)
                g_end = g_col[last:last + 1, :]
                n_mats.append(jnp.where(strict, beta * decay * kk, 0.0))
                rhs = _split(jnp.concatenate([beta * v, (beta * eg) * k], axis=1))
                qw_scr[d, ci, 0:DC, :] = (eg * q).astype(BF16)
                ke_scr[d, ci] = (jnp.exp(g_end - g_col) * k).astype(BF16)
                qk_scr[d, ci] = (qk * decay).astype(BF16)
                ge_scr[d, ci] = jnp.broadcast_to(jnp.exp(g_end), (8, LANES))
                chains.append((d, ci, rhs))
        t_invs = _tri_inverse(n_mats)
        ws = [_dot3(_split(t_inv), rhs) for t_inv, (_, _, rhs) in zip(t_invs, chains)]
        for w, (d, ci, _) in zip(ws, chains):
            qw_scr[d, ci, DC:2 * DC, :] = w[:, LANES:].astype(BF16)
            wv_scr[d, ci] = w[:, :LANES]
        return carry

    lax.fori_loop(0, n_chunks // PRE_UNROLL, pre, 0)

    def scan(j, carry):
        cis = [_chunk_order(j, d, n_chunks, n_ctx_chunks) for d in range(2)]
        xs = [_dot(qw_scr[d, cis[d]], carry[d].astype(BF16)) for d in range(2)]
        ubs = [(wv_scr[d, cis[d]] - xs[d][DC:, :]).astype(BF16) for d in range(2)]
        new = tuple(ge_scr[d, cis[d]][0:1, :] * carry[d] + _dot_tn(ke_scr[d, cis[d]], ubs[d]) for d in range(2))
        os_ = [xs[d][:DC, :] + _dot(qk_scr[d, cis[d]], ubs[d]) for d in range(2)]
        y_ref[pl.ds(pl.multiple_of(cis[0] * DC, DC), DC), :] = os_[0]
        y1_scr[pl.ds(pl.multiple_of(cis[1] * DC, DC), DC), :] = os_[1]
        return new

    zero = jnp.zeros((HEAD_DIM, HEAD_DIM), F32)
    lax.fori_loop(0, n_chunks, scan, (zero, zero))

    def add(ci, carry):
        rows = pl.ds(pl.multiple_of(ci * CH, CH), CH)
        y_ref[rows, :] = y_ref[rows, :] + y1_scr[rows, :]
        return carry

    lax.fori_loop(0, t // CH, add, 0)


def _deltanet(u, conv_w, a, at_dc, ctx_len):
    bsz, t, _ = u.shape
    n_chunks = t // DC
    kern = functools.partial(_deltanet_kernel, t=t, ctx_len=ctx_len)

    def wspec(off):
        return pl.BlockSpec((CONV_K, LANES), lambda b, h: (0, off + h))

    return pl.pallas_call(
        kern,
        grid=(bsz, N_HEADS),
        in_specs=[_head_spec(t, "d_q"), _head_spec(t, "d_k"), _head_spec(t, "d_v"),
                  wspec(0), wspec(N_HEADS), wspec(2 * N_HEADS),
                  pl.BlockSpec((None, t, LANES), lambda b, h: (b, 0, 0)),
                  pl.BlockSpec((None, n_chunks, 32, DC), lambda b, h: (b, 0, 0, 0))],
        out_specs=pl.BlockSpec((None, t, LANES), lambda b, h: (b, 0, h)),
        out_shape=jax.ShapeDtypeStruct((bsz, t, GROUP_W), F32),
        scratch_shapes=[pltpu.VMEM((t + 16, LANES), F32),
                        pltpu.VMEM((t, LANES), F32), pltpu.VMEM((t, LANES), F32), pltpu.VMEM((t, LANES), F32),
                        pltpu.VMEM((2, n_chunks, 2 * DC, LANES), BF16),
                        pltpu.VMEM((2, n_chunks, DC, LANES), F32),
                        pltpu.VMEM((2, n_chunks, DC, LANES), BF16),
                        pltpu.VMEM((2, n_chunks, DC, DC), BF16),
                        pltpu.VMEM((2, n_chunks, 8, LANES), F32),
                        pltpu.VMEM((t, LANES), F32)],
        compiler_params=_params(("parallel", "parallel")),
        name="deltanet",
    )(u, u, u, conv_w, conv_w, conv_w, a, at_dc)


def _out_kernel(x_ref, ym_ref, yr_ref, ya_ref, yd_ref, zm_ref, zr_ref, za_ref, zd_ref, hn_ref, w_ref,
                mod_ref, pw_ref, o_ref):
    b = pl.program_id(0)
    i = pl.program_id(1)

    def silu(z_ref):
        z = z_ref[...].astype(F32)
        return z * _sigmoid(z)

    def head_rms(y_ref, g):
        y = y_ref[...]
        parts = []
        for hh in range(N_HEADS):
            p = y[:, hh * LANES:(hh + 1) * LANES]
            parts.append(p * lax.rsqrt(jnp.mean(p * p, axis=-1, keepdims=True) + EPS))
        return jnp.concatenate(parts, axis=1) * hn_ref[:, g * GROUP_W:(g + 1) * GROUP_W]

    merged = jnp.concatenate([
        (head_rms(ym_ref, 0) * silu(zm_ref)).astype(BF16),
        (head_rms(yr_ref, 1) * silu(zr_ref)).astype(BF16),
        (ya_ref[...] * silu(za_ref)).astype(BF16),
        (head_rms(yd_ref, 2) * silu(zd_ref)).astype(BF16)], axis=1)
    o = _dot(merged, w_ref[...])
    o = o * lax.rsqrt(jnp.mean(o * o, axis=-1, keepdims=True) + EPS) * pw_ref[...]
    row = jnp.where(i == 0, 4, b)
    gate = mod_ref[pl.ds(row, 1), 2 * D_MODEL:3 * D_MODEL]
    o_ref[...] = x_ref[...] + gate * o


def _out_projection(xa, ys, u, hn_w, w_out, mod, post_w, ctx_len):
    bsz, t, d = xa.shape
    assert ctx_len == CH
    tok = lambda b, i: (b, i, 0)
    yspec = pl.BlockSpec((None, CH, GROUP_W), tok)

    def zspec(name):
        blk = CB[name] // N_HEADS
        return pl.BlockSpec((None, CH, GROUP_W), lambda b, i: (b, i, blk))

    return pl.pallas_call(
        _out_kernel,
        grid=(bsz, t // CH),
        in_specs=[pl.BlockSpec((None, CH, d), tok), yspec, yspec, yspec, yspec,
                  zspec("m_z"), zspec("r_z"), zspec("a_z"), zspec("d_z"),
                  pl.BlockSpec((1, 3 * GROUP_W), lambda b, i: (0, 0)),
                  pl.BlockSpec((d, d), lambda b, i: (0, 0)),
                  pl.BlockSpec((8, 3 * d), lambda b, i: (0, 0)),
                  pl.BlockSpec((1, d), lambda b, i: (0, 0))],
        out_specs=pl.BlockSpec((None, CH, d), tok),
        out_shape=jax.ShapeDtypeStruct((bsz, t, d), F32),
        compiler_params=_params(("parallel", "parallel")),
        name="out_proj",
    )(xa, *ys, u, u, u, u, hn_w.reshape(1, 3 * GROUP_W), w_out, mod, post_w.reshape(1, d))


def _reorder_w_in(w):
    g = GROUP_W
    m_gate = 5 * g
    r_start = m_gate + 4 * N_HEADS
    d_gate = r_start + 4 * g + (2 * g + 2 * KV_HEADS * HEAD_DIM) + 4 * g
    w_main = jnp.concatenate([w[:, :m_gate], w[:, r_start:d_gate]], axis=1).astype(BF16)
    gates = jnp.concatenate([w[:, m_gate:r_start], w[:, d_gate:]], axis=1)
    w_gate = jnp.pad(gates, ((0, 0), (0, LANES - gates.shape[1]))).astype(BF16)
    return w_main, w_gate


def _rope_tables(seq, ctx_len):
    rows = seq // GRID_W
    row = jnp.repeat(jnp.arange(rows), GRID_W)
    col = jnp.tile(jnp.arange(GRID_W), rows)
    n_freq = HEAD_DIM // 4
    inv_freq = ROPE_BASE ** (-jnp.arange(n_freq, dtype=F32) / n_freq)
    ar = row[:, None] * inv_freq
    ac = col[:, None] * inv_freq
    cos = jnp.concatenate([jnp.cos(ar), jnp.cos(ar), jnp.cos(ac), jnp.cos(ac)], axis=1)
    sin = jnp.concatenate([-jnp.sin(ar), jnp.sin(ar), -jnp.sin(ac), jnp.sin(ac)], axis=1)
    cos = jnp.concatenate([jnp.ones((ctx_len, HEAD_DIM), F32), cos], axis=0)
    sin = jnp.concatenate([jnp.zeros((ctx_len, HEAD_DIM), F32), sin], axis=0)
    return cos, sin


def kernel(x, c, ctx, c_ctx, ada_w, ada_b, pre_norm_w, post_norm_w, w_in, w_out, mlstm_i_bias, mlstm_f_bias,
           ret_log_gamma, attn_q_norm_w, attn_k_norm_w, dn_conv_w, dn_a_log, dn_dt_bias, head_norm_w):
    bsz, seq, d = x.shape
    ctx_len = ctx.shape[1]
    t = ctx_len + seq
    depth = ada_w.shape[0]
    assert bsz <= 4 and ctx_len % CH == 0 and seq % CH == 0

    xa = jnp.concatenate([ctx, x], axis=1)
    c8 = jnp.zeros((8, d), F32).at[:bsz].set(c).at[4].set(c_ctx)
    mod = _modulation(c8, ada_w, ada_b)
    cos_t, sin_t = _rope_tables(seq, ctx_len)

    for l in range(depth):
        w_main, w_gate = _reorder_w_in(w_in[l])
        u, g = _in_projection(xa, mod[l], pre_norm_w[l], w_main, w_gate, ctx_len)
        zeros8 = jnp.zeros((8,), F32)
        bias_row = jnp.pad(jnp.concatenate([mlstm_i_bias[l].reshape(-1), mlstm_f_bias[l].reshape(-1),
                                            dn_dt_bias[l].reshape(-1), zeros8]), (0, LANES - 32))
        alog_row = jnp.pad(jnp.concatenate([zeros8, zeros8, dn_a_log[l].reshape(-1), zeros8]), (0, LANES - 32))
        a = _gate_prep(g, bias_row.reshape(1, LANES), alog_row.reshape(1, LANES))
        at = jnp.swapaxes(a[:, :, :32], 1, 2)
        at_ch = at.reshape(bsz, 32, t // CH, CH).transpose(0, 2, 1, 3)
        at_dc = at.reshape(bsz, 32, t // DC, DC).transpose(0, 2, 1, 3)
        y_m = _mlstm(u, a, at_ch, ctx_len)
        y_r = _retention(u, ret_log_gamma[l], cos_t, sin_t, ctx_len)
        y_a = _attention(u, cos_t, sin_t, attn_q_norm_w[l], attn_k_norm_w[l], ctx_len)
        y_d = _deltanet(u, dn_conv_w[l], a, at_dc, ctx_len)
        xa = _out_projection(xa, (y_m, y_r, y_a, y_d), u, head_norm_w[l], w_out[l].astype(BF16), mod[l],
                             post_norm_w[l], ctx_len)
    return xa[:, ctx_len:]
```

```python
import functools

import jax
import jax.numpy as jnp
from jax import lax
from jax.experimental import pallas as pl
from jax.experimental.pallas import tpu as pltpu

F32 = jnp.float32
BF16 = jnp.bfloat16
HI = lax.Precision.HIGHEST

D_MODEL = 2048
GROUP_W = 512
N_HEADS = 4
HEAD_DIM = 128
KV_HEADS = 2
GRID_W = 64
CONV_K = 5
ROPE_BASE = 10000.0
EPS = 1e-6
SCALE = HEAD_DIM ** -0.5
NEG = -1e30

LANES = 128
CH = 256
DC = 64
PRE_UNROLL = 4
N_MAIN = 64 * LANES
VMEM_LIMIT = 56 * 1024 * 1024

CB = dict(m_q=0, m_k=4, m_v=8, m_o=12, m_z=16, r_q=20, r_k=24, r_v=28, r_z=32,
          a_q=36, a_k=40, a_v=42, a_z=44, d_q=48, d_k=52, d_v=56, d_z=60)


def _dot(a, b, prec=None):
    return jnp.dot(a, b, preferred_element_type=F32, precision=prec)


def _dot_nt(a, b, prec=None):
    return lax.dot_general(a, b, (((1,), (1,)), ((), ())), preferred_element_type=F32, precision=prec)


def _dot_tn(a, b, prec=None):
    return lax.dot_general(a, b, (((0,), (0,)), ((), ())), preferred_element_type=F32, precision=prec)


def _split(a):
    hi = a.astype(BF16)
    return hi, (a - hi.astype(F32)).astype(BF16)


def _dot3(a, b):
    n = a[0].shape[0]
    x = _dot(jnp.concatenate([a[0], a[1]], axis=0), b[0])
    return (x[:n] + x[n:]) + _dot(a[0], b[1])


def _sigmoid(x):
    return 1.0 / (1.0 + jnp.exp(-x))


def _lane_col(blk, idx):
    lane = lax.broadcasted_iota(jnp.int32, blk.shape, 1)
    return jnp.sum(jnp.where(lane == idx, blk, 0.0), axis=1, keepdims=True)


def _sub_row(blk, idx):
    sub = lax.broadcasted_iota(jnp.int32, blk.shape, 0)
    return jnp.sum(jnp.where(sub == idx, blk, 0.0), axis=0, keepdims=True)


def _rope(t, cos, sin_signed):
    lane = lax.broadcasted_iota(jnp.int32, t.shape, 1)
    partner = jnp.where((lane // 32) % 2 == 0, pltpu.roll(t, 96, 1), pltpu.roll(t, 32, 1))
    return t * cos + partner * sin_signed


def _params(sem):
    return pltpu.CompilerParams(dimension_semantics=sem, vmem_limit_bytes=VMEM_LIMIT)


def _mod_kernel(c_ref, w_ref, b_ref, o_ref):
    c = c_ref[...]
    o_ref[...] = _dot(c * _sigmoid(c), w_ref[...], HI) + b_ref[...]


def _modulation(c8, ada_w, ada_b):
    depth, d, n3 = ada_w.shape
    tn = 768
    return pl.pallas_call(
        _mod_kernel,
        grid=(depth, n3 // tn),
        in_specs=[pl.BlockSpec((8, d), lambda l, n: (0, 0)),
                  pl.BlockSpec((None, d, tn), lambda l, n: (l, 0, n)),
                  pl.BlockSpec((None, 1, tn), lambda l, n: (l, 0, n))],
        out_specs=pl.BlockSpec((None, 8, tn), lambda l, n: (l, 0, n)),
        out_shape=jax.ShapeDtypeStruct((depth, 8, n3), F32),
        compiler_params=_params(("parallel", "parallel")),
        name="adaln_mod",
    )(c8, ada_w, ada_b.reshape(depth, 1, n3))


def _inproj_kernel(x_ref, mod_ref, pw_ref, wm_ref, wg_ref, u_ref, g_ref, h_scr, *, tm, ctx_len):
    b = pl.program_id(0)
    i = pl.program_id(1)
    n = pl.program_id(2)

    @pl.when(n == 0)
    def _():
        x = x_ref[...]
        y = x * lax.rsqrt(jnp.mean(x * x, axis=-1, keepdims=True) + EPS) * pw_ref[...]
        row = i * tm + lax.broadcasted_iota(jnp.int32, (tm, 1), 0)
        is_ctx = row < ctx_len
        d = D_MODEL
        sh = jnp.where(is_ctx, mod_ref[4:5, 0:d], mod_ref[pl.ds(b, 1), 0:d])
        sc = jnp.where(is_ctx, mod_ref[4:5, d:2 * d], mod_ref[pl.ds(b, 1), d:2 * d])
        hh = (y * (1.0 + sc) + sh).astype(BF16)
        h_scr[...] = hh
        g_ref[...] = _dot(hh, wg_ref[...])

    u_ref[...] = _dot(h_scr[...], wm_ref[...]).astype(BF16)


def _in_projection(xa, mod, pre_w, w_main, w_gate, ctx_len):
    bsz, t, d = xa.shape
    tm = t // 4 if (t % 4 == 0 and (t // 4) % 16 == 0) else CH
    tn = 1024
    kern = functools.partial(_inproj_kernel, tm=tm, ctx_len=ctx_len)
    return pl.pallas_call(
        kern,
        grid=(bsz, t // tm, N_MAIN // tn),
        in_specs=[pl.BlockSpec((None, tm, d), lambda b, i, n: (b, i, 0)),
                  pl.BlockSpec((8, 3 * d), lambda b, i, n: (0, 0)),
                  pl.BlockSpec((1, d), lambda b, i, n: (0, 0)),
                  pl.BlockSpec((d, tn), lambda b, i, n: (0, n)),
                  pl.BlockSpec((d, LANES), lambda b, i, n: (0, 0))],
        out_specs=[pl.BlockSpec((None, tm, tn), lambda b, i, n: (b, i, n)),
                   pl.BlockSpec((None, tm, LANES), lambda b, i, n: (b, i, 0))],
        out_shape=[jax.ShapeDtypeStruct((bsz, t, N_MAIN), BF16),
                   jax.ShapeDtypeStruct((bsz, t, LANES), F32)],
        scratch_shapes=[pltpu.VMEM((tm, d), BF16)],
        compiler_params=_params(("parallel", "parallel", "arbitrary")),
        name="in_proj",
    )(xa, mod, pre_w.reshape(1, d), w_main, w_gate)


def _gate_kernel(g_ref, bias_ref, alog_ref, a_ref):
    x = g_ref[...] + bias_ref[...]
    lane = lax.broadcasted_iota(jnp.int32, x.shape, 1)
    l1p = jnp.log(1.0 + jnp.exp(-jnp.abs(x)))
    log_f = jnp.minimum(x, 0.0) - l1p
    log_a = -jnp.exp(alog_ref[...]) * (jnp.maximum(x, 0.0) + l1p)
    v = jnp.where(lane < 8, x, jnp.where(lane < 16, log_f, jnp.where(lane < 24, log_a, _sigmoid(x))))
    r = lax.broadcasted_iota(jnp.int32, (CH, CH), 0)
    c = lax.broadcasted_iota(jnp.int32, (CH, CH), 1)
    same = (r // DC) == (c // DC)
    pre = (c <= r).astype(F32)
    suf = (c >= r).astype(F32)
    pre_dc = jnp.where(same, pre, 0.0)
    suf_dc = jnp.where(same, suf, 0.0)
    fwd = (lane % 8) < 4
    cum_ch = jnp.where(fwd, _dot(pre, v, HI), _dot(suf, v, HI))
    cum_dc = jnp.where(fwd, _dot(pre_dc, v, HI), _dot(suf_dc, v, HI))
    a_ref[...] = jnp.where((lane >= 8) & (lane < 16), cum_ch, jnp.where((lane >= 16) & (lane < 24), cum_dc, v))


def _gate_prep(g, bias_row, alog_row):
    bsz, t, _ = g.shape
    return pl.pallas_call(
        _gate_kernel,
        grid=(bsz, t // CH),
        in_specs=[pl.BlockSpec((None, CH, LANES), lambda b, i: (b, i, 0)),
                  pl.BlockSpec((1, LANES), lambda b, i: (0, 0)),
                  pl.BlockSpec((1, LANES), lambda b, i: (0, 0))],
        out_specs=pl.BlockSpec((None, CH, LANES), lambda b, i: (b, i, 0)),
        out_shape=jax.ShapeDtypeStruct((bsz, t, LANES), F32),
        compiler_params=_params(("parallel", "parallel")),
        name="gate_prep",
    )(g, bias_row, alog_row)


def _chunk_order(j, d, n_chunks, n_ctx_chunks):
    if d == 0:
        return j
    return jnp.where(j < n_ctx_chunks, n_ctx_chunks - 1 - j, n_chunks + n_ctx_chunks - 1 - j)


def _head_spec(t, name):
    base = CB[name]
    return pl.BlockSpec((None, t, LANES), lambda b, h: (b, 0, base + h))


def _mlstm_kernel(q_ref, k_ref, v_ref, o_ref, a_ref, at_ref, y_ref, c_scr, y1_scr, *, n_chunks, n_ctx_chunks):
    h = pl.program_id(1)
    r = lax.broadcasted_iota(jnp.int32, (CH, CH), 0)
    c = lax.broadcasted_iota(jnp.int32, (CH, CH), 1)
    one_col = (lax.broadcasted_iota(jnp.int32, (CH, LANES), 1) == 0).astype(F32)

    masks = ((c <= r), (c >= r))
    lasts = (CH - 1, 0)
    dirs = range(2)
    c_scr[...] = jnp.zeros_like(c_scr)

    def step(j, ms):
        cis = [_chunk_order(j, d, n_chunks, n_ctx_chunks) for d in dirs]
        rows = [pl.ds(pl.multiple_of(ci * CH, CH), CH) for ci in cis]
        qs = [q_ref[rw, :] for rw in rows]
        ks = [k_ref[rw, :] for rw in rows]
        qk = [_dot_nt(qs[d], ks[d]) for d in dirs]
        pc = [_dot(qs[d], c_scr[d].astype(BF16)) for d in dirs]
        a_blk = [a_ref[rw, :] for rw in rows]
        at_blk = [at_ref[ci] for ci in cis]
        li_col = [_lane_col(a_blk[d], d * 4 + h) for d in dirs]
        b_col = [_lane_col(a_blk[d], 8 + d * 4 + h) for d in dirs]
        li_row = [_sub_row(at_blk[d], d * 4 + h) for d in dirs]
        b_row = [_sub_row(at_blk[d], 8 + d * 4 + h) for d in dirs]
        dm = [jnp.where(masks[d], b_col[d] - b_row[d] + li_row[d], NEG) for d in dirs]
        inter = [ms[d] + b_col[d] for d in dirs]
        m_t = [jnp.maximum(inter[d], jnp.max(dm[d], axis=1, keepdims=True)) for d in dirs]
        s = [(qk[d] * (jnp.exp(dm[d] - m_t[d]) * SCALE)).astype(BF16) for d in dirs]
        v_aug = [jnp.concatenate([v_ref[rw, :].astype(F32), one_col], axis=1) for rw in rows]
        sv = [_dot(s[d], v_aug[d].astype(BF16)) for d in dirs]

        b_end = [b_col[d][lasts[d]:lasts[d] + 1, :] for d in dirs]
        dec = [b_end[d] - b_col[d] + li_col[d] for d in dirs]
        m_new = [jnp.maximum(ms[d] + b_end[d], jnp.max(dec[d], axis=0, keepdims=True)) for d in dirs]
        wv = [(jnp.exp(dec[d] - m_new[d]) * v_aug[d]).astype(BF16) for d in dirs]
        upd = [_dot_tn(ks[d], wv[d]) for d in dirs]
        for d in dirs:
            c_scr[d] = jnp.exp(ms[d] + b_end[d] - m_new[d]) * c_scr[d] + upd[d]

        for d in dirs:
            nd = (jnp.exp(inter[d] - m_t[d]) * SCALE) * pc[d] + sv[d]
            h_out = nd[:, :LANES] / jnp.maximum(jnp.abs(nd[:, LANES:LANES + 1]), jnp.exp(-m_t[d]))
            if d == 0:
                y_ref[rows[d], :] = h_out
            else:
                y1_scr[rows[d], :] = h_out
        return tuple(m_new)

    zero = jnp.zeros((1, 1), F32)
    lax.fori_loop(0, n_chunks, step, (zero, zero))

    def finish(ci, carry):
        rw = pl.ds(pl.multiple_of(ci * CH, CH), CH)
        y_ref[rw, :] = (y_ref[rw, :] + y1_scr[rw, :]) * _sigmoid(o_ref[rw, :].astype(F32))
        return carry

    lax.fori_loop(0, n_chunks, finish, 0)


def _mlstm(u, a, at_ch, ctx_len):
    bsz, t, _ = u.shape
    n_chunks = t // CH
    kern = functools.partial(_mlstm_kernel, n_chunks=n_chunks, n_ctx_chunks=ctx_len // CH)
    return pl.pallas_call(
        kern,
        grid=(bsz, N_HEADS),
        in_specs=[_head_spec(t, "m_q"), _head_spec(t, "m_k"), _head_spec(t, "m_v"), _head_spec(t, "m_o"),
                  pl.BlockSpec((None, t, LANES), lambda b, h: (b, 0, 0)),
                  pl.BlockSpec((None, n_chunks, 32, CH), lambda b, h: (b, 0, 0, 0))],
        out_specs=pl.BlockSpec((None, t, LANES), lambda b, h: (b, 0, h)),
        out_shape=jax.ShapeDtypeStruct((bsz, t, GROUP_W), F32),
        scratch_shapes=[pltpu.VMEM((2, HEAD_DIM, 2 * LANES), F32), pltpu.VMEM((t, LANES), F32)],
        compiler_params=_params(("parallel", "parallel")),
        name="mlstm",
    )(u, u, u, u, a, at_ch)


def _retention_kernel(lg_ref, q_ref, k_ref, v_ref, cos_ref, sin_ref, y_ref, qp_scr, kp_scr, r_scr, y1_scr, *,
                      n_chunks, n_ctx_chunks):
    h = pl.program_id(1)

    def prep(ci, carry):
        rows = pl.ds(pl.multiple_of(ci * CH, CH), CH)
        cos = cos_ref[rows, :]
        sin = sin_ref[rows, :]
        qp_scr[rows, :] = _rope(q_ref[rows, :].astype(F32), cos, sin).astype(BF16)
        kp_scr[rows, :] = (_rope(k_ref[rows, :].astype(F32), cos, sin) * SCALE).astype(BF16)
        return carry

    lax.fori_loop(0, n_chunks, prep, 0)

    r = lax.broadcasted_iota(jnp.int32, (CH, CH), 0).astype(F32)
    c = lax.broadcasted_iota(jnp.int32, (CH, CH), 1).astype(F32)
    pos = lax.broadcasted_iota(jnp.int32, (CH, 1), 0).astype(F32)

    dirs = range(2)
    decay, q_decay, k_decay, chunk_decay = [], [], [], []
    for d in dirs:
        lg = lg_ref[d * N_HEADS + h]
        rel = (r - c) if d == 0 else (c - r)
        decay.append(jnp.where(rel >= 0, jnp.exp(lg * jnp.maximum(rel, 0.0)), 0.0))
        p_vis = pos if d == 0 else (CH - 1.0) - pos
        q_decay.append(jnp.exp(lg * (p_vis + 1.0)))
        k_decay.append(jnp.exp(lg * ((CH - 1.0) - p_vis)))
        chunk_decay.append(jnp.exp(lg * CH))
    r_scr[...] = jnp.zeros_like(r_scr)

    def step(j, carry):
        cis = [_chunk_order(j, d, n_chunks, n_ctx_chunks) for d in dirs]
        rows = [pl.ds(pl.multiple_of(ci * CH, CH), CH) for ci in cis]
        qs = [qp_scr[rw, :] for rw in rows]
        ks = [kp_scr[rw, :] for rw in rows]
        vs = [v_ref[rw, :] for rw in rows]
        qk = [_dot_nt(qs[d], ks[d]) for d in dirs]
        qr = [_dot(qs[d], r_scr[d].astype(BF16)) for d in dirs]
        sv = [_dot((qk[d] * decay[d]).astype(BF16), vs[d]) for d in dirs]
        upd = [_dot_tn((ks[d].astype(F32) * k_decay[d]).astype(BF16), vs[d]) for d in dirs]
        for d in dirs:
            r_scr[d] = chunk_decay[d] * r_scr[d] + upd[d]
        y_ref[rows[0], :] = sv[0] + q_decay[0] * qr[0]
        y1_scr[rows[1], :] = sv[1] + q_decay[1] * qr[1]
        return carry

    lax.fori_loop(0, n_chunks, step, 0)

    def finish(ci, carry):
        rw = pl.ds(pl.multiple_of(ci * CH, CH), CH)
        y_ref[rw, :] = y_ref[rw, :] + y1_scr[rw, :]
        return carry

    lax.fori_loop(0, n_chunks, finish, 0)


def _retention(u, log_gamma, cos_t, sin_t, ctx_len):
    bsz, t, _ = u.shape
    n_chunks = t // CH
    kern = functools.partial(_retention_kernel, n_chunks=n_chunks, n_ctx_chunks=ctx_len // CH)
    tab = pl.BlockSpec((t, LANES), lambda b, h: (0, 0))
    return pl.pallas_call(
        kern,
        grid=(bsz, N_HEADS),
        in_specs=[pl.BlockSpec(memory_space=pltpu.SMEM),
                  _head_spec(t, "r_q"), _head_spec(t, "r_k"), _head_spec(t, "r_v"), tab, tab],
        out_specs=pl.BlockSpec((None, t, LANES), lambda b, h: (b, 0, h)),
        out_shape=jax.ShapeDtypeStruct((bsz, t, GROUP_W), F32),
        scratch_shapes=[pltpu.VMEM((t, LANES), BF16), pltpu.VMEM((t, LANES), BF16),
                        pltpu.VMEM((2, HEAD_DIM, HEAD_DIM), F32), pltpu.VMEM((t, LANES), F32)],
        compiler_params=_params(("parallel", "parallel")),
        name="retention",
    )(log_gamma.reshape(2 * N_HEADS), u, u, u, cos_t, sin_t)


def _attn_kernel(q_ref, k_ref, v_ref, cos_ref, sin_ref, cosq_ref, sinq_ref, qw_ref, kw_ref, o_ref, kp_scr, *,
                 t, ctx_len):
    qi = pl.program_id(2)

    @pl.when(qi == 0)
    def _():
        def prep(ci, carry):
            rows = pl.ds(pl.multiple_of(ci * CH, CH), CH)
            k = k_ref[rows, :].astype(F32)
            k = k * lax.rsqrt(jnp.mean(k * k, axis=-1, keepdims=True) + EPS) * kw_ref[...]
            kp_scr[rows, :] = _rope(k, cos_ref[rows, :], sin_ref[rows, :]).astype(BF16)
            return carry

        lax.fori_loop(0, t // CH, prep, 0)

    qs = []
    for g in range(2):
        q = q_ref[:, g * LANES:(g + 1) * LANES].astype(F32)
        q = q * lax.rsqrt(jnp.mean(q * q, axis=-1, keepdims=True) + EPS) * qw_ref[...]
        qs.append((_rope(q, cosq_ref[...], sinq_ref[...]) * SCALE).astype(BF16))
    qq = jnp.concatenate(qs, axis=0)

    def attend(n_keys):
        s = _dot_nt(qq, kp_scr[0:n_keys, :])
        p = jnp.exp(s - jnp.max(s, axis=-1, keepdims=True))
        o = _dot(p.astype(BF16), v_ref[0:n_keys, :]) / jnp.sum(p, axis=-1, keepdims=True)
        o_ref[:, 0:LANES] = o[:CH]
        o_ref[:, LANES:2 * LANES] = o[CH:]

    @pl.when(qi == 0)
    def _():
        attend(ctx_len)

    @pl.when(qi > 0)
    def _():
        attend(t)


def _attention(u, cos_t, sin_t, qn_w, kn_w, ctx_len):
    bsz, t, _ = u.shape
    assert ctx_len == CH
    kern = functools.partial(_attn_kernel, t=t, ctx_len=ctx_len)
    qb = CB["a_q"] // 2
    kb = CB["a_k"]
    vb = CB["a_v"]
    tab = pl.BlockSpec((t, LANES), lambda b, kv, i: (0, 0))
    tabq = pl.BlockSpec((CH, LANES), lambda b, kv, i: (i, 0))
    vec = pl.BlockSpec((1, LANES), lambda b, kv, i: (0, 0))
    return pl.pallas_call(
        kern,
        grid=(bsz, KV_HEADS, t // CH),
        in_specs=[pl.BlockSpec((None, CH, 2 * LANES), lambda b, kv, i: (b, i, qb + kv)),
                  pl.BlockSpec((None, t, LANES), lambda b, kv, i: (b, 0, kb + kv)),
                  pl.BlockSpec((None, t, LANES), lambda b, kv, i: (b, 0, vb + kv)),
                  tab, tab, tabq, tabq, vec, vec],
        out_specs=pl.BlockSpec((None, CH, 2 * LANES), lambda b, kv, i: (b, i, kv)),
        out_shape=jax.ShapeDtypeStruct((bsz, t, GROUP_W), F32),
        scratch_shapes=[pltpu.VMEM((t, LANES), BF16)],
        compiler_params=_params(("parallel", "parallel", "arbitrary")),
        name="attention",
    )(u, u, u, cos_t, sin_t, cos_t, sin_t, qn_w.reshape(1, LANES), kn_w.reshape(1, LANES))


def _tri_inverse(n_mats):
    r = lax.broadcasted_iota(jnp.int32, (DC, DC), 0)
    c = lax.broadcasted_iota(jnp.int32, (DC, DC), 1)
    eye = (r == c).astype(F32)
    diag16 = (r // 16) == (c // 16)
    ms = [-jnp.where(diag16, n, 0.0) for n in n_mats]
    ps = [eye + m for m in ms]
    mps = [_split(m) for m in ms]
    for _ in range(3):
        mps = [_split(_dot3(mp, mp)) for mp in mps]
        ps = [p + _dot3(_split(p), mp) for p, mp in zip(ps, mps)]
    for w in (32, 64):
        off = ((r // w) == (c // w)) & ((r // (w // 2)) != (c // (w // 2)))
        pbs = [p.astype(BF16) for p in ps]
        tmp = [_dot(jnp.where(off, n, 0.0).astype(BF16), pb).astype(BF16) for n, pb in zip(n_mats, pbs)]
        ps = [p - _dot(pb, a) for p, pb, a in zip(ps, pbs, tmp)]
    return ps


def _deltanet_kernel(q_ref, k_ref, v_ref, wq_ref, wk_ref, wv_ref, a_ref, at_ref, y_ref,
                     xs_scr, qd_scr, kd_scr, vd_scr, aq_scr, b_scr, ge_scr, y1_scr, *,
                     t, ctx_len):
    h = pl.program_id(1)
    n_chunks = t // DC
    n_ctx_chunks = ctx_len // DC
    pad = 8

    def prep_stream(src_ref, w_ref, dst_scr, l2, scale):
        zeros = jnp.zeros((pad, LANES), F32)
        xs_scr[0:pad, :] = zeros
        xs_scr[t + pad:t + 2 * pad, :] = zeros

        def load(ci, carry):
            rows = pl.ds(pl.multiple_of(ci * CH, CH), CH)
            xs_scr[pl.ds(pl.multiple_of(ci * CH + pad, 8), CH), :] = src_ref[rows, :].astype(F32)
            return carry

        lax.fori_loop(0, t // CH, load, 0)
        w = w_ref[...]

        def conv(ci, carry):
            win = xs_scr[pl.ds(pl.multiple_of(ci * CH, CH), CH + 2 * pad), :]
            tok = ci * CH - pad + lax.broadcasted_iota(jnp.int32, (CH + 2 * pad, 1), 0)
            win = jnp.where((tok < ctx_len) == (ci * CH < ctx_len), win, 0.0)
            acc = win[pad - 2:pad - 2 + CH, :] * w[0:1, :]
            for j in range(1, CONV_K):
                acc = acc + win[pad - 2 + j:pad - 2 + j + CH, :] * w[j:j + 1, :]
            acc = acc * _sigmoid(acc)
            if l2:
                acc = acc * lax.rsqrt(jnp.sum(acc * acc, axis=-1, keepdims=True) + EPS) * scale
            dst_scr[pl.ds(pl.multiple_of(ci * CH, CH), CH), :] = acc
            return carry

        lax.fori_loop(0, t // CH, conv, 0)

    prep_stream(q_ref, wq_ref, qd_scr, True, SCALE)
    prep_stream(k_ref, wk_ref, kd_scr, True, 1.0)
    prep_stream(v_ref, wv_ref, vd_scr, False, 1.0)

    r = lax.broadcasted_iota(jnp.int32, (DC, DC), 0)
    c = lax.broadcasted_iota(jnp.int32, (DC, DC), 1)

    def pre(i, carry):
        chains = []
        n_mats = []
        for uu in range(PRE_UNROLL):
            ci = i * PRE_UNROLL + uu
            rows = pl.ds(pl.multiple_of(ci * DC, DC), DC)
            q = qd_scr[rows, :]
            k = kd_scr[rows, :]
            v = vd_scr[rows, :]
            a_blk = a_ref[rows, :]
            at_blk = at_ref[ci]
            kb = k.astype(BF16)
            kk = _dot_nt(kb, kb)
            qk = _dot_nt(q.astype(BF16), kb)
            for d in range(2):
                incl = (c <= r) if d == 0 else (c >= r)
                strict = (c < r) if d == 0 else (c > r)
                last = DC - 1 if d == 0 else 0
                g_col = _lane_col(a_blk, 16 + d * 4 + h)
                beta = _lane_col(a_blk, 24 + d * 4 + h)
                g_row = _sub_row(at_blk, 16 + d * 4 + h)
                decay = jnp.exp(jnp.where(incl, g_col - g_row, NEG))
                eg = jnp.exp(g_col)
                g_end = g_col[last:last + 1, :]
                n_mats.append(jnp.where(strict, beta * decay * kk, 0.0))
                rhs = jnp.concatenate([beta * v, (beta * eg) * k], axis=1).astype(BF16)
                ke = (jnp.exp(g_end - g_col) * k).astype(BF16)
                ge_scr[d, ci] = jnp.broadcast_to(jnp.exp(g_end), (8, LANES))
                chains.append((d, ci, rows, rhs, ke, (qk * decay).astype(BF16), eg * q))
        t_invs = _tri_inverse(n_mats)
        ws = [_dot(t_inv.astype(BF16), ch[3]).astype(BF16) for t_inv, ch in zip(t_invs, chains)]
        kws = [_dot_tn(ch[4], w) for w, ch in zip(ws, chains)]
        qws = [_dot(ch[5], w) for w, ch in zip(ws, chains)]
        for kw, qw, (d, ci, rows, _, _, _, egq) in zip(kws, qws, chains):
            aq_scr[d, ci, 0:HEAD_DIM, :] = (-kw[:, LANES:]).astype(BF16)
            aq_scr[d, ci, HEAD_DIM:HEAD_DIM + DC, :] = (egq - qw[:, LANES:]).astype(BF16)
            b_scr[d, ci] = kw[:, :LANES]
            if d == 0:
                y_ref[rows, :] = qw[:, :LANES]
            else:
                y1_scr[rows, :] = qw[:, :LANES]
        return carry

    lax.fori_loop(0, n_chunks // PRE_UNROLL, pre, 0)

    def scan(j, carry):
        cis = [_chunk_order(j, d, n_chunks, n_ctx_chunks) for d in range(2)]
        xs = [_dot(aq_scr[d, cis[d]], carry[d].astype(BF16)) for d in range(2)]
        new = tuple(ge_scr[d, cis[d]][0:1, :] * carry[d] + (xs[d][:HEAD_DIM, :] + b_scr[d, cis[d]])
                    for d in range(2))
        rows = [pl.ds(pl.multiple_of(cis[d] * DC, DC), DC) for d in range(2)]
        y_ref[rows[0], :] = y_ref[rows[0], :] + xs[0][HEAD_DIM:, :]
        y1_scr[rows[1], :] = y1_scr[rows[1], :] + xs[1][HEAD_DIM:, :]
        return new

    zero = jnp.zeros((HEAD_DIM, HEAD_DIM), F32)
    lax.fori_loop(0, n_chunks, scan, (zero, zero))

    def add(ci, carry):
        rows = pl.ds(pl.multiple_of(ci * CH, CH), CH)
        y_ref[rows, :] = y_ref[rows, :] + y1_scr[rows, :]
        return carry

    lax.fori_loop(0, t // CH, add, 0)


def _deltanet(u, conv_w, a, at_dc, ctx_len):
    bsz, t, _ = u.shape
    n_chunks = t // DC
    kern = functools.partial(_deltanet_kernel, t=t, ctx_len=ctx_len)

    def wspec(off):
        return pl.BlockSpec((CONV_K, LANES), lambda b, h: (0, off + h))

    return pl.pallas_call(
        kern,
        grid=(bsz, N_HEADS),
        in_specs=[_head_spec(t, "d_q"), _head_spec(t, "d_k"), _head_spec(t, "d_v"),
                  wspec(0), wspec(N_HEADS), wspec(2 * N_HEADS),
                  pl.BlockSpec((None, t, LANES), lambda b, h: (b, 0, 0)),
                  pl.BlockSpec((None, n_chunks, 32, DC), lambda b, h: (b, 0, 0, 0))],
        out_specs=pl.BlockSpec((None, t, LANES), lambda b, h: (b, 0, h)),
        out_shape=jax.ShapeDtypeStruct((bsz, t, GROUP_W), F32),
        scratch_shapes=[pltpu.VMEM((t + 16, LANES), F32),
                        pltpu.VMEM((t, LANES), F32), pltpu.VMEM((t, LANES), F32), pltpu.VMEM((t, LANES), F32),
                        pltpu.VMEM((2, n_chunks, HEAD_DIM + DC, LANES), BF16),
                        pltpu.VMEM((2, n_chunks, HEAD_DIM, LANES), F32),
                        pltpu.VMEM((2, n_chunks, 8, LANES), F32),
                        pltpu.VMEM((t, LANES), F32)],
        compiler_params=_params(("parallel", "parallel")),
        name="deltanet",
    )(u, u, u, conv_w, conv_w, conv_w, a, at_dc)


def _out_kernel(x_ref, ym_ref, yr_ref, ya_ref, yd_ref, zm_ref, zr_ref, za_ref, zd_ref, hn_ref, w_ref,
                mod_ref, pw_ref, o_ref, *, first_block):
    b = pl.program_id(0)
    i = pl.program_id(1) + first_block

    def silu(z_ref):
        z = z_ref[...].astype(F32)
        return z * _sigmoid(z)

    def head_rms(y_ref, g):
        y = y_ref[...]
        parts = []
        for hh in range(N_HEADS):
            p = y[:, hh * LANES:(hh + 1) * LANES]
            parts.append(p * lax.rsqrt(jnp.mean(p * p, axis=-1, keepdims=True) + EPS))
        return jnp.concatenate(parts, axis=1) * hn_ref[:, g * GROUP_W:(g + 1) * GROUP_W]

    merged = jnp.concatenate([
        (head_rms(ym_ref, 0) * silu(zm_ref)).astype(BF16),
        (head_rms(yr_ref, 1) * silu(zr_ref)).astype(BF16),
        (ya_ref[...] * silu(za_ref)).astype(BF16),
        (head_rms(yd_ref, 2) * silu(zd_ref)).astype(BF16)], axis=1)
    o = _dot(merged, w_ref[...])
    o = o * lax.rsqrt(jnp.mean(o * o, axis=-1, keepdims=True) + EPS) * pw_ref[...]
    row = jnp.where(i == 0, 4, b)
    gate = mod_ref[pl.ds(row, 1), 2 * D_MODEL:3 * D_MODEL]
    o_ref[...] = x_ref[...] + gate * o


def _out_projection(xa, ys, u, hn_w, w_out, mod, post_w, ctx_len, latent_only):
    bsz, t, d = xa.shape
    assert ctx_len == CH
    first = 1 if latent_only else 0
    tok = lambda b, i: (b, i + first, 0)
    yspec = pl.BlockSpec((None, CH, GROUP_W), tok)

    def zspec(name):
        blk = CB[name] // N_HEADS
        return pl.BlockSpec((None, CH, GROUP_W), lambda b, i: (b, i + first, blk))

    return pl.pallas_call(
        functools.partial(_out_kernel, first_block=first),
        grid=(bsz, t // CH - first),
        in_specs=[pl.BlockSpec((None, CH, d), tok), yspec, yspec, yspec, yspec,
                  zspec("m_z"), zspec("r_z"), zspec("a_z"), zspec("d_z"),
                  pl.BlockSpec((1, 3 * GROUP_W), lambda b, i: (0, 0)),
                  pl.BlockSpec((d, d), lambda b, i: (0, 0)),
                  pl.BlockSpec((8, 3 * d), lambda b, i: (0, 0)),
                  pl.BlockSpec((1, d), lambda b, i: (0, 0))],
        out_specs=pl.BlockSpec((None, CH, d), lambda b, i: (b, i, 0)),
        out_shape=jax.ShapeDtypeStruct((bsz, t - first * CH, d), F32),
        compiler_params=_params(("parallel", "parallel")),
        name="out_proj",
    )(xa, *ys, u, u, u, u, hn_w.reshape(1, 3 * GROUP_W), w_out, mod, post_w.reshape(1, d))


def _reorder_w_in(w):
    g = GROUP_W
    m_gate = 5 * g
    r_start = m_gate + 4 * N_HEADS
    d_gate = r_start + 4 * g + (2 * g + 2 * KV_HEADS * HEAD_DIM) + 4 * g
    w_main = jnp.concatenate([w[:, :m_gate], w[:, r_start:d_gate]], axis=1).astype(BF16)
    gates = jnp.concatenate([w[:, m_gate:r_start], w[:, d_gate:]], axis=1)
    w_gate = jnp.pad(gates, ((0, 0), (0, LANES - gates.shape[1]))).astype(BF16)
    return w_main, w_gate


def _rope_tables(seq, ctx_len):
    rows = seq // GRID_W
    row = jnp.repeat(jnp.arange(rows), GRID_W)
    col = jnp.tile(jnp.arange(GRID_W), rows)
    n_freq = HEAD_DIM // 4
    inv_freq = ROPE_BASE ** (-jnp.arange(n_freq, dtype=F32) / n_freq)
    ar = row[:, None] * inv_freq
    ac = col[:, None] * inv_freq
    cos = jnp.concatenate([jnp.cos(ar), jnp.cos(ar), jnp.cos(ac), jnp.cos(ac)], axis=1)
    sin = jnp.concatenate([-jnp.sin(ar), jnp.sin(ar), -jnp.sin(ac), jnp.sin(ac)], axis=1)
    cos = jnp.concatenate([jnp.ones((ctx_len, HEAD_DIM), F32), cos], axis=0)
    sin = jnp.concatenate([jnp.zeros((ctx_len, HEAD_DIM), F32), sin], axis=0)
    return cos, sin


def kernel(x, c, ctx, c_ctx, ada_w, ada_b, pre_norm_w, post_norm_w, w_in, w_out, mlstm_i_bias, mlstm_f_bias,
           ret_log_gamma, attn_q_norm_w, attn_k_norm_w, dn_conv_w, dn_a_log, dn_dt_bias, head_norm_w):
    bsz, seq, d = x.shape
    ctx_len = ctx.shape[1]
    t = ctx_len + seq
    depth = ada_w.shape[0]
    assert bsz <= 4 and ctx_len % CH == 0 and seq % CH == 0

    xa = jnp.concatenate([ctx, x], axis=1)
    c8 = jnp.zeros((8, d), F32).at[:bsz].set(c).at[4].set(c_ctx)
    mod = _modulation(c8, ada_w, ada_b)
    cos_t, sin_t = _rope_tables(seq, ctx_len)

    for l in range(depth):
        w_main, w_gate = _reorder_w_in(w_in[l])
        u, g = _in_projection(xa, mod[l], pre_norm_w[l], w_main, w_gate, ctx_len)
        zeros8 = jnp.zeros((8,), F32)
        bias_row = jnp.pad(jnp.concatenate([mlstm_i_bias[l].reshape(-1), mlstm_f_bias[l].reshape(-1),
                                            dn_dt_bias[l].reshape(-1), zeros8]), (0, LANES - 32))
        alog_row = jnp.pad(jnp.concatenate([zeros8, zeros8, dn_a_log[l].reshape(-1), zeros8]), (0, LANES - 32))
        a = _gate_prep(g, bias_row.reshape(1, LANES), alog_row.reshape(1, LANES))
        at = jnp.swapaxes(a[:, :, :32], 1, 2)
        at_ch = at.reshape(bsz, 32, t // CH, CH).transpose(0, 2, 1, 3)
        at_dc = at.reshape(bsz, 32, t // DC, DC).transpose(0, 2, 1, 3)
        y_m = _mlstm(u, a, at_ch, ctx_len)
        y_r = _retention(u, ret_log_gamma[l], cos_t, sin_t, ctx_len)
        y_a = _attention(u, cos_t, sin_t, attn_q_norm_w[l], attn_k_norm_w[l], ctx_len)
        y_d = _deltanet(u, dn_conv_w[l], a, at_dc, ctx_len)
        xa = _out_projection(xa, (y_m, y_r, y_a, y_d), u, head_norm_w[l], w_out[l].astype(BF16), mod[l],
                             post_norm_w[l], ctx_len, latent_only=(l == depth - 1))
    return xa
```

```python
import functools

import jax
import jax.numpy as jnp
from jax import lax
from jax.experimental import pallas as pl
from jax.experimental.pallas import tpu as pltpu

F32 = jnp.float32
BF16 = jnp.bfloat16
HI = lax.Precision.HIGHEST

D_MODEL = 2048
GROUP_W = 512
N_HEADS = 4
HEAD_DIM = 128
KV_HEADS = 2
GRID_W = 64
CONV_K = 5
ROPE_BASE = 10000.0
EPS = 1e-6
SCALE = HEAD_DIM ** -0.5
LOG2E = 1.4426950408889634
NEG = -1e30

LANES = 128
CH = 256
DC = 64
PRE_UNROLL = 4
N_MAIN = 64 * LANES
VMEM_LIMIT = 56 * 1024 * 1024

CB = dict(m_q=0, m_k=4, m_v=8, m_o=12, m_z=16, r_q=20, r_k=24, r_v=28, r_z=32,
          a_q=36, a_k=40, a_v=42, a_z=44, d_q=48, d_k=52, d_v=56, d_z=60)


def _dot(a, b, prec=None):
    return jnp.dot(a, b, preferred_element_type=F32, precision=prec)


def _dot_nt(a, b, prec=None):
    return lax.dot_general(a, b, (((1,), (1,)), ((), ())), preferred_element_type=F32, precision=prec)


def _dot_tn(a, b, prec=None):
    return lax.dot_general(a, b, (((0,), (0,)), ((), ())), preferred_element_type=F32, precision=prec)


def _split(a):
    hi = a.astype(BF16)
    return hi, (a - hi.astype(F32)).astype(BF16)


def _dot3(a, b):
    n = a[0].shape[0]
    x = _dot(jnp.concatenate([a[0], a[1]], axis=0), b[0])
    return (x[:n] + x[n:]) + _dot(a[0], b[1])


def _sigmoid(x):
    return 1.0 / (1.0 + jnp.exp(-x))


def _lane_col(blk, idx):
    lane = lax.broadcasted_iota(jnp.int32, blk.shape, 1)
    return jnp.sum(jnp.where(lane == idx, blk, 0.0), axis=1, keepdims=True)


def _sub_row(blk, idx):
    sub = lax.broadcasted_iota(jnp.int32, blk.shape, 0)
    return jnp.sum(jnp.where(sub == idx, blk, 0.0), axis=0, keepdims=True)


def _rope(t, cos, sin_signed):
    lane = lax.broadcasted_iota(jnp.int32, t.shape, 1)
    partner = jnp.where((lane // 32) % 2 == 0, pltpu.roll(t, 96, 1), pltpu.roll(t, 32, 1))
    return t * cos + partner * sin_signed


def _params(sem):
    return pltpu.CompilerParams(dimension_semantics=sem, vmem_limit_bytes=VMEM_LIMIT)


def _mod_kernel(c_ref, w_ref, b_ref, o_ref):
    c = c_ref[...]
    o_ref[...] = _dot(c * _sigmoid(c), w_ref[...], HI) + b_ref[...]


def _modulation(c8, ada_w, ada_b):
    depth, d, n3 = ada_w.shape
    tn = 768
    return pl.pallas_call(
        _mod_kernel,
        grid=(depth, n3 // tn),
        in_specs=[pl.BlockSpec((8, d), lambda l, n: (0, 0)),
                  pl.BlockSpec((None, d, tn), lambda l, n: (l, 0, n)),
                  pl.BlockSpec((None, 1, tn), lambda l, n: (l, 0, n))],
        out_specs=pl.BlockSpec((None, 8, tn), lambda l, n: (l, 0, n)),
        out_shape=jax.ShapeDtypeStruct((depth, 8, n3), F32),
        compiler_params=_params(("parallel", "parallel")),
        name="adaln_mod",
    )(c8, ada_w, ada_b.reshape(depth, 1, n3))


def _inproj_kernel(x_ref, mod_ref, pw_ref, wm_ref, wg_ref, u_ref, g_ref, h_scr, *, tm, ctx_len):
    b = pl.program_id(0)
    i = pl.program_id(1)
    n = pl.program_id(2)

    @pl.when(n == 0)
    def _():
        x = x_ref[...]
        y = x * lax.rsqrt(jnp.mean(x * x, axis=-1, keepdims=True) + EPS) * pw_ref[...]
        row = i * tm + lax.broadcasted_iota(jnp.int32, (tm, 1), 0)
        is_ctx = row < ctx_len
        d = D_MODEL
        sh = jnp.where(is_ctx, mod_ref[4:5, 0:d], mod_ref[pl.ds(b, 1), 0:d])
        sc = jnp.where(is_ctx, mod_ref[4:5, d:2 * d], mod_ref[pl.ds(b, 1), d:2 * d])
        hh = (y * (1.0 + sc) + sh).astype(BF16)
        h_scr[...] = hh
        g_ref[...] = _dot(hh, wg_ref[...])

    u_ref[...] = _dot(h_scr[...], wm_ref[...]).astype(BF16)


def _in_projection(xa, mod, pre_w, w_main, w_gate, ctx_len):
    bsz, t, d = xa.shape
    tm = t // 4 if (t % 4 == 0 and (t // 4) % 16 == 0) else CH
    tn = 1024
    kern = functools.partial(_inproj_kernel, tm=tm, ctx_len=ctx_len)
    return pl.pallas_call(
        kern,
        grid=(bsz, t // tm, N_MAIN // tn),
        in_specs=[pl.BlockSpec((None, tm, d), lambda b, i, n: (b, i, 0)),
                  pl.BlockSpec((8, 3 * d), lambda b, i, n: (0, 0)),
                  pl.BlockSpec((1, d), lambda b, i, n: (0, 0)),
                  pl.BlockSpec((d, tn), lambda b, i, n: (0, n)),
                  pl.BlockSpec((d, LANES), lambda b, i, n: (0, 0))],
        out_specs=[pl.BlockSpec((None, tm, tn), lambda b, i, n: (b, i, n)),
                   pl.BlockSpec((None, tm, LANES), lambda b, i, n: (b, i, 0))],
        out_shape=[jax.ShapeDtypeStruct((bsz, t, N_MAIN), BF16),
                   jax.ShapeDtypeStruct((bsz, t, LANES), F32)],
        scratch_shapes=[pltpu.VMEM((tm, d), BF16)],
        compiler_params=_params(("parallel", "parallel", "arbitrary")),
        name="in_proj",
    )(xa, mod, pre_w.reshape(1, d), w_main, w_gate)


def _gate_kernel(g_ref, bias_ref, alog_ref, a_ref):
    x = g_ref[...] + bias_ref[...]
    lane = lax.broadcasted_iota(jnp.int32, x.shape, 1)
    l1p = jnp.log(1.0 + jnp.exp(-jnp.abs(x)))
    log_f = jnp.minimum(x, 0.0) - l1p
    log_a = -jnp.exp(alog_ref[...]) * (jnp.maximum(x, 0.0) + l1p)
    v = jnp.where(lane < 8, x, jnp.where(lane < 16, log_f, jnp.where(lane < 24, log_a, _sigmoid(x))))
    r = lax.broadcasted_iota(jnp.int32, (CH, CH), 0)
    c = lax.broadcasted_iota(jnp.int32, (CH, CH), 1)
    same = (r // DC) == (c // DC)
    pre = (c <= r).astype(F32)
    suf = (c >= r).astype(F32)
    pre_dc = jnp.where(same, pre, 0.0)
    suf_dc = jnp.where(same, suf, 0.0)
    fwd = (lane % 8) < 4
    cum_ch = jnp.where(fwd, _dot(pre, v, HI), _dot(suf, v, HI))
    cum_dc = jnp.where(fwd, _dot(pre_dc, v, HI), _dot(suf_dc, v, HI))
    a_ref[...] = jnp.where((lane >= 8) & (lane < 16), cum_ch, jnp.where((lane >= 16) & (lane < 24), cum_dc, v))


def _gate_prep(g, bias_row, alog_row):
    bsz, t, _ = g.shape
    return pl.pallas_call(
        _gate_kernel,
        grid=(bsz, t // CH),
        in_specs=[pl.BlockSpec((None, CH, LANES), lambda b, i: (b, i, 0)),
                  pl.BlockSpec((1, LANES), lambda b, i: (0, 0)),
                  pl.BlockSpec((1, LANES), lambda b, i: (0, 0))],
        out_specs=pl.BlockSpec((None, CH, LANES), lambda b, i: (b, i, 0)),
        out_shape=jax.ShapeDtypeStruct((bsz, t, LANES), F32),
        compiler_params=_params(("parallel", "parallel")),
        name="gate_prep",
    )(g, bias_row, alog_row)


def _chunk_order(j, d, n_chunks, n_ctx_chunks):
    if d == 0:
        return j
    return jnp.where(j < n_ctx_chunks, n_ctx_chunks - 1 - j, n_chunks + n_ctx_chunks - 1 - j)


def _head_spec(t, name):
    base = CB[name]
    return pl.BlockSpec((None, t, LANES), lambda b, h: (b, 0, base + h))


def _mlstm_kernel(q_ref, k_ref, v_ref, o_ref, a_ref, at_ref, y_ref, c_scr, y1_scr, *, n_chunks, n_ctx_chunks):
    h = pl.program_id(1)
    r = lax.broadcasted_iota(jnp.int32, (CH, CH), 0)
    c = lax.broadcasted_iota(jnp.int32, (CH, CH), 1)
    one_col = (lax.broadcasted_iota(jnp.int32, (CH, LANES), 1) == 0).astype(F32)

    masks = ((c <= r), (c >= r))
    lasts = (CH - 1, 0)
    dirs = range(2)
    c_scr[...] = jnp.zeros_like(c_scr)

    def step(j, ms):
        cis = [_chunk_order(j, d, n_chunks, n_ctx_chunks) for d in dirs]
        rows = [pl.ds(pl.multiple_of(ci * CH, CH), CH) for ci in cis]
        qs = [q_ref[rw, :] for rw in rows]
        ks = [k_ref[rw, :] for rw in rows]
        qk = [_dot_nt(qs[d], ks[d]) for d in dirs]
        pc = [_dot(qs[d], c_scr[d].astype(BF16)) for d in dirs]
        a_blk = [a_ref[rw, :] for rw in rows]
        at_blk = [at_ref[ci] for ci in cis]
        li_col = [_lane_col(a_blk[d], d * 4 + h) for d in dirs]
        b_col = [_lane_col(a_blk[d], 8 + d * 4 + h) for d in dirs]
        li_row = [_sub_row(at_blk[d], d * 4 + h) for d in dirs]
        b_row = [_sub_row(at_blk[d], 8 + d * 4 + h) for d in dirs]
        dm = [jnp.where(masks[d], b_col[d] - b_row[d] + li_row[d], NEG) for d in dirs]
        inter = [ms[d] + b_col[d] for d in dirs]
        m_t = [jnp.maximum(inter[d], jnp.max(dm[d], axis=1, keepdims=True)) for d in dirs]
        s = [(qk[d] * (jnp.exp(dm[d] - m_t[d]) * SCALE)).astype(BF16) for d in dirs]
        v_aug = [jnp.concatenate([v_ref[rw, :].astype(F32), one_col], axis=1) for rw in rows]
        sv = [_dot(s[d], v_aug[d].astype(BF16)) for d in dirs]

        b_end = [b_col[d][lasts[d]:lasts[d] + 1, :] for d in dirs]
        dec = [b_end[d] - b_col[d] + li_col[d] for d in dirs]
        m_new = [jnp.maximum(ms[d] + b_end[d], jnp.max(dec[d], axis=0, keepdims=True)) for d in dirs]
        wv = [(jnp.exp(dec[d] - m_new[d]) * v_aug[d]).astype(BF16) for d in dirs]
        upd = [_dot_tn(ks[d], wv[d]) for d in dirs]
        for d in dirs:
            c_scr[d] = jnp.exp(ms[d] + b_end[d] - m_new[d]) * c_scr[d] + upd[d]

        for d in dirs:
            nd = (jnp.exp(inter[d] - m_t[d]) * SCALE) * pc[d] + sv[d]
            h_out = nd[:, :LANES] / jnp.maximum(jnp.abs(nd[:, LANES:LANES + 1]), jnp.exp(-m_t[d]))
            if d == 0:
                y_ref[rows[d], :] = h_out
            else:
                y1_scr[rows[d], :] = h_out
        return tuple(m_new)

    zero = jnp.zeros((1, 1), F32)
    lax.fori_loop(0, n_chunks, step, (zero, zero))

    def finish(ci, carry):
        rw = pl.ds(pl.multiple_of(ci * CH, CH), CH)
        y_ref[rw, :] = (y_ref[rw, :] + y1_scr[rw, :]) * _sigmoid(o_ref[rw, :].astype(F32))
        return carry

    lax.fori_loop(0, n_chunks, finish, 0)


def _mlstm(u, a, at_ch, ctx_len):
    bsz, t, _ = u.shape
    n_chunks = t // CH
    kern = functools.partial(_mlstm_kernel, n_chunks=n_chunks, n_ctx_chunks=ctx_len // CH)
    return pl.pallas_call(
        kern,
        grid=(bsz, N_HEADS),
        in_specs=[_head_spec(t, "m_q"), _head_spec(t, "m_k"), _head_spec(t, "m_v"), _head_spec(t, "m_o"),
                  pl.BlockSpec((None, t, LANES), lambda b, h: (b, 0, 0)),
                  pl.BlockSpec((None, n_chunks, 32, CH), lambda b, h: (b, 0, 0, 0))],
        out_specs=pl.BlockSpec((None, t, LANES), lambda b, h: (b, 0, h)),
        out_shape=jax.ShapeDtypeStruct((bsz, t, GROUP_W), F32),
        scratch_shapes=[pltpu.VMEM((2, HEAD_DIM, 2 * LANES), F32), pltpu.VMEM((t, LANES), F32)],
        compiler_params=_params(("parallel", "parallel")),
        name="mlstm",
    )(u, u, u, u, a, at_ch)


def _retention_kernel(lg_ref, q_ref, k_ref, v_ref, cos_ref, sin_ref, y_ref, qp_scr, kp_scr, r_scr, y1_scr, *,
                      n_chunks, n_ctx_chunks):
    h = pl.program_id(1)

    def prep(ci, carry):
        rows = pl.ds(pl.multiple_of(ci * CH, CH), CH)
        cos = cos_ref[rows, :]
        sin = sin_ref[rows, :]
        qp_scr[rows, :] = _rope(q_ref[rows, :].astype(F32), cos, sin).astype(BF16)
        kp_scr[rows, :] = (_rope(k_ref[rows, :].astype(F32), cos, sin) * SCALE).astype(BF16)
        return carry

    lax.fori_loop(0, n_chunks, prep, 0)

    r = lax.broadcasted_iota(jnp.int32, (CH, CH), 0).astype(F32)
    c = lax.broadcasted_iota(jnp.int32, (CH, CH), 1).astype(F32)
    pos = lax.broadcasted_iota(jnp.int32, (CH, 1), 0).astype(F32)

    dirs = range(2)
    decay, q_decay, k_decay, chunk_decay = [], [], [], []
    for d in dirs:
        lg = lg_ref[d * N_HEADS + h]
        rel = (r - c) if d == 0 else (c - r)
        decay.append(jnp.where(rel >= 0, jnp.exp(lg * jnp.maximum(rel, 0.0)), 0.0))
        p_vis = pos if d == 0 else (CH - 1.0) - pos
        q_decay.append(jnp.exp(lg * (p_vis + 1.0)))
        k_decay.append(jnp.exp(lg * ((CH - 1.0) - p_vis)))
        chunk_decay.append(jnp.exp(lg * CH))
    r_scr[...] = jnp.zeros_like(r_scr)

    def step(j, carry):
        cis = [_chunk_order(j, d, n_chunks, n_ctx_chunks) for d in dirs]
        rows = [pl.ds(pl.multiple_of(ci * CH, CH), CH) for ci in cis]
        qs = [qp_scr[rw, :] for rw in rows]
        ks = [kp_scr[rw, :] for rw in rows]
        vs = [v_ref[rw, :] for rw in rows]
        qk = [_dot_nt(qs[d], ks[d]) for d in dirs]
        qr = [_dot(qs[d], r_scr[d].astype(BF16)) for d in dirs]
        sv = [_dot((qk[d] * decay[d]).astype(BF16), vs[d]) for d in dirs]
        upd = [_dot_tn((ks[d].astype(F32) * k_decay[d]).astype(BF16), vs[d]) for d in dirs]
        for d in dirs:
            r_scr[d] = chunk_decay[d] * r_scr[d] + upd[d]
        y_ref[rows[0], :] = sv[0] + q_decay[0] * qr[0]
        y1_scr[rows[1], :] = sv[1] + q_decay[1] * qr[1]
        return carry

    lax.fori_loop(0, n_chunks, step, 0)

    def finish(ci, carry):
        rw = pl.ds(pl.multiple_of(ci * CH, CH), CH)
        y_ref[rw, :] = y_ref[rw, :] + y1_scr[rw, :]
        return carry

    lax.fori_loop(0, n_chunks, finish, 0)


def _retention(u, log_gamma, cos_t, sin_t, ctx_len):
    bsz, t, _ = u.shape
    n_chunks = t // CH
    kern = functools.partial(_retention_kernel, n_chunks=n_chunks, n_ctx_chunks=ctx_len // CH)
    tab = pl.BlockSpec((t, LANES), lambda b, h: (0, 0))
    return pl.pallas_call(
        kern,
        grid=(bsz, N_HEADS),
        in_specs=[pl.BlockSpec(memory_space=pltpu.SMEM),
                  _head_spec(t, "r_q"), _head_spec(t, "r_k"), _head_spec(t, "r_v"), tab, tab],
        out_specs=pl.BlockSpec((None, t, LANES), lambda b, h: (b, 0, h)),
        out_shape=jax.ShapeDtypeStruct((bsz, t, GROUP_W), F32),
        scratch_shapes=[pltpu.VMEM((t, LANES), BF16), pltpu.VMEM((t, LANES), BF16),
                        pltpu.VMEM((2, HEAD_DIM, HEAD_DIM), F32), pltpu.VMEM((t, LANES), F32)],
        compiler_params=_params(("parallel", "parallel")),
        name="retention",
    )(log_gamma.reshape(2 * N_HEADS), u, u, u, cos_t, sin_t)


def _attn_kernel(q_ref, k_ref, v_ref, cos_ref, sin_ref, cosq_ref, sinq_ref, qw_ref, kw_ref, o_ref,
                 kt_scr, s_scr, m_scr, *, n_blocks, with_ctx):
    j = pl.program_id(2)
    n_lat = n_blocks - 1

    def load_q():
        qs = []
        for g in range(2):
            q = q_ref[:, g * LANES:(g + 1) * LANES].astype(F32)
            q = q * lax.rsqrt(jnp.mean(q * q, axis=-1, keepdims=True) + EPS) * qw_ref[...]
            qs.append((_rope(q, cosq_ref[...], sinq_ref[...]) * (SCALE * LOG2E)).astype(BF16))
        return jnp.concatenate(qs, axis=0)

    def finish(o, l_part):
        o = o / jnp.sum(l_part, axis=-1, keepdims=True)
        o_ref[:, 0:LANES] = o[:CH]
        o_ref[:, LANES:2 * LANES] = o[CH:]

    @pl.when(j == 0)
    def _():
        def prep(ci, carry):
            rows = pl.ds(pl.multiple_of(ci * CH, CH), CH)
            k = k_ref[rows, :].astype(F32)
            k = k * lax.rsqrt(jnp.mean(k * k, axis=-1, keepdims=True) + EPS) * kw_ref[...]
            kt_scr[ci] = _rope(k, cos_ref[rows, :], sin_ref[rows, :]).T.astype(BF16)
            return carry

        lax.fori_loop(0, n_blocks, prep, 0)
        if with_ctx:
            s = _dot(load_q(), kt_scr[0])
            p = jnp.exp2(s - jnp.max(s, axis=-1, keepdims=True))
            finish(_dot(p.astype(BF16), v_ref[0:CH, :]), p[:, :LANES] + p[:, LANES:])

    @pl.when(j >= 1)
    def _():
        qq = load_q()
        mx = None
        for kb in range(n_blocks):
            s = _dot(qq, kt_scr[kb])
            s_scr[:, kb * CH:(kb + 1) * CH] = s
            m2 = jnp.maximum(s[:, :LANES], s[:, LANES:])
            mx = m2 if mx is None else jnp.maximum(mx, m2)
        m_scr[...] = jnp.broadcast_to(jnp.max(mx, axis=-1, keepdims=True), (2 * CH, LANES))

    def value_pass(_, carry):
        m = m_scr[...]
        l_acc = o = None
        for kb in range(n_blocks):
            cols = slice(kb * CH, (kb + 1) * CH)
            p_lo = jnp.exp2(s_scr[:, kb * CH:kb * CH + LANES] - m)
            p_hi = jnp.exp2(s_scr[:, kb * CH + LANES:(kb + 1) * CH] - m)
            l_acc = (p_lo + p_hi) if l_acc is None else l_acc + (p_lo + p_hi)
            pv = _dot(jnp.concatenate([p_lo, p_hi], axis=1).astype(BF16), v_ref[cols, :])
            o = pv if o is None else o + pv
        finish(o, l_acc)
        return carry

    lax.fori_loop(0, jnp.minimum(j, 1), value_pass, 0)


def _attention(u, cos_t, sin_t, qn_w, kn_w, ctx_len, with_ctx):
    bsz, t, _ = u.shape
    assert ctx_len == CH
    n_blocks = t // CH
    kern = functools.partial(_attn_kernel, n_blocks=n_blocks, with_ctx=with_ctx)
    qb = CB["a_q"] // 2
    kb = CB["a_k"]
    vb = CB["a_v"]
    first = 0 if with_ctx else 1
    qblk = lambda j: j
    oblk = lambda j: jnp.maximum(j, first)
    tab = pl.BlockSpec((t, LANES), lambda b, kv, j: (0, 0))
    tabq = pl.BlockSpec((CH, LANES), lambda b, kv, j: (qblk(j), 0))
    vec = pl.BlockSpec((1, LANES), lambda b, kv, j: (0, 0))
    return pl.pallas_call(
        kern,
        grid=(bsz, KV_HEADS, n_blocks),
        in_specs=[pl.BlockSpec((None, CH, 2 * LANES), lambda b, kv, j: (b, qblk(j), qb + kv)),
                  pl.BlockSpec((None, t, LANES), lambda b, kv, j: (b, 0, kb + kv)),
                  pl.BlockSpec((None, t, LANES), lambda b, kv, j: (b, 0, vb + kv)),
                  tab, tab, tabq, tabq, vec, vec],
        out_specs=pl.BlockSpec((None, CH, 2 * LANES), lambda b, kv, j: (b, oblk(j), kv)),
        out_shape=jax.ShapeDtypeStruct((bsz, t, GROUP_W), F32),
        scratch_shapes=[pltpu.VMEM((n_blocks, HEAD_DIM, CH), BF16), pltpu.VMEM((2 * CH, t), F32),
                        pltpu.VMEM((2 * CH, LANES), F32)],
        compiler_params=_params(("parallel", "parallel", "arbitrary")),
        name="attention",
    )(u, u, u, cos_t, sin_t, cos_t, sin_t, qn_w.reshape(1, LANES), kn_w.reshape(1, LANES))


def _tri_inverse(n_mats):
    r = lax.broadcasted_iota(jnp.int32, (DC, DC), 0)
    c = lax.broadcasted_iota(jnp.int32, (DC, DC), 1)
    eye = (r == c).astype(F32)
    diag16 = (r // 16) == (c // 16)
    ms = [-jnp.where(diag16, n, 0.0) for n in n_mats]
    ps = [eye + m for m in ms]
    mps = [_split(m) for m in ms]
    for _ in range(3):
        mps = [_split(_dot3(mp, mp)) for mp in mps]
        ps = [p + _dot3(_split(p), mp) for p, mp in zip(ps, mps)]
    for w in (32, 64):
        off = ((r // w) == (c // w)) & ((r // (w // 2)) != (c // (w // 2)))
        pbs = [p.astype(BF16) for p in ps]
        tmp = [_dot(jnp.where(off, n, 0.0).astype(BF16), pb).astype(BF16) for n, pb in zip(n_mats, pbs)]
        ps = [p - _dot(pb, a) for p, pb, a in zip(ps, pbs, tmp)]
    return ps


def _deltanet_kernel(q_ref, k_ref, v_ref, wq_ref, wk_ref, wv_ref, a_ref, at_ref, y_ref,
                     xs_scr, qd_scr, kd_scr, vd_scr, aq_scr, b_scr, ge_scr, y1_scr, *,
                     t, ctx_len):
    h = pl.program_id(1)
    n_chunks = t // DC
    n_ctx_chunks = ctx_len // DC
    pad = 8

    def prep_stream(src_ref, w_ref, dst_scr, l2, scale):
        zeros = jnp.zeros((pad, LANES), F32)
        xs_scr[0:pad, :] = zeros
        xs_scr[t + pad:t + 2 * pad, :] = zeros

        def load(ci, carry):
            rows = pl.ds(pl.multiple_of(ci * CH, CH), CH)
            xs_scr[pl.ds(pl.multiple_of(ci * CH + pad, 8), CH), :] = src_ref[rows, :].astype(F32)
            return carry

        lax.fori_loop(0, t // CH, load, 0)
        w = w_ref[...]

        def conv(ci, carry):
            win = xs_scr[pl.ds(pl.multiple_of(ci * CH, CH), CH + 2 * pad), :]
            tok = ci * CH - pad + lax.broadcasted_iota(jnp.int32, (CH + 2 * pad, 1), 0)
            win = jnp.where((tok < ctx_len) == (ci * CH < ctx_len), win, 0.0)
            acc = win[pad - 2:pad - 2 + CH, :] * w[0:1, :]
            for j in range(1, CONV_K):
                acc = acc + win[pad - 2 + j:pad - 2 + j + CH, :] * w[j:j + 1, :]
            acc = acc * _sigmoid(acc)
            if l2:
                acc = acc * lax.rsqrt(jnp.sum(acc * acc, axis=-1, keepdims=True) + EPS) * scale
            dst_scr[pl.ds(pl.multiple_of(ci * CH, CH), CH), :] = acc
            return carry

        lax.fori_loop(0, t // CH, conv, 0)

    prep_stream(q_ref, wq_ref, qd_scr, True, SCALE)
    prep_stream(k_ref, wk_ref, kd_scr, True, 1.0)
    prep_stream(v_ref, wv_ref, vd_scr, False, 1.0)

    r = lax.broadcasted_iota(jnp.int32, (DC, DC), 0)
    c = lax.broadcasted_iota(jnp.int32, (DC, DC), 1)

    def pre(i, carry):
        chains = []
        n_mats = []
        for uu in range(PRE_UNROLL):
            ci = i * PRE_UNROLL + uu
            rows = pl.ds(pl.multiple_of(ci * DC, DC), DC)
            q = qd_scr[rows, :]
            k = kd_scr[rows, :]
            v = vd_scr[rows, :]
            a_blk = a_ref[rows, :]
            at_blk = at_ref[ci]
            kb = k.astype(BF16)
            kk = _dot_nt(kb, kb)
            qk = _dot_nt(q.astype(BF16), kb)
            for d in range(2):
                incl = (c <= r) if d == 0 else (c >= r)
                strict = (c < r) if d == 0 else (c > r)
                last = DC - 1 if d == 0 else 0
                g_col = _lane_col(a_blk, 16 + d * 4 + h)
                beta = _lane_col(a_blk, 24 + d * 4 + h)
                g_row = _sub_row(at_blk, 16 + d * 4 + h)
                decay = jnp.exp(jnp.where(incl, g_col - g_row, NEG))
                eg = jnp.exp(g_col)
                g_end = g_col[last:last + 1, :]
                n_mats.append(jnp.where(strict, beta * decay * kk, 0.0))
                rhs = jnp.concatenate([beta * v, (beta * eg) * k], axis=1).astype(BF16)
                ke = (jnp.exp(g_end - g_col) * k).astype(BF16)
                ge_scr[d, ci] = jnp.broadcast_to(jnp.exp(g_end), (8, LANES))
                chains.append((d, ci, rows, rhs, ke, (qk * decay).astype(BF16), eg * q))
        t_invs = _tri_inverse(n_mats)
        ws = [_dot(t_inv.astype(BF16), ch[3]).astype(BF16) for t_inv, ch in zip(t_invs, chains)]
        kws = [_dot_tn(ch[4], w) for w, ch in zip(ws, chains)]
        qws = [_dot(ch[5], w) for w, ch in zip(ws, chains)]
        for kw, qw, (d, ci, rows, _, _, _, egq) in zip(kws, qws, chains):
            aq_scr[d, ci, 0:HEAD_DIM, :] = (-kw[:, LANES:]).astype(BF16)
            aq_scr[d, ci, HEAD_DIM:HEAD_DIM + DC, :] = (egq - qw[:, LANES:]).astype(BF16)
            b_scr[d, ci] = kw[:, :LANES]
            if d == 0:
                y_ref[rows, :] = qw[:, :LANES]
            else:
                y1_scr[rows, :] = qw[:, :LANES]
        return carry

    lax.fori_loop(0, n_chunks // PRE_UNROLL, pre, 0)

    def scan(j, carry):
        cis = [_chunk_order(j, d, n_chunks, n_ctx_chunks) for d in range(2)]
        xs = [_dot(aq_scr[d, cis[d]], carry[d].astype(BF16)) for d in range(2)]
        new = tuple(ge_scr[d, cis[d]][0:1, :] * carry[d] + (xs[d][:HEAD_DIM, :] + b_scr[d, cis[d]])
                    for d in range(2))
        rows = [pl.ds(pl.multiple_of(cis[d] * DC, DC), DC) for d in range(2)]
        y_ref[rows[0], :] = y_ref[rows[0], :] + xs[0][HEAD_DIM:, :]
        y1_scr[rows[1], :] = y1_scr[rows[1], :] + xs[1][HEAD_DIM:, :]
        return new

    zero = jnp.zeros((HEAD_DIM, HEAD_DIM), F32)
    lax.fori_loop(0, n_chunks, scan, (zero, zero))

    def add(ci, carry):
        rows = pl.ds(pl.multiple_of(ci * CH, CH), CH)
        y_ref[rows, :] = y_ref[rows, :] + y1_scr[rows, :]
        return carry

    lax.fori_loop(0, t // CH, add, 0)


def _deltanet(u, conv_w, a, at_dc, ctx_len):
    bsz, t, _ = u.shape
    n_chunks = t // DC
    kern = functools.partial(_deltanet_kernel, t=t, ctx_len=ctx_len)

    def wspec(off):
        return pl.BlockSpec((CONV_K, LANES), lambda b, h: (0, off + h))

    return pl.pallas_call(
        kern,
        grid=(bsz, N_HEADS),
        in_specs=[_head_spec(t, "d_q"), _head_spec(t, "d_k"), _head_spec(t, "d_v"),
                  wspec(0), wspec(N_HEADS), wspec(2 * N_HEADS),
                  pl.BlockSpec((None, t, LANES), lambda b, h: (b, 0, 0)),
                  pl.BlockSpec((None, n_chunks, 32, DC), lambda b, h: (b, 0, 0, 0))],
        out_specs=pl.BlockSpec((None, t, LANES), lambda b, h: (b, 0, h)),
        out_shape=jax.ShapeDtypeStruct((bsz, t, GROUP_W), F32),
        scratch_shapes=[pltpu.VMEM((t + 16, LANES), F32),
                        pltpu.VMEM((t, LANES), F32), pltpu.VMEM((t, LANES), F32), pltpu.VMEM((t, LANES), F32),
                        pltpu.VMEM((2, n_chunks, HEAD_DIM + DC, LANES), BF16),
                        pltpu.VMEM((2, n_chunks, HEAD_DIM, LANES), F32),
                        pltpu.VMEM((2, n_chunks, 8, LANES), F32),
                        pltpu.VMEM((t, LANES), F32)],
        compiler_params=_params(("parallel", "parallel")),
        name="deltanet",
    )(u, u, u, conv_w, conv_w, conv_w, a, at_dc)


def _out_kernel(x_ref, ym_ref, yr_ref, ya_ref, yd_ref, zm_ref, zr_ref, za_ref, zd_ref, hn_ref, w_ref,
                mod_ref, pw_ref, o_ref, *, first_block):
    b = pl.program_id(0)
    i = pl.program_id(1) + first_block

    def silu(z_ref):
        z = z_ref[...].astype(F32)
        return z * _sigmoid(z)

    def head_rms(y_ref, g):
        y = y_ref[...]
        parts = []
        for hh in range(N_HEADS):
            p = y[:, hh * LANES:(hh + 1) * LANES]
            parts.append(p * lax.rsqrt(jnp.mean(p * p, axis=-1, keepdims=True) + EPS))
        return jnp.concatenate(parts, axis=1) * hn_ref[:, g * GROUP_W:(g + 1) * GROUP_W]

    merged = jnp.concatenate([
        (head_rms(ym_ref, 0) * silu(zm_ref)).astype(BF16),
        (head_rms(yr_ref, 1) * silu(zr_ref)).astype(BF16),
        (ya_ref[...] * silu(za_ref)).astype(BF16),
        (head_rms(yd_ref, 2) * silu(zd_ref)).astype(BF16)], axis=1)
    o = _dot(merged, w_ref[...])
    o = o * lax.rsqrt(jnp.mean(o * o, axis=-1, keepdims=True) + EPS) * pw_ref[...]
    row = jnp.where(i == 0, 4, b)
    gate = mod_ref[pl.ds(row, 1), 2 * D_MODEL:3 * D_MODEL]
    o_ref[...] = x_ref[...] + gate * o


def _out_projection(xa, ys, u, hn_w, w_out, mod, post_w, ctx_len, latent_only):
    bsz, t, d = xa.shape
    assert ctx_len == CH
    first = 1 if latent_only else 0
    tok = lambda b, i: (b, i + first, 0)
    yspec = pl.BlockSpec((None, CH, GROUP_W), tok)

    def zspec(name):
        blk = CB[name] // N_HEADS
        return pl.BlockSpec((None, CH, GROUP_W), lambda b, i: (b, i + first, blk))

    return pl.pallas_call(
        functools.partial(_out_kernel, first_block=first),
        grid=(bsz, t // CH - first),
        in_specs=[pl.BlockSpec((None, CH, d), tok), yspec, yspec, yspec, yspec,
                  zspec("m_z"), zspec("r_z"), zspec("a_z"), zspec("d_z"),
                  pl.BlockSpec((1, 3 * GROUP_W), lambda b, i: (0, 0)),
                  pl.BlockSpec((d, d), lambda b, i: (0, 0)),
                  pl.BlockSpec((8, 3 * d), lambda b, i: (0, 0)),
                  pl.BlockSpec((1, d), lambda b, i: (0, 0))],
        out_specs=pl.BlockSpec((None, CH, d), lambda b, i: (b, i, 0)),
        out_shape=jax.ShapeDtypeStruct((bsz, t - first * CH, d), F32),
        compiler_params=_params(("parallel", "parallel")),
        name="out_proj",
    )(xa, *ys, u, u, u, u, hn_w.reshape(1, 3 * GROUP_W), w_out, mod, post_w.reshape(1, d))


def _reorder_w_in(w):
    g = GROUP_W
    m_gate = 5 * g
    r_start = m_gate + 4 * N_HEADS
    d_gate = r_start + 4 * g + (2 * g + 2 * KV_HEADS * HEAD_DIM) + 4 * g
    w_main = jnp.concatenate([w[:, :m_gate], w[:, r_start:d_gate]], axis=1).astype(BF16)
    gates = jnp.concatenate([w[:, m_gate:r_start], w[:, d_gate:]], axis=1)
    w_gate = jnp.pad(gates, ((0, 0), (0, LANES - gates.shape[1]))).astype(BF16)
    return w_main, w_gate


def _rope_tables(seq, ctx_len):
    rows = seq // GRID_W
    row = jnp.repeat(jnp.arange(rows), GRID_W)
    col = jnp.tile(jnp.arange(GRID_W), rows)
    n_freq = HEAD_DIM // 4
    inv_freq = ROPE_BASE ** (-jnp.arange(n_freq, dtype=F32) / n_freq)
    ar = row[:, None] * inv_freq
    ac = col[:, None] * inv_freq
    cos = jnp.concatenate([jnp.cos(ar), jnp.cos(ar), jnp.cos(ac), jnp.cos(ac)], axis=1)
    sin = jnp.concatenate([-jnp.sin(ar), jnp.sin(ar), -jnp.sin(ac), jnp.sin(ac)], axis=1)
    cos = jnp.concatenate([jnp.ones((ctx_len, HEAD_DIM), F32), cos], axis=0)
    sin = jnp.concatenate([jnp.zeros((ctx_len, HEAD_DIM), F32), sin], axis=0)
    return cos, sin


def kernel(x, c, ctx, c_ctx, ada_w, ada_b, pre_norm_w, post_norm_w, w_in, w_out, mlstm_i_bias, mlstm_f_bias,
           ret_log_gamma, attn_q_norm_w, attn_k_norm_w, dn_conv_w, dn_a_log, dn_dt_bias, head_norm_w):
    bsz, seq, d = x.shape
    ctx_len = ctx.shape[1]
    t = ctx_len + seq
    depth = ada_w.shape[0]
    assert bsz <= 4 and ctx_len % CH == 0 and seq % CH == 0

    xa = jnp.concatenate([ctx, x], axis=1)
    c8 = jnp.zeros((8, d), F32).at[:bsz].set(c).at[4].set(c_ctx)
    mod = _modulation(c8, ada_w, ada_b)
    cos_t, sin_t = _rope_tables(seq, ctx_len)

    for l in range(depth):
        w_main, w_gate = _reorder_w_in(w_in[l])
        u, g = _in_projection(xa, mod[l], pre_norm_w[l], w_main, w_gate, ctx_len)
        zeros8 = jnp.zeros((8,), F32)
        bias_row = jnp.pad(jnp.concatenate([mlstm_i_bias[l].reshape(-1), mlstm_f_bias[l].reshape(-1),
                                            dn_dt_bias[l].reshape(-1), zeros8]), (0, LANES - 32))
        alog_row = jnp.pad(jnp.concatenate([zeros8, zeros8, dn_a_log[l].reshape(-1), zeros8]), (0, LANES - 32))
        a = _gate_prep(g, bias_row.reshape(1, LANES), alog_row.reshape(1, LANES))
        at = jnp.swapaxes(a[:, :, :32], 1, 2)
        at_ch = at.reshape(bsz, 32, t // CH, CH).transpose(0, 2, 1, 3)
        at_dc = at.reshape(bsz, 32, t // DC, DC).transpose(0, 2, 1, 3)
        y_m = _mlstm(u, a, at_ch, ctx_len)
        y_r = _retention(u, ret_log_gamma[l], cos_t, sin_t, ctx_len)
        y_a = _attention(u, cos_t, sin_t, attn_q_norm_w[l], attn_k_norm_w[l], ctx_len, with_ctx=(l < depth - 1))
        y_d = _deltanet(u, dn_conv_w[l], a, at_dc, ctx_len)
        xa = _out_projection(xa, (y_m, y_r, y_a, y_d), u, head_norm_w[l], w_out[l].astype(BF16), mod[l],
                             post_norm_w[l], ctx_len, latent_only=(l == depth - 1))
    return xa
```

```python
import functools

import jax
import jax.numpy as jnp
from jax import lax
from jax.experimental import pallas as pl
from jax.experimental.pallas import tpu as pltpu

F32 = jnp.float32
BF16 = jnp.bfloat16
HI = lax.Precision.HIGHEST

D_MODEL = 2048
GROUP_W = 512
N_HEADS = 4
HEAD_DIM = 128
KV_HEADS = 2
GRID_W = 64
CONV_K = 5
ROPE_BASE = 10000.0
EPS = 1e-6
SCALE = HEAD_DIM ** -0.5
LOG2E = 1.4426950408889634
NEG = -1e30

LANES = 128
CH = 256
DC = 64
PACK = 4
PRE_GROUPS = 4
N_MAIN = 64 * LANES
VMEM_LIMIT = 56 * 1024 * 1024

CB = dict(m_q=0, m_k=4, m_v=8, m_o=12, m_z=16, r_q=20, r_k=24, r_v=28, r_z=32,
          a_q=36, a_k=40, a_v=42, a_z=44, d_q=48, d_k=52, d_v=56, d_z=60)


def _dot(a, b, prec=None):
    return jnp.dot(a, b, preferred_element_type=F32, precision=prec)


def _dot_nt(a, b, prec=None):
    return lax.dot_general(a, b, (((1,), (1,)), ((), ())), preferred_element_type=F32, precision=prec)


def _dot_tn(a, b, prec=None):
    return lax.dot_general(a, b, (((0,), (0,)), ((), ())), preferred_element_type=F32, precision=prec)


def _split(a):
    hi = a.astype(BF16)
    return hi, (a - hi.astype(F32)).astype(BF16)


def _dot3(a, b):
    n = a[0].shape[0]
    x = _dot(jnp.concatenate([a[0], a[1]], axis=0), b[0])
    return (x[:n] + x[n:]) + _dot(a[0], b[1])


def _sigmoid(x):
    return 1.0 / (1.0 + jnp.exp(-x))


def _lane_col(blk, idx):
    lane = lax.broadcasted_iota(jnp.int32, blk.shape, 1)
    return jnp.sum(jnp.where(lane == idx, blk, 0.0), axis=1, keepdims=True)


def _sub_row(blk, idx):
    sub = lax.broadcasted_iota(jnp.int32, blk.shape, 0)
    return jnp.sum(jnp.where(sub == idx, blk, 0.0), axis=0, keepdims=True)


def _rope(t, cos, sin_signed):
    lane = lax.broadcasted_iota(jnp.int32, t.shape, 1)
    partner = jnp.where((lane // 32) % 2 == 0, pltpu.roll(t, 96, 1), pltpu.roll(t, 32, 1))
    return t * cos + partner * sin_signed


def _params(sem):
    return pltpu.CompilerParams(dimension_semantics=sem, vmem_limit_bytes=VMEM_LIMIT)


def _mod_kernel(c_ref, w_ref, b_ref, o_ref):
    c = c_ref[...]
    o_ref[...] = _dot(c * _sigmoid(c), w_ref[...], HI) + b_ref[...]


def _modulation(c8, ada_w, ada_b):
    depth, d, n3 = ada_w.shape
    tn = 768
    return pl.pallas_call(
        _mod_kernel,
        grid=(depth, n3 // tn),
        in_specs=[pl.BlockSpec((8, d), lambda l, n: (0, 0)),
                  pl.BlockSpec((None, d, tn), lambda l, n: (l, 0, n)),
                  pl.BlockSpec((None, 1, tn), lambda l, n: (l, 0, n))],
        out_specs=pl.BlockSpec((None, 8, tn), lambda l, n: (l, 0, n)),
        out_shape=jax.ShapeDtypeStruct((depth, 8, n3), F32),
        compiler_params=_params(("parallel", "parallel")),
        name="adaln_mod",
    )(c8, ada_w, ada_b.reshape(depth, 1, n3))


def _inproj_kernel(x_ref, mod_ref, pw_ref, wm_ref, wg_ref, u_ref, g_ref, h_scr, *, tm, ctx_len):
    b = pl.program_id(0)
    i = pl.program_id(1)
    n = pl.program_id(2)

    @pl.when(n == 0)
    def _():
        x = x_ref[...]
        y = x * lax.rsqrt(jnp.mean(x * x, axis=-1, keepdims=True) + EPS) * pw_ref[...]
        row = i * tm + lax.broadcasted_iota(jnp.int32, (tm, 1), 0)
        is_ctx = row < ctx_len
        d = D_MODEL
        sh = jnp.where(is_ctx, mod_ref[4:5, 0:d], mod_ref[pl.ds(b, 1), 0:d])
        sc = jnp.where(is_ctx, mod_ref[4:5, d:2 * d], mod_ref[pl.ds(b, 1), d:2 * d])
        hh = (y * (1.0 + sc) + sh).astype(BF16)
        h_scr[...] = hh
        g_ref[...] = _dot(hh, wg_ref[...])

    u_ref[...] = _dot(h_scr[...], wm_ref[...]).astype(BF16)


def _in_projection(xa, mod, pre_w, w_main, w_gate, ctx_len):
    bsz, t, d = xa.shape
    tm = t // 4 if (t % 4 == 0 and (t // 4) % 16 == 0) else CH
    tn = 1024
    kern = functools.partial(_inproj_kernel, tm=tm, ctx_len=ctx_len)
    return pl.pallas_call(
        kern,
        grid=(bsz, t // tm, N_MAIN // tn),
        in_specs=[pl.BlockSpec((None, tm, d), lambda b, i, n: (b, i, 0)),
                  pl.BlockSpec((8, 3 * d), lambda b, i, n: (0, 0)),
                  pl.BlockSpec((1, d), lambda b, i, n: (0, 0)),
                  pl.BlockSpec((d, tn), lambda b, i, n: (0, n)),
                  pl.BlockSpec((d, LANES), lambda b, i, n: (0, 0))],
        out_specs=[pl.BlockSpec((None, tm, tn), lambda b, i, n: (b, i, n)),
                   pl.BlockSpec((None, tm, LANES), lambda b, i, n: (b, i, 0))],
        out_shape=[jax.ShapeDtypeStruct((bsz, t, N_MAIN), BF16),
                   jax.ShapeDtypeStruct((bsz, t, LANES), F32)],
        scratch_shapes=[pltpu.VMEM((tm, d), BF16)],
        compiler_params=_params(("parallel", "parallel", "arbitrary")),
        name="in_proj",
    )(xa, mod, pre_w.reshape(1, d), w_main, w_gate)


def _gate_kernel(g_ref, bias_ref, alog_ref, a_ref):
    x = g_ref[...] + bias_ref[...]
    lane = lax.broadcasted_iota(jnp.int32, x.shape, 1)
    l1p = jnp.log(1.0 + jnp.exp(-jnp.abs(x)))
    log_f = jnp.minimum(x, 0.0) - l1p
    log_a = -jnp.exp(alog_ref[...]) * (jnp.maximum(x, 0.0) + l1p)
    v = jnp.where(lane < 8, x, jnp.where(lane < 16, log_f, jnp.where(lane < 24, log_a, _sigmoid(x))))
    r = lax.broadcasted_iota(jnp.int32, (CH, CH), 0)
    c = lax.broadcasted_iota(jnp.int32, (CH, CH), 1)
    same = (r // DC) == (c // DC)
    pre = (c <= r).astype(F32)
    suf = (c >= r).astype(F32)
    pre_dc = jnp.where(same, pre, 0.0)
    suf_dc = jnp.where(same, suf, 0.0)
    fwd = (lane % 8) < 4
    cum_ch = jnp.where(fwd, _dot(pre, v, HI), _dot(suf, v, HI))
    cum_dc = jnp.where(fwd, _dot(pre_dc, v, HI), _dot(suf_dc, v, HI))
    a_ref[...] = jnp.where((lane >= 8) & (lane < 16), cum_ch, jnp.where((lane >= 16) & (lane < 24), cum_dc, v))


def _gate_prep(g, bias_row, alog_row):
    bsz, t, _ = g.shape
    return pl.pallas_call(
        _gate_kernel,
        grid=(bsz, t // CH),
        in_specs=[pl.BlockSpec((None, CH, LANES), lambda b, i: (b, i, 0)),
                  pl.BlockSpec((1, LANES), lambda b, i: (0, 0)),
                  pl.BlockSpec((1, LANES), lambda b, i: (0, 0))],
        out_specs=pl.BlockSpec((None, CH, LANES), lambda b, i: (b, i, 0)),
        out_shape=jax.ShapeDtypeStruct((bsz, t, LANES), F32),
        compiler_params=_params(("parallel", "parallel")),
        name="gate_prep",
    )(g, bias_row, alog_row)


def _chunk_order(j, d, n_chunks, n_ctx_chunks):
    if d == 0:
        return j
    return jnp.where(j < n_ctx_chunks, n_ctx_chunks - 1 - j, n_chunks + n_ctx_chunks - 1 - j)


def _head_spec(t, name):
    base = CB[name]
    return pl.BlockSpec((None, t, LANES), lambda b, h: (b, 0, base + h))


def _mlstm_kernel(q_ref, k_ref, v_ref, o_ref, a_ref, at_ref, y_ref, c_scr, y1_scr, *, n_chunks, n_ctx_chunks):
    h = pl.program_id(1)
    r = lax.broadcasted_iota(jnp.int32, (CH, CH), 0)
    c = lax.broadcasted_iota(jnp.int32, (CH, CH), 1)
    one_col = (lax.broadcasted_iota(jnp.int32, (CH, LANES), 1) == 0).astype(F32)

    masks = ((c <= r), (c >= r))
    lasts = (CH - 1, 0)
    dirs = range(2)
    c_scr[...] = jnp.zeros_like(c_scr)

    def step(j, ms):
        cis = [_chunk_order(j, d, n_chunks, n_ctx_chunks) for d in dirs]
        rows = [pl.ds(pl.multiple_of(ci * CH, CH), CH) for ci in cis]
        qs = [q_ref[rw, :] for rw in rows]
        ks = [k_ref[rw, :] for rw in rows]
        qk = [_dot_nt(qs[d], ks[d]) for d in dirs]
        pc = [_dot(qs[d], c_scr[d].astype(BF16)) for d in dirs]
        a_blk = [a_ref[rw, :] for rw in rows]
        at_blk = [at_ref[ci] for ci in cis]
        li_col = [_lane_col(a_blk[d], d * 4 + h) for d in dirs]
        b_col = [_lane_col(a_blk[d], 8 + d * 4 + h) for d in dirs]
        li_row = [_sub_row(at_blk[d], d * 4 + h) for d in dirs]
        b_row = [_sub_row(at_blk[d], 8 + d * 4 + h) for d in dirs]
        dm = [jnp.where(masks[d], b_col[d] - b_row[d] + li_row[d], NEG) for d in dirs]
        inter = [ms[d] + b_col[d] for d in dirs]
        m_t = [jnp.maximum(inter[d], jnp.max(dm[d], axis=1, keepdims=True)) for d in dirs]
        s = [(qk[d] * (jnp.exp(dm[d] - m_t[d]) * SCALE)).astype(BF16) for d in dirs]
        v_aug = [jnp.concatenate([v_ref[rw, :].astype(F32), one_col], axis=1) for rw in rows]
        sv = [_dot(s[d], v_aug[d].astype(BF16)) for d in dirs]

        b_end = [b_col[d][lasts[d]:lasts[d] + 1, :] for d in dirs]
        dec = [b_end[d] - b_col[d] + li_col[d] for d in dirs]
        m_new = [jnp.maximum(ms[d] + b_end[d], jnp.max(dec[d], axis=0, keepdims=True)) for d in dirs]
        wv = [(jnp.exp(dec[d] - m_new[d]) * v_aug[d]).astype(BF16) for d in dirs]
        upd = [_dot_tn(ks[d], wv[d]) for d in dirs]
        for d in dirs:
            c_scr[d] = jnp.exp(ms[d] + b_end[d] - m_new[d]) * c_scr[d] + upd[d]

        for d in dirs:
            nd = (jnp.exp(inter[d] - m_t[d]) * SCALE) * pc[d] + sv[d]
            h_out = nd[:, :LANES] / jnp.maximum(jnp.abs(nd[:, LANES:LANES + 1]), jnp.exp(-m_t[d]))
            if d == 0:
                y_ref[rows[d], :] = h_out
            else:
                y1_scr[rows[d], :] = h_out
        return tuple(m_new)

    zero = jnp.zeros((1, 1), F32)
    lax.fori_loop(0, n_chunks, step, (zero, zero))

    def finish(ci, carry):
        rw = pl.ds(pl.multiple_of(ci * CH, CH), CH)
        y_ref[rw, :] = (y_ref[rw, :] + y1_scr[rw, :]) * _sigmoid(o_ref[rw, :].astype(F32))
        return carry

    lax.fori_loop(0, n_chunks, finish, 0)


def _mlstm(u, a, at_ch, ctx_len):
    bsz, t, _ = u.shape
    n_chunks = t // CH
    kern = functools.partial(_mlstm_kernel, n_chunks=n_chunks, n_ctx_chunks=ctx_len // CH)
    return pl.pallas_call(
        kern,
        grid=(bsz, N_HEADS),
        in_specs=[_head_spec(t, "m_q"), _head_spec(t, "m_k"), _head_spec(t, "m_v"), _head_spec(t, "m_o"),
                  pl.BlockSpec((None, t, LANES), lambda b, h: (b, 0, 0)),
                  pl.BlockSpec((None, n_chunks, 32, CH), lambda b, h: (b, 0, 0, 0))],
        out_specs=pl.BlockSpec((None, t, LANES), lambda b, h: (b, 0, h)),
        out_shape=jax.ShapeDtypeStruct((bsz, t, GROUP_W), F32),
        scratch_shapes=[pltpu.VMEM((2, HEAD_DIM, 2 * LANES), F32), pltpu.VMEM((t, LANES), F32)],
        compiler_params=_params(("parallel", "parallel")),
        name="mlstm",
    )(u, u, u, u, a, at_ch)


def _retention_kernel(lg_ref, q_ref, k_ref, v_ref, cos_ref, sin_ref, y_ref, qp_scr, kp_scr, r_scr, y1_scr, *,
                      n_chunks, n_ctx_chunks):
    h = pl.program_id(1)

    def prep(ci, carry):
        rows = pl.ds(pl.multiple_of(ci * CH, CH), CH)
        cos = cos_ref[rows, :]
        sin = sin_ref[rows, :]
        qp_scr[rows, :] = _rope(q_ref[rows, :].astype(F32), cos, sin).astype(BF16)
        kp_scr[rows, :] = (_rope(k_ref[rows, :].astype(F32), cos, sin) * SCALE).astype(BF16)
        return carry

    lax.fori_loop(0, n_chunks, prep, 0)

    r = lax.broadcasted_iota(jnp.int32, (CH, CH), 0).astype(F32)
    c = lax.broadcasted_iota(jnp.int32, (CH, CH), 1).astype(F32)
    pos = lax.broadcasted_iota(jnp.int32, (CH, 1), 0).astype(F32)

    dirs = range(2)
    decay, q_decay, k_decay, chunk_decay = [], [], [], []
    for d in dirs:
        lg = lg_ref[d * N_HEADS + h]
        rel = (r - c) if d == 0 else (c - r)
        decay.append(jnp.where(rel >= 0, jnp.exp(lg * jnp.maximum(rel, 0.0)), 0.0))
        p_vis = pos if d == 0 else (CH - 1.0) - pos
        q_decay.append(jnp.exp(lg * (p_vis + 1.0)))
        k_decay.append(jnp.exp(lg * ((CH - 1.0) - p_vis)))
        chunk_decay.append(jnp.exp(lg * CH))
    r_scr[...] = jnp.zeros_like(r_scr)

    def step(j, carry):
        cis = [_chunk_order(j, d, n_chunks, n_ctx_chunks) for d in dirs]
        rows = [pl.ds(pl.multiple_of(ci * CH, CH), CH) for ci in cis]
        qs = [qp_scr[rw, :] for rw in rows]
        ks = [kp_scr[rw, :] for rw in rows]
        vs = [v_ref[rw, :] for rw in rows]
        qk = [_dot_nt(qs[d], ks[d]) for d in dirs]
        qr = [_dot(qs[d], r_scr[d].astype(BF16)) for d in dirs]
        sv = [_dot((qk[d] * decay[d]).astype(BF16), vs[d]) for d in dirs]
        upd = [_dot_tn((ks[d].astype(F32) * k_decay[d]).astype(BF16), vs[d]) for d in dirs]
        for d in dirs:
            r_scr[d] = chunk_decay[d] * r_scr[d] + upd[d]
        y_ref[rows[0], :] = sv[0] + q_decay[0] * qr[0]
        y1_scr[rows[1], :] = sv[1] + q_decay[1] * qr[1]
        return carry

    lax.fori_loop(0, n_chunks, step, 0)

    def finish(ci, carry):
        rw = pl.ds(pl.multiple_of(ci * CH, CH), CH)
        y_ref[rw, :] = y_ref[rw, :] + y1_scr[rw, :]
        return carry

    lax.fori_loop(0, n_chunks, finish, 0)


def _retention(u, log_gamma, cos_t, sin_t, ctx_len):
    bsz, t, _ = u.shape
    n_chunks = t // CH
    kern = functools.partial(_retention_kernel, n_chunks=n_chunks, n_ctx_chunks=ctx_len // CH)
    tab = pl.BlockSpec((t, LANES), lambda b, h: (0, 0))
    return pl.pallas_call(
        kern,
        grid=(bsz, N_HEADS),
        in_specs=[pl.BlockSpec(memory_space=pltpu.SMEM),
                  _head_spec(t, "r_q"), _head_spec(t, "r_k"), _head_spec(t, "r_v"), tab, tab],
        out_specs=pl.BlockSpec((None, t, LANES), lambda b, h: (b, 0, h)),
        out_shape=jax.ShapeDtypeStruct((bsz, t, GROUP_W), F32),
        scratch_shapes=[pltpu.VMEM((t, LANES), BF16), pltpu.VMEM((t, LANES), BF16),
                        pltpu.VMEM((2, HEAD_DIM, HEAD_DIM), F32), pltpu.VMEM((t, LANES), F32)],
        compiler_params=_params(("parallel", "parallel")),
        name="retention",
    )(log_gamma.reshape(2 * N_HEADS), u, u, u, cos_t, sin_t)


def _attn_kernel(q_ref, k_ref, v_ref, cos_ref, sin_ref, cosq_ref, sinq_ref, qw_ref, kw_ref, o_ref,
                 kt_scr, s_scr, m_scr, *, n_blocks, with_ctx):
    j = pl.program_id(2)

    def load_q():
        qs = []
        for g in range(2):
            q = q_ref[:, g * LANES:(g + 1) * LANES].astype(F32)
            q = q * lax.rsqrt(jnp.mean(q * q, axis=-1, keepdims=True) + EPS) * qw_ref[...]
            qs.append((_rope(q, cosq_ref[...], sinq_ref[...]) * (SCALE * LOG2E)).astype(BF16))
        return jnp.concatenate(qs, axis=0)

    def finish(o, l_part):
        o = o / jnp.sum(l_part, axis=-1, keepdims=True)
        o_ref[:, 0:LANES] = o[:CH]
        o_ref[:, LANES:2 * LANES] = o[CH:]

    @pl.when(j == 0)
    def _():
        def prep(ci, carry):
            rows = pl.ds(pl.multiple_of(ci * CH, CH), CH)
            k = k_ref[rows, :].astype(F32)
            k = k * lax.rsqrt(jnp.mean(k * k, axis=-1, keepdims=True) + EPS) * kw_ref[...]
            kt_scr[ci] = _rope(k, cos_ref[rows, :], sin_ref[rows, :]).T.astype(BF16)
            return carry

        lax.fori_loop(0, n_blocks, prep, 0)
        if with_ctx:
            s = _dot(load_q(), kt_scr[0])
            p = jnp.exp2(s - jnp.max(s, axis=-1, keepdims=True))
            finish(_dot(p.astype(BF16), v_ref[0:CH, :]), p[:, :LANES] + p[:, LANES:])
        else:
            o_ref[...] = jnp.zeros_like(o_ref)

    @pl.when(j >= 1)
    def _():
        qq = load_q()
        mx = None
        for kb in range(n_blocks):
            s = _dot(qq, kt_scr[kb])
            s_scr[:, kb * CH:(kb + 1) * CH] = s
            m2 = jnp.maximum(s[:, :LANES], s[:, LANES:])
            mx = m2 if mx is None else jnp.maximum(mx, m2)
        m_scr[...] = jnp.broadcast_to(jnp.max(mx, axis=-1, keepdims=True), (2 * CH, LANES))

    def value_pass(_, carry):
        m = m_scr[...]
        l_acc = o = None
        for kb in range(n_blocks):
            cols = slice(kb * CH, (kb + 1) * CH)
            p_lo = jnp.exp2(s_scr[:, kb * CH:kb * CH + LANES] - m)
            p_hi = jnp.exp2(s_scr[:, kb * CH + LANES:(kb + 1) * CH] - m)
            l_acc = (p_lo + p_hi) if l_acc is None else l_acc + (p_lo + p_hi)
            pv = _dot(jnp.concatenate([p_lo, p_hi], axis=1).astype(BF16), v_ref[cols, :])
            o = pv if o is None else o + pv
        finish(o, l_acc)
        return carry

    lax.fori_loop(0, jnp.minimum(j, 1), value_pass, 0)


def _attention(u, cos_t, sin_t, qn_w, kn_w, ctx_len, with_ctx):
    bsz, t, _ = u.shape
    assert ctx_len == CH
    n_blocks = t // CH
    kern = functools.partial(_attn_kernel, n_blocks=n_blocks, with_ctx=with_ctx)
    qb = CB["a_q"] // 2
    kb = CB["a_k"]
    vb = CB["a_v"]
    tab = pl.BlockSpec((t, LANES), lambda b, kv, j: (0, 0))
    tabq = pl.BlockSpec((CH, LANES), lambda b, kv, j: (j, 0))
    vec = pl.BlockSpec((1, LANES), lambda b, kv, j: (0, 0))
    return pl.pallas_call(
        kern,
        grid=(bsz, KV_HEADS, n_blocks),
        in_specs=[pl.BlockSpec((None, CH, 2 * LANES), lambda b, kv, j: (b, j, qb + kv)),
                  pl.BlockSpec((None, t, LANES), lambda b, kv, j: (b, 0, kb + kv)),
                  pl.BlockSpec((None, t, LANES), lambda b, kv, j: (b, 0, vb + kv)),
                  tab, tab, tabq, tabq, vec, vec],
        out_specs=pl.BlockSpec((None, CH, 2 * LANES), lambda b, kv, j: (b, j, kv)),
        out_shape=jax.ShapeDtypeStruct((bsz, t, GROUP_W), F32),
        scratch_shapes=[pltpu.VMEM((n_blocks, HEAD_DIM, CH), BF16), pltpu.VMEM((2 * CH, t), F32),
                        pltpu.VMEM((2 * CH, LANES), F32)],
        compiler_params=_params(("parallel", "parallel", "arbitrary")),
        name="attention",
    )(u, u, u, cos_t, sin_t, cos_t, sin_t, qn_w.reshape(1, LANES), kn_w.reshape(1, LANES))


def _block_diag(y, lane_masks):
    return jnp.concatenate([y * mk for mk in lane_masks], axis=0)


def _packed_dot3(a, b, lane_masks):
    n = a[0].shape[0]
    x = _dot(jnp.concatenate([a[0], a[1]], axis=0), _block_diag(b[0], lane_masks))
    return (x[:n] + x[n:]) + _dot(a[0], _block_diag(b[1], lane_masks))


def _tri_inverse(n_mats, r, cl, lane_masks):
    eye = (r == cl).astype(F32)
    diag16 = (r // 16) == (cl // 16)
    ms = [-jnp.where(diag16, n, 0.0) for n in n_mats]
    ps = [eye + m for m in ms]
    mps = [_split(m) for m in ms]
    mps = [_split(_packed_dot3(mp, mp, lane_masks)) for mp in mps]
    for _ in range(2):
        both = [_packed_dot3(tuple(jnp.concatenate([a, b], axis=0) for a, b in zip(_split(p), mp)), mp, lane_masks)
                for p, mp in zip(ps, mps)]
        ps = [p + x[:DC] for p, x in zip(ps, both)]
        mps = [_split(x[DC:]) for x in both]
    ps = [p + _packed_dot3(_split(p), mp, lane_masks) for p, mp in zip(ps, mps)]
    for w in (32, 64):
        off = ((r // w) == (cl // w)) & ((r // (w // 2)) != (cl // (w // 2)))
        pbs = [p.astype(BF16) for p in ps]
        pds = [_block_diag(pb, lane_masks) for pb in pbs]
        tmp = [_dot(jnp.where(off, n, 0.0).astype(BF16), pd).astype(BF16) for n, pd in zip(n_mats, pds)]
        ps = [p - _dot(pb, _block_diag(a, lane_masks)) for p, pb, a in zip(ps, pbs, tmp)]
    return ps


def _deltanet_kernel(q_ref, k_ref, v_ref, wq_ref, wk_ref, wv_ref, a_ref, at_ref, y_ref,
                     xs_scr, qd_scr, kd_scr, vd_scr, aq_scr, b_scr, ge_scr, y1_scr, *,
                     t, ctx_len):
    h = pl.program_id(1)
    n_chunks = t // DC
    n_ctx_chunks = ctx_len // DC
    pad = 8

    def prep_stream(src_ref, w_ref, dst_scr, l2, scale):
        zeros = jnp.zeros((pad, LANES), F32)
        xs_scr[0:pad, :] = zeros
        xs_scr[t + pad:t + 2 * pad, :] = zeros

        def load(ci, carry):
            rows = pl.ds(pl.multiple_of(ci * CH, CH), CH)
            xs_scr[pl.ds(pl.multiple_of(ci * CH + pad, 8), CH), :] = src_ref[rows, :].astype(F32)
            return carry

        lax.fori_loop(0, t // CH, load, 0)
        w = w_ref[...]

        def conv(ci, carry):
            win = xs_scr[pl.ds(pl.multiple_of(ci * CH, CH), CH + 2 * pad), :]
            tok = ci * CH - pad + lax.broadcasted_iota(jnp.int32, (CH + 2 * pad, 1), 0)
            win = jnp.where((tok < ctx_len) == (ci * CH < ctx_len), win, 0.0)
            acc = win[pad - 2:pad - 2 + CH, :] * w[0:1, :]
            for j in range(1, CONV_K):
                acc = acc + win[pad - 2 + j:pad - 2 + j + CH, :] * w[j:j + 1, :]
            acc = acc * _sigmoid(acc)
            if l2:
                acc = acc * lax.rsqrt(jnp.sum(acc * acc, axis=-1, keepdims=True) + EPS) * scale
            dst_scr[pl.ds(pl.multiple_of(ci * CH, CH), CH), :] = acc
            return carry

        lax.fori_loop(0, t // CH, conv, 0)

    prep_stream(q_ref, wq_ref, qd_scr, True, SCALE)
    prep_stream(k_ref, wk_ref, kd_scr, True, 1.0)
    prep_stream(v_ref, wv_ref, vd_scr, False, 1.0)

    wp = PACK * DC
    r = lax.broadcasted_iota(jnp.int32, (DC, wp), 0)
    lane = lax.broadcasted_iota(jnp.int32, (DC, wp), 1)
    blk = lane // DC
    cl = lane % DC
    fwd = (blk % 2) == 0
    ahead = jnp.where(fwd, r - cl, cl - r)
    incl = ahead >= 0
    strict = ahead > 0
    lane_masks = [(blk == i).astype(BF16) for i in range(PACK)]

    def pick(vals):
        out = vals[PACK - 1]
        for i in range(PACK - 2, -1, -1):
            out = jnp.where(blk == i, vals[i], out)
        return out

    def pre(first_pair, n_groups):
        groups = []
        n_mats = []
        for gg in range(n_groups):
            pair = first_pair + gg
            at_blk = at_ref[pair]
            g_row = jnp.where(fwd[0:1, :], _sub_row(at_blk, 16 + h), _sub_row(at_blk, 20 + h))
            kks, qks, g_cols, betas, chains = [], [], [], [], []
            for cc in range(2):
                ci = pair * 2 + cc
                rows = pl.ds(pl.multiple_of(ci * DC, DC), DC)
                q = qd_scr[rows, :]
                k = kd_scr[rows, :]
                v = vd_scr[rows, :]
                a_blk = a_ref[rows, :]
                kb = k.astype(BF16)
                kb2 = jnp.concatenate([kb, kb], axis=0)
                kks.append(_dot_nt(kb, kb2))
                qks.append(_dot_nt(q.astype(BF16), kb2))
                for d in range(2):
                    last = DC - 1 if d == 0 else 0
                    g_col = _lane_col(a_blk, 16 + d * 4 + h)
                    beta = _lane_col(a_blk, 24 + d * 4 + h)
                    eg = jnp.exp(g_col)
                    g_end = g_col[last:last + 1, :]
                    rhs = jnp.concatenate([beta * v, (beta * eg) * k], axis=1).astype(BF16)
                    ke = (jnp.exp(g_end - g_col) * k).astype(BF16)
                    ge_scr[d, ci] = jnp.broadcast_to(jnp.exp(g_end), (8, LANES))
                    g_cols.append(g_col)
                    betas.append(beta)
                    chains.append((d, ci, rows, rhs, ke, eg * q))
            decay = jnp.exp(jnp.where(incl, pick(g_cols) - g_row, NEG))
            n_mats.append(jnp.where(strict, pick(betas) * decay * jnp.concatenate(kks, axis=1), 0.0))
            qkd = (jnp.concatenate(qks, axis=1) * decay).astype(BF16)
            groups.append((chains, qkd, jnp.concatenate([ch[3] for ch in chains], axis=0)))
        t_invs = [t_inv.astype(BF16) for t_inv in _tri_inverse(n_mats, r, cl, lane_masks)]
        ws = [[_dot(t_inv * mk, rhs_all).astype(BF16) for mk in lane_masks]
              for t_inv, (_, _, rhs_all) in zip(t_invs, groups)]
        w_all = [jnp.concatenate(w4, axis=0) for w4 in ws]
        kws = [[_dot_tn(ch[4], w) for ch, w in zip(chains, w4)] for (chains, _, _), w4 in zip(groups, ws)]
        qws = [[_dot(qkd * mk, wa) for mk in lane_masks] for (_, qkd, _), wa in zip(groups, w_all)]
        for (chains, _, _), kw4, qw4 in zip(groups, kws, qws):
            for (d, ci, rows, _, _, egq), kw, qw in zip(chains, kw4, qw4):
                aq_scr[d, ci, 0:HEAD_DIM, :] = (-kw[:, LANES:]).astype(BF16)
                aq_scr[d, ci, HEAD_DIM:HEAD_DIM + DC, :] = (egq - qw[:, LANES:]).astype(BF16)
                b_scr[d, ci] = kw[:, :LANES]
                if d == 0:
                    y_ref[rows, :] = qw[:, :LANES]
                else:
                    y1_scr[rows, :] = qw[:, :LANES]

    n_pairs = n_chunks // 2
    n_trips, tail = divmod(n_pairs, PRE_GROUPS)

    def pre_trip(i, carry):
        pre(i * PRE_GROUPS, PRE_GROUPS)
        return carry

    lax.fori_loop(0, n_trips, pre_trip, 0)
    if tail:
        pre(n_trips * PRE_GROUPS, tail)

    def scan(j, carry):
        cis = [_chunk_order(j, d, n_chunks, n_ctx_chunks) for d in range(2)]
        xs = [_dot(aq_scr[d, cis[d]], carry[d].astype(BF16)) for d in range(2)]
        new = tuple(ge_scr[d, cis[d]][0:1, :] * carry[d] + (xs[d][:HEAD_DIM, :] + b_scr[d, cis[d]])
                    for d in range(2))
        rows = [pl.ds(pl.multiple_of(cis[d] * DC, DC), DC) for d in range(2)]
        y_ref[rows[0], :] = y_ref[rows[0], :] + xs[0][HEAD_DIM:, :]
        y1_scr[rows[1], :] = y1_scr[rows[1], :] + xs[1][HEAD_DIM:, :]
        return new

    zero = jnp.zeros((HEAD_DIM, HEAD_DIM), F32)
    lax.fori_loop(0, n_chunks, scan, (zero, zero))

    def add(ci, carry):
        rows = pl.ds(pl.multiple_of(ci * CH, CH), CH)
        y_ref[rows, :] = y_ref[rows, :] + y1_scr[rows, :]
        return carry

    lax.fori_loop(0, t // CH, add, 0)


def _deltanet(u, conv_w, a, at_dc, ctx_len):
    bsz, t, _ = u.shape
    n_chunks = t // DC
    kern = functools.partial(_deltanet_kernel, t=t, ctx_len=ctx_len)

    def wspec(off):
        return pl.BlockSpec((CONV_K, LANES), lambda b, h: (0, off + h))

    return pl.pallas_call(
        kern,
        grid=(bsz, N_HEADS),
        in_specs=[_head_spec(t, "d_q"), _head_spec(t, "d_k"), _head_spec(t, "d_v"),
                  wspec(0), wspec(N_HEADS), wspec(2 * N_HEADS),
                  pl.BlockSpec((None, t, LANES), lambda b, h: (b, 0, 0)),
                  pl.BlockSpec((None, n_chunks // 2, 32, PACK * DC), lambda b, h: (b, 0, 0, 0))],
        out_specs=pl.BlockSpec((None, t, LANES), lambda b, h: (b, 0, h)),
        out_shape=jax.ShapeDtypeStruct((bsz, t, GROUP_W), F32),
        scratch_shapes=[pltpu.VMEM((t + 16, LANES), F32),
                        pltpu.VMEM((t, LANES), F32), pltpu.VMEM((t, LANES), F32), pltpu.VMEM((t, LANES), F32),
                        pltpu.VMEM((2, n_chunks, HEAD_DIM + DC, LANES), BF16),
                        pltpu.VMEM((2, n_chunks, HEAD_DIM, LANES), F32),
                        pltpu.VMEM((2, n_chunks, 8, LANES), F32),
                        pltpu.VMEM((t, LANES), F32)],
        compiler_params=_params(("parallel", "parallel")),
        name="deltanet",
    )(u, u, u, conv_w, conv_w, conv_w, a, at_dc)


def _out_kernel(x_ref, ym_ref, yr_ref, ya_ref, yd_ref, zm_ref, zr_ref, za_ref, zd_ref, hn_ref, w_ref,
                mod_ref, pw_ref, o_ref, *, first_block):
    b = pl.program_id(0)
    i = pl.program_id(1) + first_block

    def silu(z_ref):
        z = z_ref[...].astype(F32)
        return z * _sigmoid(z)

    def head_rms(y_ref, g):
        y = y_ref[...]
        parts = []
        for hh in range(N_HEADS):
            p = y[:, hh * LANES:(hh + 1) * LANES]
            parts.append(p * lax.rsqrt(jnp.mean(p * p, axis=-1, keepdims=True) + EPS))
        return jnp.concatenate(parts, axis=1) * hn_ref[:, g * GROUP_W:(g + 1) * GROUP_W]

    merged = jnp.concatenate([
        (head_rms(ym_ref, 0) * silu(zm_ref)).astype(BF16),
        (head_rms(yr_ref, 1) * silu(zr_ref)).astype(BF16),
        (ya_ref[...] * silu(za_ref)).astype(BF16),
        (head_rms(yd_ref, 2) * silu(zd_ref)).astype(BF16)], axis=1)
    o = _dot(merged, w_ref[...])
    o = o * lax.rsqrt(jnp.mean(o * o, axis=-1, keepdims=True) + EPS) * pw_ref[...]
    row = jnp.where(i == 0, 4, b)
    gate = mod_ref[pl.ds(row, 1), 2 * D_MODEL:3 * D_MODEL]
    o_ref[...] = x_ref[...] + gate * o


def _out_projection(xa, ys, u, hn_w, w_out, mod, post_w, ctx_len, latent_only):
    bsz, t, d = xa.shape
    assert ctx_len == CH
    first = 1 if latent_only else 0
    tok = lambda b, i: (b, i + first, 0)
    yspec = pl.BlockSpec((None, CH, GROUP_W), tok)

    def zspec(name):
        blk = CB[name] // N_HEADS
        return pl.BlockSpec((None, CH, GROUP_W), lambda b, i: (b, i + first, blk))

    return pl.pallas_call(
        functools.partial(_out_kernel, first_block=first),
        grid=(bsz, t // CH - first),
        in_specs=[pl.BlockSpec((None, CH, d), tok), yspec, yspec, yspec, yspec,
                  zspec("m_z"), zspec("r_z"), zspec("a_z"), zspec("d_z"),
                  pl.BlockSpec((1, 3 * GROUP_W), lambda b, i: (0, 0)),
                  pl.BlockSpec((d, d), lambda b, i: (0, 0)),
                  pl.BlockSpec((8, 3 * d), lambda b, i: (0, 0)),
                  pl.BlockSpec((1, d), lambda b, i: (0, 0))],
        out_specs=pl.BlockSpec((None, CH, d), lambda b, i: (b, i, 0)),
        out_shape=jax.ShapeDtypeStruct((bsz, t - first * CH, d), F32),
        compiler_params=_params(("parallel", "parallel")),
        name="out_proj",
    )(xa, *ys, u, u, u, u, hn_w.reshape(1, 3 * GROUP_W), w_out, mod, post_w.reshape(1, d))


def _reorder_w_in(w):
    g = GROUP_W
    m_gate = 5 * g
    r_start = m_gate + 4 * N_HEADS
    d_gate = r_start + 4 * g + (2 * g + 2 * KV_HEADS * HEAD_DIM) + 4 * g
    w_main = jnp.concatenate([w[:, :m_gate], w[:, r_start:d_gate]], axis=1).astype(BF16)
    gates = jnp.concatenate([w[:, m_gate:r_start], w[:, d_gate:]], axis=1)
    w_gate = jnp.pad(gates, ((0, 0), (0, LANES - gates.shape[1]))).astype(BF16)
    return w_main, w_gate


def _rope_tables(seq, ctx_len):
    rows = seq // GRID_W
    row = jnp.repeat(jnp.arange(rows), GRID_W)
    col = jnp.tile(jnp.arange(GRID_W), rows)
    n_freq = HEAD_DIM // 4
    inv_freq = ROPE_BASE ** (-jnp.arange(n_freq, dtype=F32) / n_freq)
    ar = row[:, None] * inv_freq
    ac = col[:, None] * inv_freq
    cos = jnp.concatenate([jnp.cos(ar), jnp.cos(ar), jnp.cos(ac), jnp.cos(ac)], axis=1)
    sin = jnp.concatenate([-jnp.sin(ar), jnp.sin(ar), -jnp.sin(ac), jnp.sin(ac)], axis=1)
    cos = jnp.concatenate([jnp.ones((ctx_len, HEAD_DIM), F32), cos], axis=0)
    sin = jnp.concatenate([jnp.zeros((ctx_len, HEAD_DIM), F32), sin], axis=0)
    return cos, sin


def kernel(x, c, ctx, c_ctx, ada_w, ada_b, pre_norm_w, post_norm_w, w_in, w_out, mlstm_i_bias, mlstm_f_bias,
           ret_log_gamma, attn_q_norm_w, attn_k_norm_w, dn_conv_w, dn_a_log, dn_dt_bias, head_norm_w):
    bsz, seq, d = x.shape
    ctx_len = ctx.shape[1]
    t = ctx_len + seq
    depth = ada_w.shape[0]
    assert bsz <= 4 and ctx_len % CH == 0 and seq % CH == 0

    xa = jnp.concatenate([ctx, x], axis=1)
    c8 = jnp.zeros((8, d), F32).at[:bsz].set(c).at[4].set(c_ctx)
    mod = _modulation(c8, ada_w, ada_b)
    cos_t, sin_t = _rope_tables(seq, ctx_len)

    for l in range(depth):
        w_main, w_gate = _reorder_w_in(w_in[l])
        u, g = _in_projection(xa, mod[l], pre_norm_w[l], w_main, w_gate, ctx_len)
        zeros8 = jnp.zeros((8,), F32)
        bias_row = jnp.pad(jnp.concatenate([mlstm_i_bias[l].reshape(-1), mlstm_f_bias[l].reshape(-1),
                                            dn_dt_bias[l].reshape(-1), zeros8]), (0, LANES - 32))
        alog_row = jnp.pad(jnp.concatenate([zeros8, zeros8, dn_a_log[l].reshape(-1), zeros8]), (0, LANES - 32))
        a = _gate_prep(g, bias_row.reshape(1, LANES), alog_row.reshape(1, LANES))
        at = jnp.swapaxes(a[:, :, :32], 1, 2)
        at_ch = at.reshape(bsz, 32, t // CH, CH).transpose(0, 2, 1, 3)
        at_dc = jnp.repeat(at.reshape(bsz, 32, t // (2 * DC), 2, 1, DC), 2, axis=4)
        at_dc = at_dc.reshape(bsz, 32, t // (2 * DC), PACK * DC).transpose(0, 2, 1, 3)
        y_m = _mlstm(u, a, at_ch, ctx_len)
        y_r = _retention(u, ret_log_gamma[l], cos_t, sin_t, ctx_len)
        y_a = _attention(u, cos_t, sin_t, attn_q_norm_w[l], attn_k_norm_w[l], ctx_len, with_ctx=(l < depth - 1))
        y_d = _deltanet(u, dn_conv_w[l], a, at_dc, ctx_len)
        xa = _out_projection(xa, (y_m, y_r, y_a, y_d), u, head_norm_w[l], w_out[l].astype(BF16), mod[l],
                             post_norm_w[l], ctx_len, latent_only=(l == depth - 1))
    return xa
```

```python
import functools

import jax
import jax.numpy as jnp
from jax import lax
from jax.experimental import pallas as pl
from jax.experimental.pallas import tpu as pltpu

F32 = jnp.float32
BF16 = jnp.bfloat16
HI = lax.Precision.HIGHEST

D_MODEL = 2048
GROUP_W = 512
N_HEADS = 4
HEAD_DIM = 128
KV_HEADS = 2
GRID_W = 64
CONV_K = 5
ROPE_BASE = 10000.0
EPS = 1e-6
SCALE = HEAD_DIM ** -0.5
LOG2E = 1.4426950408889634
NEG = -1e30

LANES = 128
CH = 256
DC = 64
PACK = 4
PRE_GROUPS = 4
N_MAIN = 64 * LANES
VMEM_LIMIT = 56 * 1024 * 1024

CB = dict(m_q=0, m_k=4, m_v=8, m_o=12, m_z=16, r_q=20, r_k=24, r_v=28, r_z=32,
          a_q=36, a_k=40, a_v=42, a_z=44, d_q=48, d_k=52, d_v=56, d_z=60)


def _dot(a, b, prec=None):
    return jnp.dot(a, b, preferred_element_type=F32, precision=prec)


def _dot_nt(a, b, prec=None):
    return lax.dot_general(a, b, (((1,), (1,)), ((), ())), preferred_element_type=F32, precision=prec)


def _dot_tn(a, b, prec=None):
    return lax.dot_general(a, b, (((0,), (0,)), ((), ())), preferred_element_type=F32, precision=prec)


def _split(a):
    hi = a.astype(BF16)
    return hi, (a - hi.astype(F32)).astype(BF16)


def _dot3(a, b):
    n = a[0].shape[0]
    x = _dot(jnp.concatenate([a[0], a[1]], axis=0), b[0])
    return (x[:n] + x[n:]) + _dot(a[0], b[1])


def _sigmoid(x):
    return 1.0 / (1.0 + jnp.exp(-x))


def _lane_col(blk, idx):
    lane = lax.broadcasted_iota(jnp.int32, blk.shape, 1)
    return jnp.sum(jnp.where(lane == idx, blk, 0.0), axis=1, keepdims=True)


def _sub_row(blk, idx):
    sub = lax.broadcasted_iota(jnp.int32, blk.shape, 0)
    return jnp.sum(jnp.where(sub == idx, blk, 0.0), axis=0, keepdims=True)


def _rope(t, cos, sin_signed):
    lane = lax.broadcasted_iota(jnp.int32, t.shape, 1)
    partner = jnp.where((lane // 32) % 2 == 0, pltpu.roll(t, 96, 1), pltpu.roll(t, 32, 1))
    return t * cos + partner * sin_signed


def _params(sem):
    return pltpu.CompilerParams(dimension_semantics=sem, vmem_limit_bytes=VMEM_LIMIT)


def _mod_kernel(c_ref, w_ref, b_ref, o_ref):
    c = c_ref[...]
    o_ref[...] = _dot(c * _sigmoid(c), w_ref[...], HI) + b_ref[...]


def _modulation(c8, ada_w, ada_b):
    depth, d, n3 = ada_w.shape
    tn = 768
    return pl.pallas_call(
        _mod_kernel,
        grid=(depth, n3 // tn),
        in_specs=[pl.BlockSpec((8, d), lambda l, n: (0, 0)),
                  pl.BlockSpec((None, d, tn), lambda l, n: (l, 0, n)),
                  pl.BlockSpec((None, 1, tn), lambda l, n: (l, 0, n))],
        out_specs=pl.BlockSpec((None, 8, tn), lambda l, n: (l, 0, n)),
        out_shape=jax.ShapeDtypeStruct((depth, 8, n3), F32),
        compiler_params=_params(("parallel", "parallel")),
        name="adaln_mod",
    )(c8, ada_w, ada_b.reshape(depth, 1, n3))


def _inproj_kernel(x_ref, mod_ref, pw_ref, wm_ref, wg_ref, u_ref, g_ref, h_scr, *, tm, ctx_len):
    b = pl.program_id(0)
    i = pl.program_id(1)
    n = pl.program_id(2)

    @pl.when(n == 0)
    def _():
        x = x_ref[...]
        y = x * lax.rsqrt(jnp.mean(x * x, axis=-1, keepdims=True) + EPS) * pw_ref[...]
        row = i * tm + lax.broadcasted_iota(jnp.int32, (tm, 1), 0)
        is_ctx = row < ctx_len
        d = D_MODEL
        sh = jnp.where(is_ctx, mod_ref[4:5, 0:d], mod_ref[pl.ds(b, 1), 0:d])
        sc = jnp.where(is_ctx, mod_ref[4:5, d:2 * d], mod_ref[pl.ds(b, 1), d:2 * d])
        hh = (y * (1.0 + sc) + sh).astype(BF16)
        h_scr[...] = hh
        g_ref[...] = _dot(hh, wg_ref[...])

    u_ref[...] = _dot(h_scr[...], wm_ref[...]).astype(BF16)


def _in_projection(xa, mod, pre_w, w_main, w_gate, layer, ctx_len):
    bsz, t, d = xa.shape
    tm = t // 4 if (t % 4 == 0 and (t // 4) % 16 == 0) else CH
    tn = 1024
    kern = functools.partial(_inproj_kernel, tm=tm, ctx_len=ctx_len)
    return pl.pallas_call(
        kern,
        grid=(bsz, t // tm, N_MAIN // tn),
        in_specs=[pl.BlockSpec((None, tm, d), lambda b, i, n: (b, i, 0)),
                  pl.BlockSpec((8, 3 * d), lambda b, i, n: (0, 0)),
                  pl.BlockSpec((1, d), lambda b, i, n: (0, 0)),
                  pl.BlockSpec((None, d, tn), lambda b, i, n: (layer, 0, n)),
                  pl.BlockSpec((None, d, LANES), lambda b, i, n: (layer, 0, 0))],
        out_specs=[pl.BlockSpec((None, tm, tn), lambda b, i, n: (b, i, n)),
                   pl.BlockSpec((None, tm, LANES), lambda b, i, n: (b, i, 0))],
        out_shape=[jax.ShapeDtypeStruct((bsz, t, N_MAIN), BF16),
                   jax.ShapeDtypeStruct((bsz, t, LANES), F32)],
        scratch_shapes=[pltpu.VMEM((tm, d), BF16)],
        compiler_params=_params(("parallel", "parallel", "arbitrary")),
        name="in_proj",
    )(xa, mod, pre_w.reshape(1, d), w_main, w_gate)


def _gate_kernel(g_ref, bias_ref, alog_ref, a_ref):
    x = g_ref[...] + bias_ref[...]
    lane = lax.broadcasted_iota(jnp.int32, x.shape, 1)
    l1p = jnp.log(1.0 + jnp.exp(-jnp.abs(x)))
    log_f = jnp.minimum(x, 0.0) - l1p
    log_a = -jnp.exp(alog_ref[...]) * (jnp.maximum(x, 0.0) + l1p)
    v = jnp.where(lane < 8, x, jnp.where(lane < 16, log_f, jnp.where(lane < 24, log_a, _sigmoid(x))))
    r = lax.broadcasted_iota(jnp.int32, (4 * CH, CH), 0)
    c = lax.broadcasted_iota(jnp.int32, (4 * CH, CH), 1)
    kind = r // CH
    rr = r % CH
    ahead = jnp.where(kind % 2 == 0, rr - c, c - rr)
    other_sub = jnp.where(kind < 2, 0, jnp.abs(rr // DC - c // DC))
    sums = jnp.where(ahead - CH * other_sub >= 0, 1.0, 0.0).astype(BF16)
    v1 = v.astype(BF16)
    res1 = v - v1.astype(F32)
    v2 = res1.astype(BF16)
    v3 = (res1 - v2.astype(F32)).astype(BF16)
    x = _dot(sums, v1) + (_dot(sums, v2) + _dot(sums, v3))
    fwd = (lane % 8) < 4
    cum_ch = jnp.where(fwd, x[0:CH], x[CH:2 * CH])
    cum_dc = jnp.where(fwd, x[2 * CH:3 * CH], x[3 * CH:])
    a_ref[...] = jnp.where((lane >= 8) & (lane < 16), cum_ch, jnp.where((lane >= 16) & (lane < 24), cum_dc, v))


def _gate_prep(g, bias_row, alog_row):
    bsz, t, _ = g.shape
    return pl.pallas_call(
        _gate_kernel,
        grid=(bsz, t // CH),
        in_specs=[pl.BlockSpec((None, CH, LANES), lambda b, i: (b, i, 0)),
                  pl.BlockSpec((1, LANES), lambda b, i: (0, 0)),
                  pl.BlockSpec((1, LANES), lambda b, i: (0, 0))],
        out_specs=pl.BlockSpec((None, CH, LANES), lambda b, i: (b, i, 0)),
        out_shape=jax.ShapeDtypeStruct((bsz, t, LANES), F32),
        compiler_params=_params(("parallel", "parallel")),
        name="gate_prep",
    )(g, bias_row, alog_row)


def _chunk_order(j, d, n_chunks, n_ctx_chunks):
    if d == 0:
        return j
    return jnp.where(j < n_ctx_chunks, n_ctx_chunks - 1 - j, n_chunks + n_ctx_chunks - 1 - j)


def _head_spec(t, name):
    base = CB[name]
    return pl.BlockSpec((None, t, LANES), lambda b, h: (b, 0, base + h))


def _mlstm_kernel(q_ref, k_ref, v_ref, o_ref, a_ref, at_ref, y_ref, c_scr, y1_scr, *, n_chunks, n_ctx_chunks):
    h = pl.program_id(1)
    r = lax.broadcasted_iota(jnp.int32, (CH, CH), 0)
    c = lax.broadcasted_iota(jnp.int32, (CH, CH), 1)
    one_col = (lax.broadcasted_iota(jnp.int32, (CH, LANES), 1) == 0).astype(F32)

    masks = ((c <= r), (c >= r))
    lasts = (CH - 1, 0)
    dirs = range(2)
    c_scr[...] = jnp.zeros_like(c_scr)

    def step(j, ms):
        cis = [_chunk_order(j, d, n_chunks, n_ctx_chunks) for d in dirs]
        rows = [pl.ds(pl.multiple_of(ci * CH, CH), CH) for ci in cis]
        qs = [q_ref[rw, :] for rw in rows]
        ks = [k_ref[rw, :] for rw in rows]
        qk = [_dot_nt(qs[d], ks[d]) for d in dirs]
        pc = [_dot(qs[d], c_scr[d].astype(BF16)) for d in dirs]
        a_blk = [a_ref[rw, :] for rw in rows]
        at_blk = [at_ref[ci] for ci in cis]
        li_col = [_lane_col(a_blk[d], d * 4 + h) for d in dirs]
        b_col = [_lane_col(a_blk[d], 8 + d * 4 + h) for d in dirs]
        li_row = [_sub_row(at_blk[d], d * 4 + h) for d in dirs]
        b_row = [_sub_row(at_blk[d], 8 + d * 4 + h) for d in dirs]
        dm = [jnp.where(masks[d], b_col[d] - b_row[d] + li_row[d], NEG) for d in dirs]
        inter = [ms[d] + b_col[d] for d in dirs]
        m_t = [jnp.maximum(inter[d], jnp.max(dm[d], axis=1, keepdims=True)) for d in dirs]
        s = [(qk[d] * (jnp.exp(dm[d] - m_t[d]) * SCALE)).astype(BF16) for d in dirs]
        v_aug = [jnp.concatenate([v_ref[rw, :].astype(F32), one_col], axis=1) for rw in rows]
        sv = [_dot(s[d], v_aug[d].astype(BF16)) for d in dirs]

        b_end = [b_col[d][lasts[d]:lasts[d] + 1, :] for d in dirs]
        dec = [b_end[d] - b_col[d] + li_col[d] for d in dirs]
        m_new = [jnp.maximum(ms[d] + b_end[d], jnp.max(dec[d], axis=0, keepdims=True)) for d in dirs]
        wv = [(jnp.exp(dec[d] - m_new[d]) * v_aug[d]).astype(BF16) for d in dirs]
        upd = [_dot_tn(ks[d], wv[d]) for d in dirs]
        for d in dirs:
            c_scr[d] = jnp.exp(ms[d] + b_end[d] - m_new[d]) * c_scr[d] + upd[d]

        for d in dirs:
            nd = (jnp.exp(inter[d] - m_t[d]) * SCALE) * pc[d] + sv[d]
            h_out = nd[:, :LANES] / jnp.maximum(jnp.abs(nd[:, LANES:LANES + 1]), jnp.exp(-m_t[d]))
            if d == 0:
                y_ref[rows[d], :] = h_out
            else:
                y1_scr[rows[d], :] = h_out
        return tuple(m_new)

    zero = jnp.zeros((1, 1), F32)
    lax.fori_loop(0, n_chunks, step, (zero, zero))

    def finish(ci, carry):
        rw = pl.ds(pl.multiple_of(ci * CH, CH), CH)
        y_ref[rw, :] = (y_ref[rw, :] + y1_scr[rw, :]) * _sigmoid(o_ref[rw, :].astype(F32))
        return carry

    lax.fori_loop(0, n_chunks, finish, 0)


def _mlstm(u, a, at_ch, ctx_len):
    bsz, t, _ = u.shape
    n_chunks = t // CH
    kern = functools.partial(_mlstm_kernel, n_chunks=n_chunks, n_ctx_chunks=ctx_len // CH)
    return pl.pallas_call(
        kern,
        grid=(bsz, N_HEADS),
        in_specs=[_head_spec(t, "m_q"), _head_spec(t, "m_k"), _head_spec(t, "m_v"), _head_spec(t, "m_o"),
                  pl.BlockSpec((None, t, LANES), lambda b, h: (b, 0, 0)),
                  pl.BlockSpec((None, n_chunks, 32, CH), lambda b, h: (b, 0, 0, 0))],
        out_specs=pl.BlockSpec((None, t, LANES), lambda b, h: (b, 0, h)),
        out_shape=jax.ShapeDtypeStruct((bsz, t, GROUP_W), F32),
        scratch_shapes=[pltpu.VMEM((2, HEAD_DIM, 2 * LANES), F32), pltpu.VMEM((t, LANES), F32)],
        compiler_params=_params(("parallel", "parallel")),
        name="mlstm",
    )(u, u, u, u, a, at_ch)


def _retention_kernel(lg_ref, q_ref, k_ref, v_ref, cos_ref, sin_ref, y_ref, qp_scr, kp_scr, r_scr, y1_scr, *,
                      n_chunks, n_ctx_chunks):
    h = pl.program_id(1)

    def prep(ci, carry):
        rows = pl.ds(pl.multiple_of(ci * CH, CH), CH)
        cos = cos_ref[rows, :]
        sin = sin_ref[rows, :]
        qp_scr[rows, :] = _rope(q_ref[rows, :].astype(F32), cos, sin).astype(BF16)
        kp_scr[rows, :] = (_rope(k_ref[rows, :].astype(F32), cos, sin) * SCALE).astype(BF16)
        return carry

    lax.fori_loop(0, n_chunks, prep, 0)

    r = lax.broadcasted_iota(jnp.int32, (CH, CH), 0).astype(F32)
    c = lax.broadcasted_iota(jnp.int32, (CH, CH), 1).astype(F32)
    pos = lax.broadcasted_iota(jnp.int32, (CH, 1), 0).astype(F32)

    dirs = range(2)
    decay, q_decay, k_decay, chunk_decay = [], [], [], []
    for d in dirs:
        lg = lg_ref[d * N_HEADS + h]
        rel = (r - c) if d == 0 else (c - r)
        decay.append(jnp.where(rel >= 0, jnp.exp(lg * jnp.maximum(rel, 0.0)), 0.0))
        p_vis = pos if d == 0 else (CH - 1.0) - pos
        q_decay.append(jnp.exp(lg * (p_vis + 1.0)))
        k_decay.append(jnp.exp(lg * ((CH - 1.0) - p_vis)))
        chunk_decay.append(jnp.exp(lg * CH))
    r_scr[...] = jnp.zeros_like(r_scr)

    def step(j, carry):
        cis = [_chunk_order(j, d, n_chunks, n_ctx_chunks) for d in dirs]
        rows = [pl.ds(pl.multiple_of(ci * CH, CH), CH) for ci in cis]
        qs = [qp_scr[rw, :] for rw in rows]
        ks = [kp_scr[rw, :] for rw in rows]
        vs = [v_ref[rw, :] for rw in rows]
        qk = [_dot_nt(qs[d], ks[d]) for d in dirs]
        qr = [_dot(qs[d], r_scr[d].astype(BF16)) for d in dirs]
        sv = [_dot((qk[d] * decay[d]).astype(BF16), vs[d]) for d in dirs]
        upd = [_dot_tn((ks[d].astype(F32) * k_decay[d]).astype(BF16), vs[d]) for d in dirs]
        for d in dirs:
            r_scr[d] = chunk_decay[d] * r_scr[d] + upd[d]
        y_ref[rows[0], :] = sv[0] + q_decay[0] * qr[0]
        y1_scr[rows[1], :] = sv[1] + q_decay[1] * qr[1]
        return carry

    lax.fori_loop(0, n_chunks, step, 0)

    def finish(ci, carry):
        rw = pl.ds(pl.multiple_of(ci * CH, CH), CH)
        y_ref[rw, :] = y_ref[rw, :] + y1_scr[rw, :]
        return carry

    lax.fori_loop(0, n_chunks, finish, 0)


def _retention(u, log_gamma, cos_t, sin_t, ctx_len):
    bsz, t, _ = u.shape
    n_chunks = t // CH
    kern = functools.partial(_retention_kernel, n_chunks=n_chunks, n_ctx_chunks=ctx_len // CH)
    tab = pl.BlockSpec((t, LANES), lambda b, h: (0, 0))
    return pl.pallas_call(
        kern,
        grid=(bsz, N_HEADS),
        in_specs=[pl.BlockSpec(memory_space=pltpu.SMEM),
                  _head_spec(t, "r_q"), _head_spec(t, "r_k"), _head_spec(t, "r_v"), tab, tab],
        out_specs=pl.BlockSpec((None, t, LANES), lambda b, h: (b, 0, h)),
        out_shape=jax.ShapeDtypeStruct((bsz, t, GROUP_W), F32),
        scratch_shapes=[pltpu.VMEM((t, LANES), BF16), pltpu.VMEM((t, LANES), BF16),
                        pltpu.VMEM((2, HEAD_DIM, HEAD_DIM), F32), pltpu.VMEM((t, LANES), F32)],
        compiler_params=_params(("parallel", "parallel")),
        name="retention",
    )(log_gamma.reshape(2 * N_HEADS), u, u, u, cos_t, sin_t)


def _attn_kernel(q_ref, k_ref, v_ref, cos_ref, sin_ref, cosq_ref, sinq_ref, qw_ref, kw_ref, o_ref,
                 kt_scr, s_scr, m_scr, *, n_blocks, with_ctx):
    j = pl.program_id(2)

    def load_q():
        qs = []
        for g in range(2):
            q = q_ref[:, g * LANES:(g + 1) * LANES].astype(F32)
            q = q * lax.rsqrt(jnp.mean(q * q, axis=-1, keepdims=True) + EPS) * qw_ref[...]
            qs.append((_rope(q, cosq_ref[...], sinq_ref[...]) * (SCALE * LOG2E)).astype(BF16))
        return jnp.concatenate(qs, axis=0)

    def finish(o, l_part):
        o = o / jnp.sum(l_part, axis=-1, keepdims=True)
        o_ref[:, 0:LANES] = o[:CH]
        o_ref[:, LANES:2 * LANES] = o[CH:]

    @pl.when(j == 0)
    def _():
        def prep(ci, carry):
            rows = pl.ds(pl.multiple_of(ci * CH, CH), CH)
            k = k_ref[rows, :].astype(F32)
            k = k * lax.rsqrt(jnp.mean(k * k, axis=-1, keepdims=True) + EPS) * kw_ref[...]
            kt_scr[ci] = _rope(k, cos_ref[rows, :], sin_ref[rows, :]).T.astype(BF16)
            return carry

        lax.fori_loop(0, n_blocks, prep, 0)
        if with_ctx:
            s = _dot(load_q(), kt_scr[0])
            p = jnp.exp2(s - jnp.max(s, axis=-1, keepdims=True))
            finish(_dot(p.astype(BF16), v_ref[0:CH, :]), p[:, :LANES] + p[:, LANES:])
        else:
            o_ref[...] = jnp.zeros_like(o_ref)

    @pl.when(j >= 1)
    def _():
        qq = load_q()
        mx = None
        for kb in range(n_blocks):
            s = _dot(qq, kt_scr[kb])
            s_scr[:, kb * CH:(kb + 1) * CH] = s
            m2 = jnp.maximum(s[:, :LANES], s[:, LANES:])
            mx = m2 if mx is None else jnp.maximum(mx, m2)
        m_scr[...] = jnp.broadcast_to(jnp.max(mx, axis=-1, keepdims=True), (2 * CH, LANES))

    def value_pass(_, carry):
        m = m_scr[...]
        l_acc = o = None
        for kb in range(n_blocks):
            cols = slice(kb * CH, (kb + 1) * CH)
            p_lo = jnp.exp2(s_scr[:, kb * CH:kb * CH + LANES] - m)
            p_hi = jnp.exp2(s_scr[:, kb * CH + LANES:(kb + 1) * CH] - m)
            l_acc = (p_lo + p_hi) if l_acc is None else l_acc + (p_lo + p_hi)
            pv = _dot(jnp.concatenate([p_lo, p_hi], axis=1).astype(BF16), v_ref[cols, :])
            o = pv if o is None else o + pv
        finish(o, l_acc)
        return carry

    lax.fori_loop(0, jnp.minimum(j, 1), value_pass, 0)


def _attention(u, cos_t, sin_t, qn_w, kn_w, ctx_len, with_ctx):
    bsz, t, _ = u.shape
    assert ctx_len == CH
    n_blocks = t // CH
    kern = functools.partial(_attn_kernel, n_blocks=n_blocks, with_ctx=with_ctx)
    qb = CB["a_q"] // 2
    kb = CB["a_k"]
    vb = CB["a_v"]
    tab = pl.BlockSpec((t, LANES), lambda b, kv, j: (0, 0))
    tabq = pl.BlockSpec((CH, LANES), lambda b, kv, j: (j, 0))
    vec = pl.BlockSpec((1, LANES), lambda b, kv, j: (0, 0))
    return pl.pallas_call(
        kern,
        grid=(bsz, KV_HEADS, n_blocks),
        in_specs=[pl.BlockSpec((None, CH, 2 * LANES), lambda b, kv, j: (b, j, qb + kv)),
                  pl.BlockSpec((None, t, LANES), lambda b, kv, j: (b, 0, kb + kv)),
                  pl.BlockSpec((None, t, LANES), lambda b, kv, j: (b, 0, vb + kv)),
                  tab, tab, tabq, tabq, vec, vec],
        out_specs=pl.BlockSpec((None, CH, 2 * LANES), lambda b, kv, j: (b, j, kv)),
        out_shape=jax.ShapeDtypeStruct((bsz, t, GROUP_W), F32),
        scratch_shapes=[pltpu.VMEM((n_blocks, HEAD_DIM, CH), BF16), pltpu.VMEM((2 * CH, t), F32),
                        pltpu.VMEM((2 * CH, LANES), F32)],
        compiler_params=_params(("parallel", "parallel", "arbitrary")),
        name="attention",
    )(u, u, u, cos_t, sin_t, cos_t, sin_t, qn_w.reshape(1, LANES), kn_w.reshape(1, LANES))


def _block_diag(y, lane_masks):
    return jnp.concatenate([y * mk for mk in lane_masks], axis=0)


def _packed_dot3(a, b, lane_masks):
    n = a[0].shape[0]
    x = _dot(jnp.concatenate([a[0], a[1]], axis=0), _block_diag(b[0], lane_masks))
    return (x[:n] + x[n:]) + _dot(a[0], _block_diag(b[1], lane_masks))


def _tri_inverse(n_mats, r, cl, lane_masks):
    eye = (r == cl).astype(F32)
    diag16 = (r // 16) == (cl // 16)
    ms = [-jnp.where(diag16, n, 0.0) for n in n_mats]
    ps = [eye + m for m in ms]
    mps = [_split(m) for m in ms]
    mps = [_split(_packed_dot3(mp, mp, lane_masks)) for mp in mps]
    for _ in range(2):
        both = [_packed_dot3(tuple(jnp.concatenate([a, b], axis=0) for a, b in zip(_split(p), mp)), mp, lane_masks)
                for p, mp in zip(ps, mps)]
        ps = [p + x[:DC] for p, x in zip(ps, both)]
        mps = [_split(x[DC:]) for x in both]
    ps = [p + _packed_dot3(_split(p), mp, lane_masks) for p, mp in zip(ps, mps)]
    for w in (32, 64):
        off = ((r // w) == (cl // w)) & ((r // (w // 2)) != (cl // (w // 2)))
        pbs = [p.astype(BF16) for p in ps]
        pds = [_block_diag(pb, lane_masks) for pb in pbs]
        tmp = [_dot(jnp.where(off, n, 0.0).astype(BF16), pd).astype(BF16) for n, pd in zip(n_mats, pds)]
        ps = [p - _dot(pb, _block_diag(a, lane_masks)) for p, pb, a in zip(ps, pbs, tmp)]
    return ps


def _deltanet_kernel(q_ref, k_ref, v_ref, wq_ref, wk_ref, wv_ref, a_ref, at_ref, y_ref,
                     xs_scr, qd_scr, kd_scr, vd_scr, aq_scr, b_scr, ge_scr, y1_scr, *,
                     t, ctx_len):
    h = pl.program_id(1)
    n_chunks = t // DC
    n_ctx_chunks = ctx_len // DC
    pad = 8

    zeros = jnp.zeros((pad, LANES), F32)
    xs_scr[0:pad, :] = zeros
    xs_scr[ctx_len + pad:ctx_len + 2 * pad, :] = zeros
    xs_scr[t + 2 * pad:t + 3 * pad, :] = zeros

    def seg_row(ci):
        return ci * CH + pad + jnp.where(ci * CH >= ctx_len, pad, 0)

    def prep_stream(src_ref, w_ref, dst_scr, l2, scale):
        def load(ci, carry):
            rows = pl.ds(pl.multiple_of(ci * CH, CH), CH)
            xs_scr[pl.ds(pl.multiple_of(seg_row(ci), 8), CH), :] = src_ref[rows, :].astype(F32)
            return carry

        lax.fori_loop(0, t // CH, load, 0)
        w = w_ref[...]

        def conv(ci, carry):
            base = seg_row(ci) - CONV_K // 2
            acc = xs_scr[pl.ds(base, CH), :] * w[0:1, :]
            for j in range(1, CONV_K):
                acc = acc + xs_scr[pl.ds(base + j, CH), :] * w[j:j + 1, :]
            acc = acc * _sigmoid(acc)
            if l2:
                acc = acc * lax.rsqrt(jnp.sum(acc * acc, axis=-1, keepdims=True) + EPS) * scale
            dst_scr[pl.ds(pl.multiple_of(ci * CH, CH), CH), :] = acc
            return carry

        lax.fori_loop(0, t // CH, conv, 0)

    prep_stream(q_ref, wq_ref, qd_scr, True, SCALE)
    prep_stream(k_ref, wk_ref, kd_scr, True, 1.0)
    prep_stream(v_ref, wv_ref, vd_scr, False, 1.0)

    wp = PACK * DC
    r = lax.broadcasted_iota(jnp.int32, (DC, wp), 0)
    lane = lax.broadcasted_iota(jnp.int32, (DC, wp), 1)
    blk = lane // DC
    cl = lane % DC
    fwd = (blk % 2) == 0
    ahead = jnp.where(fwd, r - cl, cl - r)
    incl = ahead >= 0
    strict = ahead > 0
    lane_masks = [(blk == i).astype(BF16) for i in range(PACK)]

    def pick(vals):
        out = vals[PACK - 1]
        for i in range(PACK - 2, -1, -1):
            out = jnp.where(blk == i, vals[i], out)
        return out

    def pre(first_pair, n_groups):
        groups = []
        n_mats = []
        for gg in range(n_groups):
            pair = first_pair + gg
            at_blk = at_ref[pair]
            g_row = jnp.where(fwd[0:1, :], _sub_row(at_blk, 16 + h), _sub_row(at_blk, 20 + h))
            kks, qks, g_cols, betas, chains = [], [], [], [], []
            for cc in range(2):
                ci = pair * 2 + cc
                rows = pl.ds(pl.multiple_of(ci * DC, DC), DC)
                q = qd_scr[rows, :]
                k = kd_scr[rows, :]
                v = vd_scr[rows, :]
                a_blk = a_ref[rows, :]
                kb = k.astype(BF16)
                kb2 = jnp.concatenate([kb, kb], axis=0)
                kks.append(_dot_nt(kb, kb2))
                qks.append(_dot_nt(q.astype(BF16), kb2))
                for d in range(2):
                    last = DC - 1 if d == 0 else 0
                    g_col = _lane_col(a_blk, 16 + d * 4 + h)
                    beta = _lane_col(a_blk, 24 + d * 4 + h)
                    eg = jnp.exp(g_col)
                    g_end = g_col[last:last + 1, :]
                    rhs = jnp.concatenate([beta * v, (beta * eg) * k], axis=1).astype(BF16)
                    ke = (jnp.exp(g_end - g_col) * k).astype(BF16)
                    ge_scr[d, ci] = jnp.broadcast_to(jnp.exp(g_end), (8, LANES))
                    g_cols.append(g_col)
                    betas.append(beta)
                    chains.append((d, ci, rows, rhs, ke, eg * q))
            decay = jnp.exp(jnp.where(incl, pick(g_cols) - g_row, NEG))
            n_mats.append(jnp.where(strict, pick(betas) * decay * jnp.concatenate(kks, axis=1), 0.0))
            qkd = (jnp.concatenate(qks, axis=1) * decay).astype(BF16)
            groups.append((chains, qkd, jnp.concatenate([ch[3] for ch in chains], axis=0)))
        t_invs = [t_inv.astype(BF16) for t_inv in _tri_inverse(n_mats, r, cl, lane_masks)]
        ws = [[_dot(t_inv * mk, rhs_all).astype(BF16) for mk in lane_masks]
              for t_inv, (_, _, rhs_all) in zip(t_invs, groups)]
        w_all = [jnp.concatenate(w4, axis=0) for w4 in ws]
        kws = [[_dot_tn(ch[4], w) for ch, w in zip(chains, w4)] for (chains, _, _), w4 in zip(groups, ws)]
        qws = [[_dot(qkd * mk, wa) for mk in lane_masks] for (_, qkd, _), wa in zip(groups, w_all)]
        for (chains, _, _), kw4, qw4 in zip(groups, kws, qws):
            for (d, ci, rows, _, _, egq), kw, qw in zip(chains, kw4, qw4):
                aq_scr[d, ci, 0:HEAD_DIM, :] = (-kw[:, LANES:]).astype(BF16)
                aq_scr[d, ci, HEAD_DIM:HEAD_DIM + DC, :] = (egq - qw[:, LANES:]).astype(BF16)
                b_scr[d, ci] = kw[:, :LANES]
                if d == 0:
                    y_ref[rows, :] = qw[:, :LANES]
                else:
                    y1_scr[rows, :] = qw[:, :LANES]

    n_pairs = n_chunks // 2
    n_trips, tail = divmod(n_pairs, PRE_GROUPS)

    def pre_trip(i, carry):
        pre(i * PRE_GROUPS, PRE_GROUPS)
        return carry

    lax.fori_loop(0, n_trips, pre_trip, 0)
    if tail:
        pre(n_trips * PRE_GROUPS, tail)

    def scan(j, carry):
        cis = [_chunk_order(j, d, n_chunks, n_ctx_chunks) for d in range(2)]
        xs = [_dot(aq_scr[d, cis[d]], carry[d].astype(BF16)) for d in range(2)]
        new = tuple(ge_scr[d, cis[d]][0:1, :] * carry[d] + (xs[d][:HEAD_DIM, :] + b_scr[d, cis[d]])
                    for d in range(2))
        rows = [pl.ds(pl.multiple_of(cis[d] * DC, DC), DC) for d in range(2)]
        y_ref[rows[0], :] = y_ref[rows[0], :] + xs[0][HEAD_DIM:, :]
        y1_scr[rows[1], :] = y1_scr[rows[1], :] + xs[1][HEAD_DIM:, :]
        return new

    zero = jnp.zeros((HEAD_DIM, HEAD_DIM), F32)
    lax.fori_loop(0, n_chunks, scan, (zero, zero))

    def add(ci, carry):
        rows = pl.ds(pl.multiple_of(ci * CH, CH), CH)
        y_ref[rows, :] = y_ref[rows, :] + y1_scr[rows, :]
        return carry

    lax.fori_loop(0, t // CH, add, 0)


def _deltanet(u, conv_w, a, at_dc, ctx_len):
    bsz, t, _ = u.shape
    n_chunks = t // DC
    kern = functools.partial(_deltanet_kernel, t=t, ctx_len=ctx_len)

    def wspec(off):
        return pl.BlockSpec((CONV_K, LANES), lambda b, h: (0, off + h))

    return pl.pallas_call(
        kern,
        grid=(bsz, N_HEADS),
        in_specs=[_head_spec(t, "d_q"), _head_spec(t, "d_k"), _head_spec(t, "d_v"),
                  wspec(0), wspec(N_HEADS), wspec(2 * N_HEADS),
                  pl.BlockSpec((None, t, LANES), lambda b, h: (b, 0, 0)),
                  pl.BlockSpec((None, n_chunks // 2, 32, PACK * DC), lambda b, h: (b, 0, 0, 0))],
        out_specs=pl.BlockSpec((None, t, LANES), lambda b, h: (b, 0, h)),
        out_shape=jax.ShapeDtypeStruct((bsz, t, GROUP_W), F32),
        scratch_shapes=[pltpu.VMEM((t + 24, LANES), F32),
                        pltpu.VMEM((t, LANES), F32), pltpu.VMEM((t, LANES), F32), pltpu.VMEM((t, LANES), F32),
                        pltpu.VMEM((2, n_chunks, HEAD_DIM + DC, LANES), BF16),
                        pltpu.VMEM((2, n_chunks, HEAD_DIM, LANES), F32),
                        pltpu.VMEM((2, n_chunks, 8, LANES), F32),
                        pltpu.VMEM((t, LANES), F32)],
        compiler_params=_params(("parallel", "parallel")),
        name="deltanet",
    )(u, u, u, conv_w, conv_w, conv_w, a, at_dc)


def _out_kernel(x_ref, ym_ref, yr_ref, ya_ref, yd_ref, zm_ref, zr_ref, za_ref, zd_ref, hn_ref, w_ref,
                mod_ref, pw_ref, o_ref, *, first_block):
    b = pl.program_id(0)
    i = pl.program_id(1) + first_block

    def silu(z_ref):
        z = z_ref[...].astype(F32)
        return z * _sigmoid(z)

    def head_rms(y_ref, g):
        y = y_ref[...]
        parts = []
        for hh in range(N_HEADS):
            p = y[:, hh * LANES:(hh + 1) * LANES]
            parts.append(p * lax.rsqrt(jnp.mean(p * p, axis=-1, keepdims=True) + EPS))
        return jnp.concatenate(parts, axis=1) * hn_ref[:, g * GROUP_W:(g + 1) * GROUP_W]

    merged = jnp.concatenate([
        (head_rms(ym_ref, 0) * silu(zm_ref)).astype(BF16),
        (head_rms(yr_ref, 1) * silu(zr_ref)).astype(BF16),
        (ya_ref[...] * silu(za_ref)).astype(BF16),
        (head_rms(yd_ref, 2) * silu(zd_ref)).astype(BF16)], axis=1)
    o = _dot(merged, w_ref[...])
    o = o * lax.rsqrt(jnp.mean(o * o, axis=-1, keepdims=True) + EPS) * pw_ref[...]
    row = jnp.where(i == 0, 4, b)
    gate = mod_ref[pl.ds(row, 1), 2 * D_MODEL:3 * D_MODEL]
    o_ref[...] = x_ref[...] + gate * o


def _out_projection(xa, ys, u, hn_w, w_out, mod, post_w, ctx_len, latent_only):
    bsz, t, d = xa.shape
    assert ctx_len == CH
    first = 1 if latent_only else 0
    tok = lambda b, i: (b, i + first, 0)
    yspec = pl.BlockSpec((None, CH, GROUP_W), tok)

    def zspec(name):
        blk = CB[name] // N_HEADS
        return pl.BlockSpec((None, CH, GROUP_W), lambda b, i: (b, i + first, blk))

    return pl.pallas_call(
        functools.partial(_out_kernel, first_block=first),
        grid=(bsz, t // CH - first),
        in_specs=[pl.BlockSpec((None, CH, d), tok), yspec, yspec, yspec, yspec,
                  zspec("m_z"), zspec("r_z"), zspec("a_z"), zspec("d_z"),
                  pl.BlockSpec((1, 3 * GROUP_W), lambda b, i: (0, 0)),
                  pl.BlockSpec((d, d), lambda b, i: (0, 0)),
                  pl.BlockSpec((8, 3 * d), lambda b, i: (0, 0)),
                  pl.BlockSpec((1, d), lambda b, i: (0, 0))],
        out_specs=pl.BlockSpec((None, CH, d), lambda b, i: (b, i, 0)),
        out_shape=jax.ShapeDtypeStruct((bsz, t - first * CH, d), F32),
        compiler_params=_params(("parallel", "parallel")),
        name="out_proj",
    )(xa, *ys, u, u, u, u, hn_w.reshape(1, 3 * GROUP_W), w_out, mod, post_w.reshape(1, d))


M_GATE_COL = 5 * GROUP_W
N_M_GATES = 4 * N_HEADS
D_GATE_COL = N_MAIN + N_M_GATES
W_TILE = 512


def _wprep_kernel(a_ref, b_ref, o_ref):
    n = pl.program_id(1)

    @pl.when(n < M_GATE_COL // W_TILE)
    def _():
        o_ref[...] = a_ref[...].astype(BF16)

    @pl.when(n >= M_GATE_COL // W_TILE)
    def _():
        o_ref[...] = jnp.concatenate([a_ref[:, N_M_GATES:], b_ref[:, :N_M_GATES]], axis=1).astype(BF16)


def _reorder_w_in(w_in):
    depth, d, _ = w_in.shape
    assert M_GATE_COL % W_TILE == 0
    w_main = pl.pallas_call(
        _wprep_kernel,
        grid=(depth, N_MAIN // W_TILE),
        in_specs=[pl.BlockSpec((None, d, W_TILE), lambda l, n: (l, 0, n)),
                  pl.BlockSpec((None, d, LANES), lambda l, n: (l, 0, (n + 1) * (W_TILE // LANES)))],
        out_specs=pl.BlockSpec((None, d, W_TILE), lambda l, n: (l, 0, n)),
        out_shape=jax.ShapeDtypeStruct((depth, d, N_MAIN), BF16),
        compiler_params=_params(("parallel", "parallel")),
        name="w_in_prep",
    )(w_in, w_in)
    gates = jnp.concatenate([w_in[:, :, M_GATE_COL:M_GATE_COL + N_M_GATES], w_in[:, :, D_GATE_COL:]], axis=2)
    w_gate = jnp.pad(gates, ((0, 0), (0, 0), (0, LANES - gates.shape[2]))).astype(BF16)
    return w_main, w_gate


def _rope_tables(seq, ctx_len):
    rows = seq // GRID_W
    row = jnp.repeat(jnp.arange(rows), GRID_W)
    col = jnp.tile(jnp.arange(GRID_W), rows)
    n_freq = HEAD_DIM // 4
    inv_freq = ROPE_BASE ** (-jnp.arange(n_freq, dtype=F32) / n_freq)
    ar = row[:, None] * inv_freq
    ac = col[:, None] * inv_freq
    cos = jnp.concatenate([jnp.cos(ar), jnp.cos(ar), jnp.cos(ac), jnp.cos(ac)], axis=1)
    sin = jnp.concatenate([-jnp.sin(ar), jnp.sin(ar), -jnp.sin(ac), jnp.sin(ac)], axis=1)
    cos = jnp.concatenate([jnp.ones((ctx_len, HEAD_DIM), F32), cos], axis=0)
    sin = jnp.concatenate([jnp.zeros((ctx_len, HEAD_DIM), F32), sin], axis=0)
    return cos, sin


def kernel(x, c, ctx, c_ctx, ada_w, ada_b, pre_norm_w, post_norm_w, w_in, w_out, mlstm_i_bias, mlstm_f_bias,
           ret_log_gamma, attn_q_norm_w, attn_k_norm_w, dn_conv_w, dn_a_log, dn_dt_bias, head_norm_w):
    bsz, seq, d = x.shape
    ctx_len = ctx.shape[1]
    t = ctx_len + seq
    depth = ada_w.shape[0]
    assert bsz <= 4 and ctx_len % CH == 0 and seq % CH == 0

    xa = jnp.concatenate([ctx, x], axis=1)
    c8 = jnp.zeros((8, d), F32).at[:bsz].set(c).at[4].set(c_ctx)
    mod = _modulation(c8, ada_w, ada_b)
    cos_t, sin_t = _rope_tables(seq, ctx_len)

    w_main, w_gate = _reorder_w_in(w_in)

    for l in range(depth):
        u, g = _in_projection(xa, mod[l], pre_norm_w[l], w_main, w_gate, l, ctx_len)
        zeros8 = jnp.zeros((8,), F32)
        bias_row = jnp.pad(jnp.concatenate([mlstm_i_bias[l].reshape(-1), mlstm_f_bias[l].reshape(-1),
                                            dn_dt_bias[l].reshape(-1), zeros8]), (0, LANES - 32))
        alog_row = jnp.pad(jnp.concatenate([zeros8, zeros8, dn_a_log[l].reshape(-1), zeros8]), (0, LANES - 32))
        a = _gate_prep(g, bias_row.reshape(1, LANES), alog_row.reshape(1, LANES))
        at = jnp.swapaxes(a[:, :, :32], 1, 2)
        at_ch = at.reshape(bsz, 32, t // CH, CH).transpose(0, 2, 1, 3)
        at_dc = jnp.repeat(at.reshape(bsz, 32, t // (2 * DC), 2, 1, DC), 2, axis=4)
        at_dc = at_dc.reshape(bsz, 32, t // (2 * DC), PACK * DC).transpose(0, 2, 1, 3)
        y_m = _mlstm(u, a, at_ch, ctx_len)
        y_r = _retention(u, ret_log_gamma[l], cos_t, sin_t, ctx_len)
        y_a = _attention(u, cos_t, sin_t, attn_q_norm_w[l], attn_k_norm_w[l], ctx_len, with_ctx=(l < depth - 1))
        y_d = _deltanet(u, dn_conv_w[l], a, at_dc, ctx_len)
        xa = _out_projection(xa, (y_m, y_r, y_a, y_d), u, head_norm_w[l], w_out[l].astype(BF16), mod[l],
                             post_norm_w[l], ctx_len, latent_only=(l == depth - 1))
    return xa
```

```python
import functools

import jax
import jax.numpy as jnp
from jax import lax
from jax.experimental import pallas as pl
from jax.experimental.pallas import tpu as pltpu

F32 = jnp.float32
BF16 = jnp.bfloat16
HI = lax.Precision.HIGHEST

D_MODEL = 2048
GROUP_W = 512
N_HEADS = 4
HEAD_DIM = 128
KV_HEADS = 2
GRID_W = 64
CONV_K = 5
ROPE_BASE = 10000.0
EPS = 1e-6
SCALE = HEAD_DIM ** -0.5
LOG2E = 1.4426950408889634
NEG = -1e30

LANES = 128
CH = 256
DC = 64
PACK = 4
PRE_GROUPS = 4
N_MAIN = 64 * LANES
VMEM_LIMIT = 56 * 1024 * 1024
IN_PROJ_MAX_TM = 1088

CB = dict(m_q=0, m_k=4, m_v=8, m_o=12, m_z=16, r_q=20, r_k=24, r_v=28, r_z=32,
          a_q=36, a_k=40, a_v=42, a_z=44, d_q=48, d_k=52, d_v=56, d_z=60)


def _dot(a, b, prec=None):
    return jnp.dot(a, b, preferred_element_type=F32, precision=prec)


def _dot_nt(a, b, prec=None):
    return lax.dot_general(a, b, (((1,), (1,)), ((), ())), preferred_element_type=F32, precision=prec)


def _dot_tn(a, b, prec=None):
    return lax.dot_general(a, b, (((0,), (0,)), ((), ())), preferred_element_type=F32, precision=prec)


def _split(a):
    hi = a.astype(BF16)
    return hi, (a - hi.astype(F32)).astype(BF16)


def _dot3(a, b):
    n = a[0].shape[0]
    x = _dot(jnp.concatenate([a[0], a[1]], axis=0), b[0])
    return (x[:n] + x[n:]) + _dot(a[0], b[1])


def _sigmoid(x):
    return 1.0 / (1.0 + jnp.exp(-x))


def _lane_col(blk, idx):
    lane = lax.broadcasted_iota(jnp.int32, blk.shape, 1)
    return jnp.sum(jnp.where(lane == idx, blk, 0.0), axis=1, keepdims=True)


def _sub_row(blk, idx):
    sub = lax.broadcasted_iota(jnp.int32, blk.shape, 0)
    return jnp.sum(jnp.where(sub == idx, blk, 0.0), axis=0, keepdims=True)


def _rope(t, cos, sin_signed):
    lane = lax.broadcasted_iota(jnp.int32, t.shape, 1)
    partner = jnp.where((lane // 32) % 2 == 0, pltpu.roll(t, 96, 1), pltpu.roll(t, 32, 1))
    return t * cos + partner * sin_signed


def _params(sem):
    return pltpu.CompilerParams(dimension_semantics=sem, vmem_limit_bytes=VMEM_LIMIT)


def _mod_kernel(c_ref, w_ref, b_ref, o_ref):
    c = c_ref[...]
    o_ref[...] = _dot(c * _sigmoid(c), w_ref[...], HI) + b_ref[...]


def _modulation(c8, ada_w, ada_b):
    depth, d, n3 = ada_w.shape
    tn = 768
    return pl.pallas_call(
        _mod_kernel,
        grid=(depth, n3 // tn),
        in_specs=[pl.BlockSpec((8, d), lambda l, n: (0, 0)),
                  pl.BlockSpec((None, d, tn), lambda l, n: (l, 0, n)),
                  pl.BlockSpec((None, 1, tn), lambda l, n: (l, 0, n))],
        out_specs=pl.BlockSpec((None, 8, tn), lambda l, n: (l, 0, n)),
        out_shape=jax.ShapeDtypeStruct((depth, 8, n3), F32),
        compiler_params=_params(("parallel", "parallel")),
        name="adaln_mod",
    )(c8, ada_w, ada_b.reshape(depth, 1, n3))


def _inproj_kernel(x_ref, mod_ref, pw_ref, wm_ref, wg_ref, u_ref, g_ref, h_scr, *, tm, ctx_len, per_batch,
                   n_blocks, n_tiles):
    m = pl.program_id(0)
    n = pl.program_id(1)

    d = D_MODEL

    def modulated(x, row):
        y = x * lax.rsqrt(jnp.mean(x * x, axis=-1, keepdims=True) + EPS) * pw_ref[...]
        return (y * (1.0 + mod_ref[pl.ds(row, 1), d:2 * d]) + mod_ref[pl.ds(row, 1), 0:d]).astype(BF16)

    def normalise(blk, slot):
        b = blk // per_batch
        head = modulated(x_ref[0:ctx_len, :], jnp.where(blk % per_batch == 0, 4, b))
        hh = jnp.concatenate([head, modulated(x_ref[ctx_len:, :], b)], axis=0)
        h_scr[slot] = hh
        g_ref[...] = _dot(hh, wg_ref[...])

    @pl.when((m == 0) & (n == 0))
    def _():
        normalise(0, 0)

    @pl.when(n < n_tiles - 1)
    def _():
        u_ref[...] = _dot(h_scr[m % 2], wm_ref[...]).astype(BF16)

    @pl.when(n == n_tiles - 1)
    def _():
        u_ref[...] = _dot(h_scr[m % 2], wm_ref[...]).astype(BF16)
        normalise(jnp.minimum(m + 1, n_blocks - 1), (m + 1) % 2)


def _in_projection(xa, mod, pre_w, w_main, w_gate, layer, ctx_len):
    bsz, t, d = xa.shape
    tm = max(c for c in range(16, IN_PROJ_MAX_TM + 1, 16) if t % c == 0)
    assert tm > ctx_len
    tn = 1024
    per_batch = t // tm
    n_blocks = bsz * per_batch
    n_tiles = N_MAIN // tn
    kern = functools.partial(_inproj_kernel, tm=tm, ctx_len=ctx_len, per_batch=per_batch, n_blocks=n_blocks,
                             n_tiles=n_tiles)
    nxt = lambda m: jnp.minimum(m + 1, n_blocks - 1)
    u, g = pl.pallas_call(
        kern,
        grid=(n_blocks, n_tiles),
        in_specs=[pl.BlockSpec((None, tm, d), lambda m, n: (jnp.where((m == 0) & (n == 0), 0, nxt(m)), 0, 0)),
                  pl.BlockSpec((8, 3 * d), lambda m, n: (0, 0)),
                  pl.BlockSpec((1, d), lambda m, n: (0, 0)),
                  pl.BlockSpec((None, d, tn), lambda m, n: (layer, 0, n)),
                  pl.BlockSpec((None, d, LANES), lambda m, n: (layer, 0, 0))],
        out_specs=[pl.BlockSpec((None, tm, tn), lambda m, n: (m, 0, n)),
                   pl.BlockSpec((None, tm, LANES), lambda m, n: (jnp.where(n == n_tiles - 1, nxt(m), m), 0, 0))],
        out_shape=[jax.ShapeDtypeStruct((n_blocks, tm, N_MAIN), BF16),
                   jax.ShapeDtypeStruct((n_blocks, tm, LANES), F32)],
        scratch_shapes=[pltpu.VMEM((2, tm, d), BF16)],
        compiler_params=_params(("arbitrary", "arbitrary")),
        name="in_proj",
    )(xa.reshape(n_blocks, tm, d), mod, pre_w.reshape(1, d), w_main, w_gate)
    return u.reshape(bsz, t, N_MAIN), g.reshape(bsz, t, LANES)


def _gate_kernel(g_ref, bias_ref, alog_ref, a_ref, at_ref, atp_ref):
    x = g_ref[...] + bias_ref[...]
    lane = lax.broadcasted_iota(jnp.int32, x.shape, 1)
    l1p = jnp.log(1.0 + jnp.exp(-jnp.abs(x)))
    log_f = jnp.minimum(x, 0.0) - l1p
    log_a = -jnp.exp(alog_ref[...]) * (jnp.maximum(x, 0.0) + l1p)
    v = jnp.where(lane < 8, x, jnp.where(lane < 16, log_f, jnp.where(lane < 24, log_a, _sigmoid(x))))
    r = lax.broadcasted_iota(jnp.int32, (4 * CH, CH), 0)
    c = lax.broadcasted_iota(jnp.int32, (4 * CH, CH), 1)
    kind = r // CH
    rr = r % CH
    ahead = jnp.where(kind % 2 == 0, rr - c, c - rr)
    other_sub = jnp.where(kind < 2, 0, jnp.abs(rr // DC - c // DC))
    sums = jnp.where(ahead - CH * other_sub >= 0, 1.0, 0.0).astype(BF16)
    v1 = v.astype(BF16)
    res1 = v - v1.astype(F32)
    v2 = res1.astype(BF16)
    v3 = (res1 - v2.astype(F32)).astype(BF16)
    x = _dot(sums, v1) + (_dot(sums, v2) + _dot(sums, v3))
    fwd = (lane % 8) < 4
    cum_ch = jnp.where(fwd, x[0:CH], x[CH:2 * CH])
    cum_dc = jnp.where(fwd, x[2 * CH:3 * CH], x[3 * CH:])
    a = jnp.where((lane >= 8) & (lane < 16), cum_ch, jnp.where((lane >= 16) & (lane < 24), cum_dc, v))
    a_ref[...] = a
    a_t = a.T[0:32, :]
    at_ref[...] = a_t
    for p in range(CH // (2 * DC)):
        c0 = a_t[:, 2 * p * DC:(2 * p + 1) * DC]
        c1 = a_t[:, (2 * p + 1) * DC:(2 * p + 2) * DC]
        atp_ref[p] = jnp.concatenate([c0, c0, c1, c1], axis=1)


def _gate_prep(g, bias_row, alog_row):
    bsz, t, _ = g.shape
    return pl.pallas_call(
        _gate_kernel,
        grid=(bsz, t // CH),
        in_specs=[pl.BlockSpec((None, CH, LANES), lambda b, i: (b, i, 0)),
                  pl.BlockSpec((1, LANES), lambda b, i: (0, 0)),
                  pl.BlockSpec((1, LANES), lambda b, i: (0, 0))],
        out_specs=[pl.BlockSpec((None, CH, LANES), lambda b, i: (b, i, 0)),
                   pl.BlockSpec((None, None, 32, CH), lambda b, i: (b, i, 0, 0)),
                   pl.BlockSpec((None, CH // (2 * DC), 32, PACK * DC), lambda b, i: (b, i, 0, 0))],
        out_shape=[jax.ShapeDtypeStruct((bsz, t, LANES), F32),
                   jax.ShapeDtypeStruct((bsz, t // CH, 32, CH), F32),
                   jax.ShapeDtypeStruct((bsz, t // (2 * DC), 32, PACK * DC), F32)],
        compiler_params=_params(("parallel", "parallel")),
        name="gate_prep",
    )(g, bias_row, alog_row)


def _chunk_order(j, d, n_chunks, n_ctx_chunks):
    if d == 0:
        return j
    return jnp.where(j < n_ctx_chunks, n_ctx_chunks - 1 - j, n_chunks + n_ctx_chunks - 1 - j)


def _head_spec(t, name):
    base = CB[name]
    return pl.BlockSpec((None, t, LANES), lambda b, h: (b, 0, base + h))


def _mlstm_kernel(q_ref, k_ref, v_ref, o_ref, a_ref, at_ref, y_ref, c_scr, y1_scr, *, n_chunks, n_ctx_chunks):
    h = pl.program_id(1)
    r = lax.broadcasted_iota(jnp.int32, (CH, CH), 0)
    c = lax.broadcasted_iota(jnp.int32, (CH, CH), 1)
    one_col = (lax.broadcasted_iota(jnp.int32, (CH, LANES), 1) == 0).astype(F32)

    masks = ((c <= r), (c >= r))
    lasts = (CH - 1, 0)
    dirs = range(2)
    c_scr[...] = jnp.zeros_like(c_scr)

    def step(j, ms):
        cis = [_chunk_order(j, d, n_chunks, n_ctx_chunks) for d in dirs]
        rows = [pl.ds(pl.multiple_of(ci * CH, CH), CH) for ci in cis]
        qs = [q_ref[rw, :] for rw in rows]
        ks = [k_ref[rw, :] for rw in rows]
        qk = [_dot_nt(qs[d], ks[d]) for d in dirs]
        pc = [_dot(qs[d], c_scr[d].astype(BF16)) for d in dirs]
        a_blk = [a_ref[rw, :] for rw in rows]
        at_blk = [at_ref[ci] for ci in cis]
        li_col = [_lane_col(a_blk[d], d * 4 + h) for d in dirs]
        b_col = [_lane_col(a_blk[d], 8 + d * 4 + h) for d in dirs]
        li_row = [_sub_row(at_blk[d], d * 4 + h) for d in dirs]
        b_row = [_sub_row(at_blk[d], 8 + d * 4 + h) for d in dirs]
        dm = [jnp.where(masks[d], b_col[d] - b_row[d] + li_row[d], NEG) for d in dirs]
        inter = [ms[d] + b_col[d] for d in dirs]
        m_t = [jnp.maximum(inter[d], jnp.max(dm[d], axis=1, keepdims=True)) for d in dirs]
        s = [(qk[d] * (jnp.exp(dm[d] - m_t[d]) * SCALE)).astype(BF16) for d in dirs]
        v_aug = [jnp.concatenate([v_ref[rw, :].astype(F32), one_col], axis=1) for rw in rows]
        sv = [_dot(s[d], v_aug[d].astype(BF16)) for d in dirs]

        b_end = [b_col[d][lasts[d]:lasts[d] + 1, :] for d in dirs]
        dec = [b_end[d] - b_col[d] + li_col[d] for d in dirs]
        m_new = [jnp.maximum(ms[d] + b_end[d], jnp.max(dec[d], axis=0, keepdims=True)) for d in dirs]
        wv = [(jnp.exp(dec[d] - m_new[d]) * v_aug[d]).astype(BF16) for d in dirs]
        upd = [_dot_tn(ks[d], wv[d]) for d in dirs]
        for d in dirs:
            c_scr[d] = jnp.exp(ms[d] + b_end[d] - m_new[d]) * c_scr[d] + upd[d]

        for d in dirs:
            nd = (jnp.exp(inter[d] - m_t[d]) * SCALE) * pc[d] + sv[d]
            h_out = nd[:, :LANES] / jnp.maximum(jnp.abs(nd[:, LANES:LANES + 1]), jnp.exp(-m_t[d]))
            if d == 0:
                y_ref[rows[d], :] = h_out
            else:
                y1_scr[rows[d], :] = h_out
        return tuple(m_new)

    zero = jnp.zeros((1, 1), F32)
    lax.fori_loop(0, n_chunks, step, (zero, zero))

    def finish(ci, carry):
        rw = pl.ds(pl.multiple_of(ci * CH, CH), CH)
        y_ref[rw, :] = (y_ref[rw, :] + y1_scr[rw, :]) * _sigmoid(o_ref[rw, :].astype(F32))
        return carry

    lax.fori_loop(0, n_chunks, finish, 0)


def _mlstm(u, a, at_ch, ctx_len):
    bsz, t, _ = u.shape
    n_chunks = t // CH
    kern = functools.partial(_mlstm_kernel, n_chunks=n_chunks, n_ctx_chunks=ctx_len // CH)
    return pl.pallas_call(
        kern,
        grid=(bsz, N_HEADS),
        in_specs=[_head_spec(t, "m_q"), _head_spec(t, "m_k"), _head_spec(t, "m_v"), _head_spec(t, "m_o"),
                  pl.BlockSpec((None, t, LANES), lambda b, h: (b, 0, 0)),
                  pl.BlockSpec((None, n_chunks, 32, CH), lambda b, h: (b, 0, 0, 0))],
        out_specs=pl.BlockSpec((None, t, LANES), lambda b, h: (b, 0, h)),
        out_shape=jax.ShapeDtypeStruct((bsz, t, GROUP_W), F32),
        scratch_shapes=[pltpu.VMEM((2, HEAD_DIM, 2 * LANES), F32), pltpu.VMEM((t, LANES), F32)],
        compiler_params=_params(("parallel", "parallel")),
        name="mlstm",
    )(u, u, u, u, a, at_ch)


def _retention_kernel(lg_ref, q_ref, k_ref, v_ref, cos_ref, sin_ref, y_ref, qp_scr, kp_scr, r_scr, y1_scr, *,
                      n_chunks, n_ctx_chunks):
    h = pl.program_id(1)

    def prep(ci, carry):
        rows = pl.ds(pl.multiple_of(ci * CH, CH), CH)
        cos = cos_ref[rows, :]
        sin = sin_ref[rows, :]
        qp_scr[rows, :] = _rope(q_ref[rows, :].astype(F32), cos, sin).astype(BF16)
        kp_scr[rows, :] = (_rope(k_ref[rows, :].astype(F32), cos, sin) * SCALE).astype(BF16)
        return carry

    lax.fori_loop(0, n_chunks, prep, 0)

    r = lax.broadcasted_iota(jnp.int32, (CH, CH), 0).astype(F32)
    c = lax.broadcasted_iota(jnp.int32, (CH, CH), 1).astype(F32)
    pos = lax.broadcasted_iota(jnp.int32, (CH, 1), 0).astype(F32)

    dirs = range(2)
    decay, q_decay, k_decay, chunk_decay = [], [], [], []
    for d in dirs:
        lg = lg_ref[d * N_HEADS + h]
        rel = (r - c) if d == 0 else (c - r)
        decay.append(jnp.where(rel >= 0, jnp.exp(lg * jnp.maximum(rel, 0.0)), 0.0))
        p_vis = pos if d == 0 else (CH - 1.0) - pos
        q_decay.append(jnp.exp(lg * (p_vis + 1.0)))
        k_decay.append(jnp.exp(lg * ((CH - 1.0) - p_vis)))
        chunk_decay.append(jnp.exp(lg * CH))
    r_scr[...] = jnp.zeros_like(r_scr)

    def step(j, carry):
        cis = [_chunk_order(j, d, n_chunks, n_ctx_chunks) for d in dirs]
        rows = [pl.ds(pl.multiple_of(ci * CH, CH), CH) for ci in cis]
        qs = [qp_scr[rw, :] for rw in rows]
        ks = [kp_scr[rw, :] for rw in rows]
        vs = [v_ref[rw, :] for rw in rows]
        qk = [_dot_nt(qs[d], ks[d]) for d in dirs]
        qr = [_dot(qs[d], r_scr[d].astype(BF16)) for d in dirs]
        sv = [_dot((qk[d] * decay[d]).astype(BF16), vs[d]) for d in dirs]
        upd = [_dot_tn((ks[d].astype(F32) * k_decay[d]).astype(BF16), vs[d]) for d in dirs]
        for d in dirs:
            r_scr[d] = chunk_decay[d] * r_scr[d] + upd[d]
        y_ref[rows[0], :] = sv[0] + q_decay[0] * qr[0]
        y1_scr[rows[1], :] = sv[1] + q_decay[1] * qr[1]
        return carry

    lax.fori_loop(0, n_chunks, step, 0)

    def finish(ci, carry):
        rw = pl.ds(pl.multiple_of(ci * CH, CH), CH)
        y_ref[rw, :] = y_ref[rw, :] + y1_scr[rw, :]
        return carry

    lax.fori_loop(0, n_chunks, finish, 0)


def _retention(u, log_gamma, cos_t, sin_t, ctx_len):
    bsz, t, _ = u.shape
    n_chunks = t // CH
    kern = functools.partial(_retention_kernel, n_chunks=n_chunks, n_ctx_chunks=ctx_len // CH)
    tab = pl.BlockSpec((t, LANES), lambda b, h: (0, 0))
    return pl.pallas_call(
        kern,
        grid=(bsz, N_HEADS),
        in_specs=[pl.BlockSpec(memory_space=pltpu.SMEM),
                  _head_spec(t, "r_q"), _head_spec(t, "r_k"), _head_spec(t, "r_v"), tab, tab],
        out_specs=pl.BlockSpec((None, t, LANES), lambda b, h: (b, 0, h)),
        out_shape=jax.ShapeDtypeStruct((bsz, t, GROUP_W), F32),
        scratch_shapes=[pltpu.VMEM((t, LANES), BF16), pltpu.VMEM((t, LANES), BF16),
                        pltpu.VMEM((2, HEAD_DIM, HEAD_DIM), F32), pltpu.VMEM((t, LANES), F32)],
        compiler_params=_params(("parallel", "parallel")),
        name="retention",
    )(log_gamma.reshape(2 * N_HEADS), u, u, u, cos_t, sin_t)


def _attn_kernel(q_ref, k_ref, v_ref, cos_ref, sin_ref, cosq_ref, sinq_ref, qw_ref, kw_ref, o_ref,
                 kt_scr, s_scr, m_scr, *, n_blocks, with_ctx):
    j = pl.program_id(2)

    def load_q():
        qs = []
        for g in range(2):
            q = q_ref[:, g * LANES:(g + 1) * LANES].astype(F32)
            q = q * lax.rsqrt(jnp.mean(q * q, axis=-1, keepdims=True) + EPS) * qw_ref[...]
            qs.append((_rope(q, cosq_ref[...], sinq_ref[...]) * (SCALE * LOG2E)).astype(BF16))
        return jnp.concatenate(qs, axis=0)

    def finish(o, l_part):
        o = o / jnp.sum(l_part, axis=-1, keepdims=True)
        o_ref[:, 0:LANES] = o[:CH]
        o_ref[:, LANES:2 * LANES] = o[CH:]

    @pl.when(j == 0)
    def _():
        def prep(ci, carry):
            rows = pl.ds(pl.multiple_of(ci * CH, CH), CH)
            k = k_ref[rows, :].astype(F32)
            k = k * lax.rsqrt(jnp.mean(k * k, axis=-1, keepdims=True) + EPS) * kw_ref[...]
            kt_scr[ci] = _rope(k, cos_ref[rows, :], sin_ref[rows, :]).T.astype(BF16)
            return carry

        lax.fori_loop(0, n_blocks, prep, 0)
        if with_ctx:
            s = _dot(load_q(), kt_scr[0])
            p = jnp.exp2(s - jnp.max(s, axis=-1, keepdims=True))
            finish(_dot(p.astype(BF16), v_ref[0:CH, :]), p[:, :LANES] + p[:, LANES:])
        else:
            o_ref[...] = jnp.zeros_like(o_ref)

    @pl.when(j >= 1)
    def _():
        qq = load_q()
        mx = None
        for kb in range(n_blocks):
            s = _dot(qq, kt_scr[kb])
            s_scr[:, kb * CH:(kb + 1) * CH] = s
            m2 = jnp.maximum(s[:, :LANES], s[:, LANES:])
            mx = m2 if mx is None else jnp.maximum(mx, m2)
        m_scr[...] = jnp.broadcast_to(jnp.max(mx, axis=-1, keepdims=True), (2 * CH, LANES))

    def value_pass(_, carry):
        m = m_scr[...]
        l_acc = o = None
        for kb in range(n_blocks):
            cols = slice(kb * CH, (kb + 1) * CH)
            p_lo = jnp.exp2(s_scr[:, kb * CH:kb * CH + LANES] - m)
            p_hi = jnp.exp2(s_scr[:, kb * CH + LANES:(kb + 1) * CH] - m)
            l_acc = (p_lo + p_hi) if l_acc is None else l_acc + (p_lo + p_hi)
            pv = _dot(jnp.concatenate([p_lo, p_hi], axis=1).astype(BF16), v_ref[cols, :])
            o = pv if o is None else o + pv
        finish(o, l_acc)
        return carry

    lax.fori_loop(0, jnp.minimum(j, 1), value_pass, 0)


def _attention(u, cos_t, sin_t, qn_w, kn_w, ctx_len, with_ctx):
    bsz, t, _ = u.shape
    assert ctx_len == CH
    n_blocks = t // CH
    kern = functools.partial(_attn_kernel, n_blocks=n_blocks, with_ctx=with_ctx)
    qb = CB["a_q"] // 2
    kb = CB["a_k"]
    vb = CB["a_v"]
    tab = pl.BlockSpec((t, LANES), lambda b, kv, j: (0, 0))
    tabq = pl.BlockSpec((CH, LANES), lambda b, kv, j: (j, 0))
    vec = pl.BlockSpec((1, LANES), lambda b, kv, j: (0, 0))
    return pl.pallas_call(
        kern,
        grid=(bsz, KV_HEADS, n_blocks),
        in_specs=[pl.BlockSpec((None, CH, 2 * LANES), lambda b, kv, j: (b, j, qb + kv)),
                  pl.BlockSpec((None, t, LANES), lambda b, kv, j: (b, 0, kb + kv)),
                  pl.BlockSpec((None, t, LANES), lambda b, kv, j: (b, 0, vb + kv)),
                  tab, tab, tabq, tabq, vec, vec],
        out_specs=pl.BlockSpec((None, CH, 2 * LANES), lambda b, kv, j: (b, j, kv)),
        out_shape=jax.ShapeDtypeStruct((bsz, t, GROUP_W), F32),
        scratch_shapes=[pltpu.VMEM((n_blocks, HEAD_DIM, CH), BF16), pltpu.VMEM((2 * CH, t), F32),
                        pltpu.VMEM((2 * CH, LANES), F32)],
        compiler_params=_params(("parallel", "parallel", "arbitrary")),
        name="attention",
    )(u, u, u, cos_t, sin_t, cos_t, sin_t, qn_w.reshape(1, LANES), kn_w.reshape(1, LANES))


def _block_diag(y, lane_masks):
    return jnp.concatenate([y * mk for mk in lane_masks], axis=0)


def _packed_dot3(a, b, lane_masks):
    n = a[0].shape[0]
    x = _dot(jnp.concatenate([a[0], a[1]], axis=0), _block_diag(b[0], lane_masks))
    return (x[:n] + x[n:]) + _dot(a[0], _block_diag(b[1], lane_masks))


def _tri_inverse(n_mats, r, cl, lane_masks):
    eye = (r == cl).astype(F32)
    diag16 = (r // 16) == (cl // 16)
    ms = [-jnp.where(diag16, n, 0.0) for n in n_mats]
    ps = [eye + m for m in ms]
    mps = [_split(m) for m in ms]
    mps = [_split(_packed_dot3(mp, mp, lane_masks)) for mp in mps]
    for _ in range(2):
        both = [_packed_dot3(tuple(jnp.concatenate([a, b], axis=0) for a, b in zip(_split(p), mp)), mp, lane_masks)
                for p, mp in zip(ps, mps)]
        ps = [p + x[:DC] for p, x in zip(ps, both)]
        mps = [_split(x[DC:]) for x in both]
    ps = [p + _packed_dot3(_split(p), mp, lane_masks) for p, mp in zip(ps, mps)]
    for w in (32, 64):
        off = ((r // w) == (cl // w)) & ((r // (w // 2)) != (cl // (w // 2)))
        pbs = [p.astype(BF16) for p in ps]
        pds = [_block_diag(pb, lane_masks) for pb in pbs]
        tmp = [_dot(jnp.where(off, n, 0.0).astype(BF16), pd).astype(BF16) for n, pd in zip(n_mats, pds)]
        ps = [p - _dot(pb, _block_diag(a, lane_masks)) for p, pb, a in zip(ps, pbs, tmp)]
    return ps


def _deltanet_kernel(q_ref, k_ref, v_ref, wq_ref, wk_ref, wv_ref, a_ref, at_ref, y_ref,
                     xs_scr, qd_scr, kd_scr, vd_scr, aq_scr, b_scr, ge_scr, y1_scr, *,
                     t, ctx_len):
    h = pl.program_id(1)
    n_chunks = t // DC
    n_ctx_chunks = ctx_len // DC
    pad = 8

    zeros = jnp.zeros((pad, LANES), F32)
    xs_scr[0:pad, :] = zeros
    xs_scr[ctx_len + pad:ctx_len + 2 * pad, :] = zeros
    xs_scr[t + 2 * pad:t + 3 * pad, :] = zeros

    def seg_row(ci):
        return ci * CH + pad + jnp.where(ci * CH >= ctx_len, pad, 0)

    def prep_stream(src_ref, w_ref, dst_scr, l2, scale):
        def load(ci, carry):
            rows = pl.ds(pl.multiple_of(ci * CH, CH), CH)
            xs_scr[pl.ds(pl.multiple_of(seg_row(ci), 8), CH), :] = src_ref[rows, :].astype(F32)
            return carry

        lax.fori_loop(0, t // CH, load, 0)
        w = w_ref[...]

        def conv(ci, carry):
            base = seg_row(ci) - CONV_K // 2
            acc = xs_scr[pl.ds(base, CH), :] * w[0:1, :]
            for j in range(1, CONV_K):
                acc = acc + xs_scr[pl.ds(base + j, CH), :] * w[j:j + 1, :]
            acc = acc * _sigmoid(acc)
            if l2:
                acc = acc * lax.rsqrt(jnp.sum(acc * acc, axis=-1, keepdims=True) + EPS) * scale
            dst_scr[pl.ds(pl.multiple_of(ci * CH, CH), CH), :] = acc
            return carry

        lax.fori_loop(0, t // CH, conv, 0)

    prep_stream(q_ref, wq_ref, qd_scr, True, SCALE)
    prep_stream(k_ref, wk_ref, kd_scr, True, 1.0)
    prep_stream(v_ref, wv_ref, vd_scr, False, 1.0)

    wp = PACK * DC
    r = lax.broadcasted_iota(jnp.int32, (DC, wp), 0)
    lane = lax.broadcasted_iota(jnp.int32, (DC, wp), 1)
    blk = lane // DC
    cl = lane % DC
    fwd = (blk % 2) == 0
    ahead = jnp.where(fwd, r - cl, cl - r)
    incl = ahead >= 0
    strict = ahead > 0
    lane_masks = [(blk == i).astype(BF16) for i in range(PACK)]

    def pick(vals):
        out = vals[PACK - 1]
        for i in range(PACK - 2, -1, -1):
            out = jnp.where(blk == i, vals[i], out)
        return out

    def pre(first_pair, n_groups):
        groups = []
        n_mats = []
        for gg in range(n_groups):
            pair = first_pair + gg
            at_blk = at_ref[pair]
            g_row = jnp.where(fwd[0:1, :], _sub_row(at_blk, 16 + h), _sub_row(at_blk, 20 + h))
            kks, qks, g_cols, betas, chains = [], [], [], [], []
            for cc in range(2):
                ci = pair * 2 + cc
                rows = pl.ds(pl.multiple_of(ci * DC, DC), DC)
                q = qd_scr[rows, :]
                k = kd_scr[rows, :]
                v = vd_scr[rows, :]
                a_blk = a_ref[rows, :]
                kb = k.astype(BF16)
                kb2 = jnp.concatenate([kb, kb], axis=0)
                kks.append(_dot_nt(kb, kb2))
                qks.append(_dot_nt(q.astype(BF16), kb2))
                for d in range(2):
                    last = DC - 1 if d == 0 else 0
                    g_col = _lane_col(a_blk, 16 + d * 4 + h)
                    beta = _lane_col(a_blk, 24 + d * 4 + h)
                    eg = jnp.exp(g_col)
                    g_end = g_col[last:last + 1, :]
                    rhs = jnp.concatenate([beta * v, (beta * eg) * k], axis=1).astype(BF16)
                    ke = (jnp.exp(g_end - g_col) * k).astype(BF16)
                    ge_scr[d, ci] = jnp.broadcast_to(jnp.exp(g_end), (8, LANES))
                    g_cols.append(g_col)
                    betas.append(beta)
                    chains.append((d, ci, rows, rhs, ke, eg * q))
            decay = jnp.exp(jnp.where(incl, pick(g_cols) - g_row, NEG))
            n_mats.append(jnp.where(strict, pick(betas) * decay * jnp.concatenate(kks, axis=1), 0.0))
            qkd = (jnp.concatenate(qks, axis=1) * decay).astype(BF16)
            groups.append((chains, qkd, jnp.concatenate([ch[3] for ch in chains], axis=0)))
        t_invs = [t_inv.astype(BF16) for t_inv in _tri_inverse(n_mats, r, cl, lane_masks)]
        ws = [[_dot(t_inv * mk, rhs_all).astype(BF16) for mk in lane_masks]
              for t_inv, (_, _, rhs_all) in zip(t_invs, groups)]
        w_all = [jnp.concatenate(w4, axis=0) for w4 in ws]
        kws = [[_dot_tn(ch[4], w) for ch, w in zip(chains, w4)] for (chains, _, _), w4 in zip(groups, ws)]
        qws = [[_dot(qkd * mk, wa) for mk in lane_masks] for (_, qkd, _), wa in zip(groups, w_all)]
        for (chains, _, _), kw4, qw4 in zip(groups, kws, qws):
            for (d, ci, rows, _, _, egq), kw, qw in zip(chains, kw4, qw4):
                aq_scr[d, ci, 0:HEAD_DIM, :] = (-kw[:, LANES:]).astype(BF16)
                aq_scr[d, ci, HEAD_DIM:HEAD_DIM + DC, :] = (egq - qw[:, LANES:]).astype(BF16)
                b_scr[d, ci] = kw[:, :LANES]
                if d == 0:
                    y_ref[rows, :] = qw[:, :LANES]
                else:
                    y1_scr[rows, :] = qw[:, :LANES]

    n_pairs = n_chunks // 2
    n_trips, tail = divmod(n_pairs, PRE_GROUPS)

    def pre_trip(i, carry):
        pre(i * PRE_GROUPS, PRE_GROUPS)
        return carry

    lax.fori_loop(0, n_trips, pre_trip, 0)
    if tail:
        pre(n_trips * PRE_GROUPS, tail)

    def scan(j, carry):
        cis = [_chunk_order(j, d, n_chunks, n_ctx_chunks) for d in range(2)]
        xs = [_dot(aq_scr[d, cis[d]], carry[d].astype(BF16)) for d in range(2)]
        new = tuple(ge_scr[d, cis[d]][0:1, :] * carry[d] + (xs[d][:HEAD_DIM, :] + b_scr[d, cis[d]])
                    for d in range(2))
        rows = [pl.ds(pl.multiple_of(cis[d] * DC, DC), DC) for d in range(2)]
        y_ref[rows[0], :] = y_ref[rows[0], :] + xs[0][HEAD_DIM:, :]
        y1_scr[rows[1], :] = y1_scr[rows[1], :] + xs[1][HEAD_DIM:, :]
        return new

    zero = jnp.zeros((HEAD_DIM, HEAD_DIM), F32)
    lax.fori_loop(0, n_chunks, scan, (zero, zero))

    def add(ci, carry):
        rows = pl.ds(pl.multiple_of(ci * CH, CH), CH)
        y_ref[rows, :] = y_ref[rows, :] + y1_scr[rows, :]
        return carry

    lax.fori_loop(0, t // CH, add, 0)


def _deltanet(u, conv_w, a, at_dc, ctx_len):
    bsz, t, _ = u.shape
    n_chunks = t // DC
    kern = functools.partial(_deltanet_kernel, t=t, ctx_len=ctx_len)

    def wspec(off):
        return pl.BlockSpec((CONV_K, LANES), lambda b, h: (0, off + h))

    return pl.pallas_call(
        kern,
        grid=(bsz, N_HEADS),
        in_specs=[_head_spec(t, "d_q"), _head_spec(t, "d_k"), _head_spec(t, "d_v"),
                  wspec(0), wspec(N_HEADS), wspec(2 * N_HEADS),
                  pl.BlockSpec((None, t, LANES), lambda b, h: (b, 0, 0)),
                  pl.BlockSpec((None, n_chunks // 2, 32, PACK * DC), lambda b, h: (b, 0, 0, 0))],
        out_specs=pl.BlockSpec((None, t, LANES), lambda b, h: (b, 0, h)),
        out_shape=jax.ShapeDtypeStruct((bsz, t, GROUP_W), F32),
        scratch_shapes=[pltpu.VMEM((t + 24, LANES), F32),
                        pltpu.VMEM((t, LANES), F32), pltpu.VMEM((t, LANES), F32), pltpu.VMEM((t, LANES), F32),
                        pltpu.VMEM((2, n_chunks, HEAD_DIM + DC, LANES), BF16),
                        pltpu.VMEM((2, n_chunks, HEAD_DIM, LANES), F32),
                        pltpu.VMEM((2, n_chunks, 8, LANES), F32),
                        pltpu.VMEM((t, LANES), F32)],
        compiler_params=_params(("parallel", "parallel")),
        name="deltanet",
    )(u, u, u, conv_w, conv_w, conv_w, a, at_dc)


def _out_kernel(x_ref, ym_ref, yr_ref, ya_ref, yd_ref, zm_ref, zr_ref, za_ref, zd_ref, hn_ref, w_ref,
                mod_ref, pw_ref, o_ref, *, first_block):
    b = pl.program_id(0)
    i = pl.program_id(1) + first_block

    def silu(z_ref):
        z = z_ref[...].astype(F32)
        return z * _sigmoid(z)

    def head_rms(y_ref, g):
        y = y_ref[...]
        parts = []
        for hh in range(N_HEADS):
            p = y[:, hh * LANES:(hh + 1) * LANES]
            parts.append(p * lax.rsqrt(jnp.mean(p * p, axis=-1, keepdims=True) + EPS))
        return jnp.concatenate(parts, axis=1) * hn_ref[:, g * GROUP_W:(g + 1) * GROUP_W]

    merged = jnp.concatenate([
        (head_rms(ym_ref, 0) * silu(zm_ref)).astype(BF16),
        (head_rms(yr_ref, 1) * silu(zr_ref)).astype(BF16),
        (ya_ref[...] * silu(za_ref)).astype(BF16),
        (head_rms(yd_ref, 2) * silu(zd_ref)).astype(BF16)], axis=1)
    o = _dot(merged, w_ref[...])
    o = o * lax.rsqrt(jnp.mean(o * o, axis=-1, keepdims=True) + EPS) * pw_ref[...]
    row = jnp.where(i == 0, 4, b)
    gate = mod_ref[pl.ds(row, 1), 2 * D_MODEL:3 * D_MODEL]
    o_ref[...] = x_ref[...] + gate * o


def _out_projection(xa, ys, u, hn_w, w_out, layer, mod, post_w, ctx_len, latent_only):
    bsz, t, d = xa.shape
    assert ctx_len == CH
    first = 1 if latent_only else 0
    tok = lambda b, i: (b, i + first, 0)
    yspec = pl.BlockSpec((None, CH, GROUP_W), tok)

    def zspec(name):
        blk = CB[name] // N_HEADS
        return pl.BlockSpec((None, CH, GROUP_W), lambda b, i: (b, i + first, blk))

    return pl.pallas_call(
        functools.partial(_out_kernel, first_block=first),
        grid=(bsz, t // CH - first),
        in_specs=[pl.BlockSpec((None, CH, d), tok), yspec, yspec, yspec, yspec,
                  zspec("m_z"), zspec("r_z"), zspec("a_z"), zspec("d_z"),
                  pl.BlockSpec((1, 3 * GROUP_W), lambda b, i: (0, 0)),
                  pl.BlockSpec((None, d, d), lambda b, i: (layer, 0, 0)),
                  pl.BlockSpec((8, 3 * d), lambda b, i: (0, 0)),
                  pl.BlockSpec((1, d), lambda b, i: (0, 0))],
        out_specs=pl.BlockSpec((None, CH, d), lambda b, i: (b, i, 0)),
        out_shape=jax.ShapeDtypeStruct((bsz, t - first * CH, d), F32),
        compiler_params=_params(("parallel", "parallel")),
        name="out_proj",
    )(xa, *ys, u, u, u, u, hn_w.reshape(1, 3 * GROUP_W), w_out, mod, post_w.reshape(1, d))


M_GATE_COL = 5 * GROUP_W
N_M_GATES = 4 * N_HEADS
D_GATE_COL = N_MAIN + N_M_GATES
W_TILE = 512


def _wprep_kernel(a_ref, b_ref, o_ref):
    n = pl.program_id(1)

    @pl.when(n < M_GATE_COL // W_TILE)
    def _():
        o_ref[...] = a_ref[...]

    @pl.when(n >= M_GATE_COL // W_TILE)
    def _():
        o_ref[...] = jnp.concatenate([a_ref[:, N_M_GATES:], b_ref[:, :N_M_GATES]], axis=1)


def _reorder_w_in(w_in):
    depth, d, _ = w_in.shape
    assert M_GATE_COL % W_TILE == 0
    wb = w_in.astype(BF16)
    w_main = pl.pallas_call(
        _wprep_kernel,
        grid=(depth, N_MAIN // W_TILE),
        in_specs=[pl.BlockSpec((None, d, W_TILE), lambda l, n: (l, 0, n)),
                  pl.BlockSpec((None, d, LANES), lambda l, n: (l, 0, (n + 1) * (W_TILE // LANES)))],
        out_specs=pl.BlockSpec((None, d, W_TILE), lambda l, n: (l, 0, n)),
        out_shape=jax.ShapeDtypeStruct((depth, d, N_MAIN), BF16),
        compiler_params=_params(("parallel", "parallel")),
        name="w_in_prep",
    )(wb, wb)
    gates = jnp.concatenate([wb[:, :, M_GATE_COL:M_GATE_COL + N_M_GATES], wb[:, :, D_GATE_COL:]], axis=2)
    w_gate = jnp.pad(gates, ((0, 0), (0, 0), (0, LANES - gates.shape[2])))
    return w_main, w_gate


def _rope_tables(seq, ctx_len):
    rows = seq // GRID_W
    row = jnp.repeat(jnp.arange(rows), GRID_W)
    col = jnp.tile(jnp.arange(GRID_W), rows)
    n_freq = HEAD_DIM // 4
    inv_freq = ROPE_BASE ** (-jnp.arange(n_freq, dtype=F32) / n_freq)
    ar = row[:, None] * inv_freq
    ac = col[:, None] * inv_freq
    cos = jnp.concatenate([jnp.cos(ar), jnp.cos(ar), jnp.cos(ac), jnp.cos(ac)], axis=1)
    sin = jnp.concatenate([-jnp.sin(ar), jnp.sin(ar), -jnp.sin(ac), jnp.sin(ac)], axis=1)
    cos = jnp.concatenate([jnp.ones((ctx_len, HEAD_DIM), F32), cos], axis=0)
    sin = jnp.concatenate([jnp.zeros((ctx_len, HEAD_DIM), F32), sin], axis=0)
    return cos, sin


def kernel(x, c, ctx, c_ctx, ada_w, ada_b, pre_norm_w, post_norm_w, w_in, w_out, mlstm_i_bias, mlstm_f_bias,
           ret_log_gamma, attn_q_norm_w, attn_k_norm_w, dn_conv_w, dn_a_log, dn_dt_bias, head_norm_w):
    bsz, seq, d = x.shape
    ctx_len = ctx.shape[1]
    t = ctx_len + seq
    depth = ada_w.shape[0]
    assert bsz <= 4 and ctx_len % CH == 0 and seq % CH == 0

    xa = jnp.concatenate([ctx, x], axis=1)
    c8 = jnp.zeros((8, d), F32).at[:bsz].set(c).at[4].set(c_ctx)
    mod = _modulation(c8, ada_w, ada_b)
    cos_t, sin_t = _rope_tables(seq, ctx_len)

    w_main, w_gate = _reorder_w_in(w_in)
    w_out_b = w_out.astype(BF16)

    for l in range(depth):
        u, g = _in_projection(xa, mod[l], pre_norm_w[l], w_main, w_gate, l, ctx_len)
        zeros8 = jnp.zeros((8,), F32)
        bias_row = jnp.pad(jnp.concatenate([mlstm_i_bias[l].reshape(-1), mlstm_f_bias[l].reshape(-1),
                                            dn_dt_bias[l].reshape(-1), zeros8]), (0, LANES - 32))
        alog_row = jnp.pad(jnp.concatenate([zeros8, zeros8, dn_a_log[l].reshape(-1), zeros8]), (0, LANES - 32))
        a, at_ch, at_dc = _gate_prep(g, bias_row.reshape(1, LANES), alog_row.reshape(1, LANES))
        y_m = _mlstm(u, a, at_ch, ctx_len)
        y_r = _retention(u, ret_log_gamma[l], cos_t, sin_t, ctx_len)
        y_a = _attention(u, cos_t, sin_t, attn_q_norm_w[l], attn_k_norm_w[l], ctx_len, with_ctx=(l < depth - 1))
        y_d = _deltanet(u, dn_conv_w[l], a, at_dc, ctx_len)
        xa = _out_projection(xa, (y_m, y_r, y_a, y_d), u, head_norm_w[l], w_out_b, l, mod[l],
                             post_norm_w[l], ctx_len, latent_only=(l == depth - 1))
    return xa
```

```python
import functools

import jax
import jax.numpy as jnp
from jax import lax
from jax.experimental import pallas as pl
from jax.experimental.pallas import tpu as pltpu

F32 = jnp.float32
BF16 = jnp.bfloat16
HI = lax.Precision.HIGHEST

D_MODEL = 2048
GROUP_W = 512
N_HEADS = 4
HEAD_DIM = 128
KV_HEADS = 2
GRID_W = 64
CONV_K = 5
ROPE_BASE = 10000.0
EPS = 1e-6
SCALE = HEAD_DIM ** -0.5
LOG2E = 1.4426950408889634
NEG = -1e30

LANES = 128
CH = 256
DC = 64
PACK = 4
PRE_GROUPS = 4
N_MAIN = 64 * LANES
VMEM_LIMIT = 56 * 1024 * 1024
IN_PROJ_MAX_TM = 1088

CB = dict(m_q=0, m_k=4, m_v=8, m_o=12, m_z=16, r_q=20, r_k=24, r_v=28, r_z=32,
          a_q=36, a_k=40, a_v=42, a_z=44, d_q=48, d_k=52, d_v=56, d_z=60)


def _dot(a, b, prec=None):
    return jnp.dot(a, b, preferred_element_type=F32, precision=prec)


def _dot_nt(a, b, prec=None):
    return lax.dot_general(a, b, (((1,), (1,)), ((), ())), preferred_element_type=F32, precision=prec)


def _dot_tn(a, b, prec=None):
    return lax.dot_general(a, b, (((0,), (0,)), ((), ())), preferred_element_type=F32, precision=prec)


def _split(a):
    hi = a.astype(BF16)
    return hi, (a - hi.astype(F32)).astype(BF16)


def _dot3(a, b):
    n = a[0].shape[0]
    x = _dot(jnp.concatenate([a[0], a[1]], axis=0), b[0])
    return (x[:n] + x[n:]) + _dot(a[0], b[1])


def _sigmoid(x):
    return 1.0 / (1.0 + jnp.exp(-x))


def _lane_col(blk, idx):
    lane = lax.broadcasted_iota(jnp.int32, blk.shape, 1)
    return jnp.sum(jnp.where(lane == idx, blk, 0.0), axis=1, keepdims=True)


def _sub_row(blk, idx):
    sub = lax.broadcasted_iota(jnp.int32, blk.shape, 0)
    return jnp.sum(jnp.where(sub == idx, blk, 0.0), axis=0, keepdims=True)


def _rope(t, cos, sin_signed):
    lane = lax.broadcasted_iota(jnp.int32, t.shape, 1)
    partner = jnp.where((lane // 32) % 2 == 0, pltpu.roll(t, 96, 1), pltpu.roll(t, 32, 1))
    return t * cos + partner * sin_signed


def _params(sem):
    return pltpu.CompilerParams(dimension_semantics=sem, vmem_limit_bytes=VMEM_LIMIT)


def _mod_kernel(c_ref, w_ref, b_ref, o_ref):
    c = c_ref[...]
    o_ref[...] = _dot3(_split(c * _sigmoid(c)), _split(w_ref[...])) + b_ref[...]


def _modulation(c8, ada_w, ada_b):
    depth, d, n3 = ada_w.shape
    tn = 768
    return pl.pallas_call(
        _mod_kernel,
        grid=(depth, n3 // tn),
        in_specs=[pl.BlockSpec((8, d), lambda l, n: (0, 0)),
                  pl.BlockSpec((None, d, tn), lambda l, n: (l, 0, n)),
                  pl.BlockSpec((None, 1, tn), lambda l, n: (l, 0, n))],
        out_specs=pl.BlockSpec((None, 8, tn), lambda l, n: (l, 0, n)),
        out_shape=jax.ShapeDtypeStruct((depth, 8, n3), F32),
        compiler_params=_params(("parallel", "parallel")),
        name="adaln_mod",
    )(c8, ada_w, ada_b.reshape(depth, 1, n3))


def _inproj_kernel(x_ref, mod_ref, pw_ref, wm_ref, wg_ref, u_ref, g_ref, h_scr, *, tm, ctx_len, per_batch,
                   n_blocks, n_tiles):
    m = pl.program_id(0)
    n = pl.program_id(1)

    d = D_MODEL

    def modulated(x, row):
        y = x * lax.rsqrt(jnp.mean(x * x, axis=-1, keepdims=True) + EPS) * pw_ref[...]
        return (y * (1.0 + mod_ref[pl.ds(row, 1), d:2 * d]) + mod_ref[pl.ds(row, 1), 0:d]).astype(BF16)

    def normalise(blk, slot):
        b = blk // per_batch
        head = modulated(x_ref[0:ctx_len, :], jnp.where(blk % per_batch == 0, 4, b))
        hh = jnp.concatenate([head, modulated(x_ref[ctx_len:, :], b)], axis=0)
        h_scr[slot] = hh
        g_ref[...] = _dot(hh, wg_ref[...])

    @pl.when((m == 0) & (n == 0))
    def _():
        normalise(0, 0)

    @pl.when(n < n_tiles - 1)
    def _():
        u_ref[...] = _dot(h_scr[m % 2], wm_ref[...]).astype(BF16)

    @pl.when(n == n_tiles - 1)
    def _():
        u_ref[...] = _dot(h_scr[m % 2], wm_ref[...]).astype(BF16)
        normalise(jnp.minimum(m + 1, n_blocks - 1), (m + 1) % 2)


def _in_projection(xa, mod, pre_w, w_main, w_gate, layer, ctx_len):
    bsz, t, d = xa.shape
    tm = max(c for c in range(16, IN_PROJ_MAX_TM + 1, 16) if t % c == 0)
    assert tm > ctx_len
    tn = 1024
    per_batch = t // tm
    n_blocks = bsz * per_batch
    n_tiles = N_MAIN // tn
    kern = functools.partial(_inproj_kernel, tm=tm, ctx_len=ctx_len, per_batch=per_batch, n_blocks=n_blocks,
                             n_tiles=n_tiles)
    nxt = lambda m: jnp.minimum(m + 1, n_blocks - 1)
    u, g = pl.pallas_call(
        kern,
        grid=(n_blocks, n_tiles),
        in_specs=[pl.BlockSpec((None, tm, d), lambda m, n: (jnp.where((m == 0) & (n == 0), 0, nxt(m)), 0, 0)),
                  pl.BlockSpec((8, 3 * d), lambda m, n: (0, 0)),
                  pl.BlockSpec((1, d), lambda m, n: (0, 0)),
                  pl.BlockSpec((None, d, tn), lambda m, n: (layer, 0, n)),
                  pl.BlockSpec((None, d, LANES), lambda m, n: (layer, 0, 0))],
        out_specs=[pl.BlockSpec((None, tm, tn), lambda m, n: (m, 0, n)),
                   pl.BlockSpec((None, tm, LANES), lambda m, n: (jnp.where(n == n_tiles - 1, nxt(m), m), 0, 0))],
        out_shape=[jax.ShapeDtypeStruct((n_blocks, tm, N_MAIN), BF16),
                   jax.ShapeDtypeStruct((n_blocks, tm, LANES), F32)],
        scratch_shapes=[pltpu.VMEM((2, tm, d), BF16)],
        compiler_params=_params(("arbitrary", "arbitrary")),
        name="in_proj",
    )(xa.reshape(n_blocks, tm, d), mod, pre_w.reshape(1, d), w_main, w_gate)
    return u.reshape(bsz, t, N_MAIN), g.reshape(bsz, t, LANES)


def _gate_kernel(g_ref, bias_ref, alog_ref, a_ref, at_ref, atp_ref, *, n_chunks):
    lane = lax.broadcasted_iota(jnp.int32, (CH, LANES), 1)
    s_idx = lax.broadcasted_iota(jnp.int32, (CH, 4 * CH), 0)
    col = lax.broadcasted_iota(jnp.int32, (CH, 4 * CH), 1)
    kind = col // CH
    t_idx = col % CH
    ahead = jnp.where(kind % 2 == 0, t_idx - s_idx, s_idx - t_idx)
    other_sub = jnp.where(kind < 2, 0, jnp.abs(t_idx // DC - s_idx // DC))
    sums = jnp.where(ahead - CH * other_sub >= 0, 1.0, 0.0).astype(BF16)
    gate = lax.broadcasted_iota(jnp.int32, (32, CH), 0)
    fwd = (gate % 8) < 4
    bias = bias_ref[...]
    neg_ea = -jnp.exp(alog_ref[...])

    def chunk(ci, carry):
        rows = pl.ds(pl.multiple_of(ci * CH, CH), CH)
        x = g_ref[rows, :] + bias
        l1p = jnp.log(1.0 + jnp.exp(-jnp.abs(x)))
        log_f = jnp.minimum(x, 0.0) - l1p
        log_a = neg_ea * (jnp.maximum(x, 0.0) + l1p)
        v = jnp.where(lane < 8, x, jnp.where(lane < 16, log_f, jnp.where(lane < 24, log_a, _sigmoid(x))))
        v_t = v.T[0:32, :]
        v1 = v_t.astype(BF16)
        res1 = v_t - v1.astype(F32)
        v2 = res1.astype(BF16)
        v3 = (res1 - v2.astype(F32)).astype(BF16)
        cums = _dot(jnp.concatenate([v1, v2, v3], axis=0), sums)
        cums = cums[0:32] + (cums[32:64] + cums[64:96])
        cum_ch = jnp.where(fwd, cums[:, 0:CH], cums[:, CH:2 * CH])
        cum_dc = jnp.where(fwd, cums[:, 2 * CH:3 * CH], cums[:, 3 * CH:])
        a_t = jnp.where((gate >= 8) & (gate < 16), cum_ch, jnp.where((gate >= 16) & (gate < 24), cum_dc, v_t))
        a_ref[rows, :] = jnp.concatenate([a_t, jnp.zeros((LANES - 32, CH), F32)], axis=0).T
        at_ref[ci] = a_t
        pairs = CH // (2 * DC)
        for p in range(pairs):
            c0 = a_t[:, 2 * p * DC:(2 * p + 1) * DC]
            c1 = a_t[:, (2 * p + 1) * DC:(2 * p + 2) * DC]
            atp_ref[ci * pairs + p] = jnp.concatenate([c0, c0, c1, c1], axis=1)
        return carry

    lax.fori_loop(0, n_chunks, chunk, 0)


def _gate_prep(g, bias_row, alog_row):
    bsz, t, _ = g.shape
    n_chunks = t // CH
    return pl.pallas_call(
        functools.partial(_gate_kernel, n_chunks=n_chunks),
        grid=(bsz,),
        in_specs=[pl.BlockSpec((None, t, LANES), lambda b: (b, 0, 0)),
                  pl.BlockSpec((1, LANES), lambda b: (0, 0)),
                  pl.BlockSpec((1, LANES), lambda b: (0, 0))],
        out_specs=[pl.BlockSpec((None, t, LANES), lambda b: (b, 0, 0)),
                   pl.BlockSpec((None, n_chunks, 32, CH), lambda b: (b, 0, 0, 0)),
                   pl.BlockSpec((None, t // (2 * DC), 32, PACK * DC), lambda b: (b, 0, 0, 0))],
        out_shape=[jax.ShapeDtypeStruct((bsz, t, LANES), F32),
                   jax.ShapeDtypeStruct((bsz, n_chunks, 32, CH), F32),
                   jax.ShapeDtypeStruct((bsz, t // (2 * DC), 32, PACK * DC), F32)],
        compiler_params=_params(("parallel",)),
        name="gate_prep",
    )(g, bias_row, alog_row)


def _chunk_order(j, d, n_chunks, n_ctx_chunks):
    if d == 0:
        return j
    return jnp.where(j < n_ctx_chunks, n_ctx_chunks - 1 - j, n_chunks + n_ctx_chunks - 1 - j)


def _head_spec(t, name):
    base = CB[name]
    return pl.BlockSpec((None, t, LANES), lambda b, h: (b, 0, base + h))


def _mlstm_kernel(q_ref, k_ref, v_ref, o_ref, a_ref, at_ref, y_ref, c_scr, y1_scr, *, n_chunks, n_ctx_chunks):
    h = pl.program_id(1)
    r = lax.broadcasted_iota(jnp.int32, (CH, CH), 0)
    c = lax.broadcasted_iota(jnp.int32, (CH, CH), 1)
    one_col = (lax.broadcasted_iota(jnp.int32, (CH, LANES), 1) == 0).astype(F32)

    masks = ((c <= r), (c >= r))
    lasts = (CH - 1, 0)
    dirs = range(2)
    c_scr[...] = jnp.zeros_like(c_scr)

    def step(j, ms):
        cis = [_chunk_order(j, d, n_chunks, n_ctx_chunks) for d in dirs]
        rows = [pl.ds(pl.multiple_of(ci * CH, CH), CH) for ci in cis]
        qs = [q_ref[rw, :] for rw in rows]
        ks = [k_ref[rw, :] for rw in rows]
        qk = [_dot_nt(qs[d], ks[d]) for d in dirs]
        pc = [_dot(qs[d], c_scr[d].astype(BF16)) for d in dirs]
        a_blk = [a_ref[rw, :] for rw in rows]
        at_blk = [at_ref[ci] for ci in cis]
        li_col = [_lane_col(a_blk[d], d * 4 + h) for d in dirs]
        b_col = [_lane_col(a_blk[d], 8 + d * 4 + h) for d in dirs]
        li_row = [_sub_row(at_blk[d], d * 4 + h) for d in dirs]
        b_row = [_sub_row(at_blk[d], 8 + d * 4 + h) for d in dirs]
        dm = [jnp.where(masks[d], b_col[d] - b_row[d] + li_row[d], NEG) for d in dirs]
        inter = [ms[d] + b_col[d] for d in dirs]
        m_t = [jnp.maximum(inter[d], jnp.max(dm[d], axis=1, keepdims=True)) for d in dirs]
        s = [(qk[d] * (jnp.exp(dm[d] - m_t[d]) * SCALE)).astype(BF16) for d in dirs]
        v_aug = [jnp.concatenate([v_ref[rw, :].astype(F32), one_col], axis=1) for rw in rows]
        sv = [_dot(s[d], v_aug[d].astype(BF16)) for d in dirs]

        b_end = [b_col[d][lasts[d]:lasts[d] + 1, :] for d in dirs]
        dec = [b_end[d] - b_col[d] + li_col[d] for d in dirs]
        m_new = [jnp.maximum(ms[d] + b_end[d], jnp.max(dec[d], axis=0, keepdims=True)) for d in dirs]
        wv = [(jnp.exp(dec[d] - m_new[d]) * v_aug[d]).astype(BF16) for d in dirs]
        upd = [_dot_tn(ks[d], wv[d]) for d in dirs]
        for d in dirs:
            c_scr[d] = jnp.exp(ms[d] + b_end[d] - m_new[d]) * c_scr[d] + upd[d]

        for d in dirs:
            nd = (jnp.exp(inter[d] - m_t[d]) * SCALE) * pc[d] + sv[d]
            h_out = nd[:, :LANES] / jnp.maximum(jnp.abs(nd[:, LANES:LANES + 1]), jnp.exp(-m_t[d]))
            if d == 0:
                y_ref[rows[d], :] = h_out
            else:
                y1_scr[rows[d], :] = h_out
        return tuple(m_new)

    zero = jnp.zeros((1, 1), F32)
    lax.fori_loop(0, n_chunks, step, (zero, zero))

    def finish(ci, carry):
        rw = pl.ds(pl.multiple_of(ci * CH, CH), CH)
        y_ref[rw, :] = (y_ref[rw, :] + y1_scr[rw, :]) * _sigmoid(o_ref[rw, :].astype(F32))
        return carry

    lax.fori_loop(0, n_chunks, finish, 0)


def _mlstm(u, a, at_ch, ctx_len):
    bsz, t, _ = u.shape
    n_chunks = t // CH
    kern = functools.partial(_mlstm_kernel, n_chunks=n_chunks, n_ctx_chunks=ctx_len // CH)
    return pl.pallas_call(
        kern,
        grid=(bsz, N_HEADS),
        in_specs=[_head_spec(t, "m_q"), _head_spec(t, "m_k"), _head_spec(t, "m_v"), _head_spec(t, "m_o"),
                  pl.BlockSpec((None, t, LANES), lambda b, h: (b, 0, 0)),
                  pl.BlockSpec((None, n_chunks, 32, CH), lambda b, h: (b, 0, 0, 0))],
        out_specs=pl.BlockSpec((None, t, LANES), lambda b, h: (b, 0, h)),
        out_shape=jax.ShapeDtypeStruct((bsz, t, GROUP_W), F32),
        scratch_shapes=[pltpu.VMEM((2, HEAD_DIM, 2 * LANES), F32), pltpu.VMEM((t, LANES), F32)],
        compiler_params=_params(("parallel", "parallel")),
        name="mlstm",
    )(u, u, u, u, a, at_ch)


def _retention_kernel(lg_ref, q_ref, k_ref, v_ref, cos_ref, sin_ref, y_ref, qp_scr, kp_scr, r_scr, y1_scr, *,
                      n_chunks, n_ctx_chunks):
    h = pl.program_id(1)

    def prep(ci, carry):
        rows = pl.ds(pl.multiple_of(ci * CH, CH), CH)
        cos = cos_ref[rows, :]
        sin = sin_ref[rows, :]
        qp_scr[rows, :] = _rope(q_ref[rows, :].astype(F32), cos, sin).astype(BF16)
        kp_scr[rows, :] = (_rope(k_ref[rows, :].astype(F32), cos, sin) * SCALE).astype(BF16)
        return carry

    lax.fori_loop(0, n_chunks, prep, 0)

    r = lax.broadcasted_iota(jnp.int32, (CH, CH), 0).astype(F32)
    c = lax.broadcasted_iota(jnp.int32, (CH, CH), 1).astype(F32)
    pos = lax.broadcasted_iota(jnp.int32, (CH, 1), 0).astype(F32)

    dirs = range(2)
    decay, q_decay, k_decay, chunk_decay = [], [], [], []
    for d in dirs:
        lg = lg_ref[d * N_HEADS + h]
        rel = (r - c) if d == 0 else (c - r)
        decay.append(jnp.where(rel >= 0, jnp.exp(lg * jnp.maximum(rel, 0.0)), 0.0))
        p_vis = pos if d == 0 else (CH - 1.0) - pos
        q_decay.append(jnp.exp(lg * (p_vis + 1.0)))
        k_decay.append(jnp.exp(lg * ((CH - 1.0) - p_vis)))
        chunk_decay.append(jnp.exp(lg * CH))
    r_scr[...] = jnp.zeros_like(r_scr)

    def step(j, carry):
        cis = [_chunk_order(j, d, n_chunks, n_ctx_chunks) for d in dirs]
        rows = [pl.ds(pl.multiple_of(ci * CH, CH), CH) for ci in cis]
        qs = [qp_scr[rw, :] for rw in rows]
        ks = [kp_scr[rw, :] for rw in rows]
        vs = [v_ref[rw, :] for rw in rows]
        qk = [_dot_nt(qs[d], ks[d]) for d in dirs]
        qr = [_dot(qs[d], r_scr[d].astype(BF16)) for d in dirs]
        sv = [_dot((qk[d] * decay[d]).astype(BF16), vs[d]) for d in dirs]
        upd = [_dot_tn((ks[d].astype(F32) * k_decay[d]).astype(BF16), vs[d]) for d in dirs]
        for d in dirs:
            r_scr[d] = chunk_decay[d] * r_scr[d] + upd[d]
        y_ref[rows[0], :] = sv[0] + q_decay[0] * qr[0]
        y1_scr[rows[1], :] = sv[1] + q_decay[1] * qr[1]
        return carry

    lax.fori_loop(0, n_chunks, step, 0)

    def finish(ci, carry):
        rw = pl.ds(pl.multiple_of(ci * CH, CH), CH)
        y_ref[rw, :] = y_ref[rw, :] + y1_scr[rw, :]
        return carry

    lax.fori_loop(0, n_chunks, finish, 0)


def _retention(u, log_gamma, cos_t, sin_t, ctx_len):
    bsz, t, _ = u.shape
    n_chunks = t // CH
    kern = functools.partial(_retention_kernel, n_chunks=n_chunks, n_ctx_chunks=ctx_len // CH)
    tab = pl.BlockSpec((t, LANES), lambda b, h: (0, 0))
    return pl.pallas_call(
        kern,
        grid=(bsz, N_HEADS),
        in_specs=[pl.BlockSpec(memory_space=pltpu.SMEM),
                  _head_spec(t, "r_q"), _head_spec(t, "r_k"), _head_spec(t, "r_v"), tab, tab],
        out_specs=pl.BlockSpec((None, t, LANES), lambda b, h: (b, 0, h)),
        out_shape=jax.ShapeDtypeStruct((bsz, t, GROUP_W), F32),
        scratch_shapes=[pltpu.VMEM((t, LANES), BF16), pltpu.VMEM((t, LANES), BF16),
                        pltpu.VMEM((2, HEAD_DIM, HEAD_DIM), F32), pltpu.VMEM((t, LANES), F32)],
        compiler_params=_params(("parallel", "parallel")),
        name="retention",
    )(log_gamma.reshape(2 * N_HEADS), u, u, u, cos_t, sin_t)


def _attn_kernel(q_ref, k_ref, v_ref, cos_ref, sin_ref, cosq_ref, sinq_ref, qw_ref, kw_ref, o_ref,
                 kt_scr, s_scr, m_scr, *, n_blocks, with_ctx):
    j = pl.program_id(2)

    def load_q():
        qs = []
        for g in range(2):
            q = q_ref[:, g * LANES:(g + 1) * LANES].astype(F32)
            q = q * lax.rsqrt(jnp.mean(q * q, axis=-1, keepdims=True) + EPS) * qw_ref[...]
            qs.append((_rope(q, cosq_ref[...], sinq_ref[...]) * (SCALE * LOG2E)).astype(BF16))
        return jnp.concatenate(qs, axis=0)

    def finish(o, l_part):
        o = o / jnp.sum(l_part, axis=-1, keepdims=True)
        o_ref[:, 0:LANES] = o[:CH]
        o_ref[:, LANES:2 * LANES] = o[CH:]

    @pl.when(j == 0)
    def _():
        def prep(ci, carry):
            rows = pl.ds(pl.multiple_of(ci * CH, CH), CH)
            k = k_ref[rows, :].astype(F32)
            k = k * lax.rsqrt(jnp.mean(k * k, axis=-1, keepdims=True) + EPS) * kw_ref[...]
            kt_scr[ci] = _rope(k, cos_ref[rows, :], sin_ref[rows, :]).T.astype(BF16)
            return carry

        lax.fori_loop(0, n_blocks, prep, 0)
        if with_ctx:
            s = _dot(load_q(), kt_scr[0])
            p = jnp.exp2(s - jnp.max(s, axis=-1, keepdims=True))
            finish(_dot(p.astype(BF16), v_ref[0:CH, :]), p[:, :LANES] + p[:, LANES:])
        else:
            o_ref[...] = jnp.zeros_like(o_ref)

    @pl.when(j >= 1)
    def _():
        qq = load_q()
        mx = None
        for kb in range(n_blocks):
            s = _dot(qq, kt_scr[kb])
            s_scr[:, kb * CH:(kb + 1) * CH] = s
            m2 = jnp.maximum(s[:, :LANES], s[:, LANES:])
            mx = m2 if mx is None else jnp.maximum(mx, m2)
        m_scr[...] = jnp.broadcast_to(jnp.max(mx, axis=-1, keepdims=True), (2 * CH, LANES))

    def value_pass(_, carry):
        m = m_scr[...]
        l_acc = o = None
        for kb in range(n_blocks):
            cols = slice(kb * CH, (kb + 1) * CH)
            p_lo = jnp.exp2(s_scr[:, kb * CH:kb * CH + LANES] - m)
            p_hi = jnp.exp2(s_scr[:, kb * CH + LANES:(kb + 1) * CH] - m)
            l_acc = (p_lo + p_hi) if l_acc is None else l_acc + (p_lo + p_hi)
            pv = _dot(jnp.concatenate([p_lo, p_hi], axis=1).astype(BF16), v_ref[cols, :])
            o = pv if o is None else o + pv
        finish(o, l_acc)
        return carry

    lax.fori_loop(0, jnp.minimum(j, 1), value_pass, 0)


def _attention(u, cos_t, sin_t, qn_w, kn_w, ctx_len, with_ctx):
    bsz, t, _ = u.shape
    assert ctx_len == CH
    n_blocks = t // CH
    kern = functools.partial(_attn_kernel, n_blocks=n_blocks, with_ctx=with_ctx)
    qb = CB["a_q"] // 2
    kb = CB["a_k"]
    vb = CB["a_v"]
    tab = pl.BlockSpec((t, LANES), lambda b, kv, j: (0, 0))
    tabq = pl.BlockSpec((CH, LANES), lambda b, kv, j: (j, 0))
    vec = pl.BlockSpec((1, LANES), lambda b, kv, j: (0, 0))
    return pl.pallas_call(
        kern,
        grid=(bsz, KV_HEADS, n_blocks),
        in_specs=[pl.BlockSpec((None, CH, 2 * LANES), lambda b, kv, j: (b, j, qb + kv)),
                  pl.BlockSpec((None, t, LANES), lambda b, kv, j: (b, 0, kb + kv)),
                  pl.BlockSpec((None, t, LANES), lambda b, kv, j: (b, 0, vb + kv)),
                  tab, tab, tabq, tabq, vec, vec],
        out_specs=pl.BlockSpec((None, CH, 2 * LANES), lambda b, kv, j: (b, j, kv)),
        out_shape=jax.ShapeDtypeStruct((bsz, t, GROUP_W), F32),
        scratch_shapes=[pltpu.VMEM((n_blocks, HEAD_DIM, CH), BF16), pltpu.VMEM((2 * CH, t), F32),
                        pltpu.VMEM((2 * CH, LANES), F32)],
        compiler_params=_params(("parallel", "parallel", "arbitrary")),
        name="attention",
    )(u, u, u, cos_t, sin_t, cos_t, sin_t, qn_w.reshape(1, LANES), kn_w.reshape(1, LANES))


def _block_diag(y, lane_masks):
    return jnp.concatenate([y * mk for mk in lane_masks], axis=0)


def _packed_dot3(a, b, lane_masks):
    n = a[0].shape[0]
    x = _dot(jnp.concatenate([a[0], a[1]], axis=0), _block_diag(b[0], lane_masks))
    return (x[:n] + x[n:]) + _dot(a[0], _block_diag(b[1], lane_masks))


def _tri_inverse(n_mats, r, cl, lane_masks):
    eye = (r == cl).astype(F32)
    diag16 = (r // 16) == (cl // 16)
    ms = [-jnp.where(diag16, n, 0.0) for n in n_mats]
    ps = [eye + m for m in ms]
    mps = [_split(m) for m in ms]
    mps = [_split(_packed_dot3(mp, mp, lane_masks)) for mp in mps]
    for _ in range(2):
        both = [_packed_dot3(tuple(jnp.concatenate([a, b], axis=0) for a, b in zip(_split(p), mp)), mp, lane_masks)
                for p, mp in zip(ps, mps)]
        ps = [p + x[:DC] for p, x in zip(ps, both)]
        mps = [_split(x[DC:]) for x in both]
    ps = [p + _packed_dot3(_split(p), mp, lane_masks) for p, mp in zip(ps, mps)]
    for w in (32, 64):
        off = ((r // w) == (cl // w)) & ((r // (w // 2)) != (cl // (w // 2)))
        pbs = [p.astype(BF16) for p in ps]
        pds = [_block_diag(pb, lane_masks) for pb in pbs]
        tmp = [_dot(jnp.where(off, n, 0.0).astype(BF16), pd).astype(BF16) for n, pd in zip(n_mats, pds)]
        ps = [p - _dot(pb, _block_diag(a, lane_masks)) for p, pb, a in zip(ps, pbs, tmp)]
    return ps


def _deltanet_kernel(q_ref, k_ref, v_ref, wq_ref, wk_ref, wv_ref, a_ref, at_ref, y_ref,
                     xs_scr, qd_scr, kd_scr, vd_scr, aq_scr, b_scr, ge_scr, y1_scr, *,
                     t, ctx_len):
    h = pl.program_id(1)
    n_chunks = t // DC
    n_ctx_chunks = ctx_len // DC
    pad = 8

    streams = ((q_ref, wq_ref, qd_scr, True, SCALE), (k_ref, wk_ref, kd_scr, True, 1.0),
               (v_ref, wv_ref, vd_scr, False, 1.0))
    zeros = jnp.zeros((pad, LANES), F32)
    for si in range(len(streams)):
        xs_scr[si, 0:pad, :] = zeros
        xs_scr[si, ctx_len + pad:ctx_len + 2 * pad, :] = zeros
        xs_scr[si, t + 2 * pad:t + 3 * pad, :] = zeros

    def seg_row(ci):
        return ci * CH + pad + jnp.where(ci * CH >= ctx_len, pad, 0)

    def load(ci, carry):
        rows = pl.ds(pl.multiple_of(ci * CH, CH), CH)
        for si, (src_ref, _, _, _, _) in enumerate(streams):
            xs_scr[si, pl.ds(pl.multiple_of(seg_row(ci), 8), CH), :] = src_ref[rows, :].astype(F32)
        return carry

    lax.fori_loop(0, t // CH, load, 0)

    def conv(ci, carry):
        base = seg_row(ci) - CONV_K // 2
        for si, (_, w_ref, dst_scr, l2, scale) in enumerate(streams):
            acc = xs_scr[si, pl.ds(base, CH), :] * w_ref[0:1, :]
            for j in range(1, CONV_K):
                acc = acc + xs_scr[si, pl.ds(base + j, CH), :] * w_ref[j:j + 1, :]
            acc = acc * _sigmoid(acc)
            if l2:
                acc = acc * lax.rsqrt(jnp.sum(acc * acc, axis=-1, keepdims=True) + EPS) * scale
            dst_scr[pl.ds(pl.multiple_of(ci * CH, CH), CH), :] = acc
        return carry

    lax.fori_loop(0, t // CH, conv, 0)

    wp = PACK * DC
    r = lax.broadcasted_iota(jnp.int32, (DC, wp), 0)
    lane = lax.broadcasted_iota(jnp.int32, (DC, wp), 1)
    blk = lane // DC
    cl = lane % DC
    fwd = (blk % 2) == 0
    ahead = jnp.where(fwd, r - cl, cl - r)
    incl = ahead >= 0
    strict = ahead > 0
    lane_masks = [(blk == i).astype(BF16) for i in range(PACK)]

    def pick(vals):
        out = vals[PACK - 1]
        for i in range(PACK - 2, -1, -1):
            out = jnp.where(blk == i, vals[i], out)
        return out

    def pre(first_pair, n_groups):
        groups = []
        n_mats = []
        for gg in range(n_groups):
            pair = first_pair + gg
            at_blk = at_ref[pair]
            g_row = jnp.where(fwd[0:1, :], _sub_row(at_blk, 16 + h), _sub_row(at_blk, 20 + h))
            kks, qks, g_cols, betas, chains = [], [], [], [], []
            for cc in range(2):
                ci = pair * 2 + cc
                rows = pl.ds(pl.multiple_of(ci * DC, DC), DC)
                q = qd_scr[rows, :]
                k = kd_scr[rows, :]
                v = vd_scr[rows, :]
                a_blk = a_ref[rows, :]
                kb = k.astype(BF16)
                kb2 = jnp.concatenate([kb, kb], axis=0)
                kks.append(_dot_nt(kb, kb2))
                qks.append(_dot_nt(q.astype(BF16), kb2))
                for d in range(2):
                    last = DC - 1 if d == 0 else 0
                    g_col = _lane_col(a_blk, 16 + d * 4 + h)
                    beta = _lane_col(a_blk, 24 + d * 4 + h)
                    eg = jnp.exp(g_col)
                    g_end = g_col[last:last + 1, :]
                    rhs = jnp.concatenate([beta * v, (beta * eg) * k], axis=1).astype(BF16)
                    ke = (jnp.exp(g_end - g_col) * k).astype(BF16)
                    ge_scr[d, ci] = jnp.broadcast_to(jnp.exp(g_end), (8, LANES))
                    g_cols.append(g_col)
                    betas.append(beta)
                    chains.append((d, ci, rows, rhs, ke, eg * q))
            decay = jnp.exp(jnp.where(incl, pick(g_cols) - g_row, NEG))
            n_mats.append(jnp.where(strict, pick(betas) * decay * jnp.concatenate(kks, axis=1), 0.0))
            qkd = (jnp.concatenate(qks, axis=1) * decay).astype(BF16)
            groups.append((chains, qkd, jnp.concatenate([ch[3] for ch in chains], axis=0)))
        t_invs = [t_inv.astype(BF16) for t_inv in _tri_inverse(n_mats, r, cl, lane_masks)]
        ws = [[_dot(t_inv * mk, rhs_all).astype(BF16) for mk in lane_masks]
              for t_inv, (_, _, rhs_all) in zip(t_invs, groups)]
        w_all = [jnp.concatenate(w4, axis=0) for w4 in ws]
        kws = [[_dot_tn(ch[4], w) for ch, w in zip(chains, w4)] for (chains, _, _), w4 in zip(groups, ws)]
        qws = [[_dot(qkd * mk, wa) for mk in lane_masks] for (_, qkd, _), wa in zip(groups, w_all)]
        for (chains, _, _), kw4, qw4 in zip(groups, kws, qws):
            for (d, ci, rows, _, _, egq), kw, qw in zip(chains, kw4, qw4):
                aq_scr[d, ci, 0:HEAD_DIM, :] = (-kw[:, LANES:]).astype(BF16)
                aq_scr[d, ci, HEAD_DIM:HEAD_DIM + DC, :] = (egq - qw[:, LANES:]).astype(BF16)
                b_scr[d, ci] = kw[:, :LANES]
                if d == 0:
                    y_ref[rows, :] = qw[:, :LANES]
                else:
                    y1_scr[rows, :] = qw[:, :LANES]

    n_pairs = n_chunks // 2
    n_trips, tail = divmod(n_pairs, PRE_GROUPS)

    def pre_trip(i, carry):
        pre(i * PRE_GROUPS, PRE_GROUPS)
        return carry

    lax.fori_loop(0, n_trips, pre_trip, 0)
    if tail:
        pre(n_trips * PRE_GROUPS, tail)

    def scan(j, carry):
        cis = [_chunk_order(j, d, n_chunks, n_ctx_chunks) for d in range(2)]
        xs = [_dot(aq_scr[d, cis[d]], carry[d].astype(BF16)) for d in range(2)]
        new = tuple(ge_scr[d, cis[d]][0:1, :] * carry[d] + (xs[d][:HEAD_DIM, :] + b_scr[d, cis[d]])
                    for d in range(2))
        rows = [pl.ds(pl.multiple_of(cis[d] * DC, DC), DC) for d in range(2)]
        y_ref[rows[0], :] = y_ref[rows[0], :] + xs[0][HEAD_DIM:, :]
        y1_scr[rows[1], :] = y1_scr[rows[1], :] + xs[1][HEAD_DIM:, :]
        return new

    zero = jnp.zeros((HEAD_DIM, HEAD_DIM), F32)
    lax.fori_loop(0, n_chunks, scan, (zero, zero))

    def add(ci, carry):
        rows = pl.ds(pl.multiple_of(ci * CH, CH), CH)
        y_ref[rows, :] = y_ref[rows, :] + y1_scr[rows, :]
        return carry

    lax.fori_loop(0, t // CH, add, 0)


def _deltanet(u, conv_w, a, at_dc, ctx_len):
    bsz, t, _ = u.shape
    n_chunks = t // DC
    kern = functools.partial(_deltanet_kernel, t=t, ctx_len=ctx_len)

    def wspec(off):
        return pl.BlockSpec((CONV_K, LANES), lambda b, h: (0, off + h))

    return pl.pallas_call(
        kern,
        grid=(bsz, N_HEADS),
        in_specs=[_head_spec(t, "d_q"), _head_spec(t, "d_k"), _head_spec(t, "d_v"),
                  wspec(0), wspec(N_HEADS), wspec(2 * N_HEADS),
                  pl.BlockSpec((None, t, LANES), lambda b, h: (b, 0, 0)),
                  pl.BlockSpec((None, n_chunks // 2, 32, PACK * DC), lambda b, h: (b, 0, 0, 0))],
        out_specs=pl.BlockSpec((None, t, LANES), lambda b, h: (b, 0, h)),
        out_shape=jax.ShapeDtypeStruct((bsz, t, GROUP_W), F32),
        scratch_shapes=[pltpu.VMEM((3, t + 24, LANES), F32),
                        pltpu.VMEM((t, LANES), F32), pltpu.VMEM((t, LANES), F32), pltpu.VMEM((t, LANES), F32),
                        pltpu.VMEM((2, n_chunks, HEAD_DIM + DC, LANES), BF16),
                        pltpu.VMEM((2, n_chunks, HEAD_DIM, LANES), F32),
                        pltpu.VMEM((2, n_chunks, 8, LANES), F32),
                        pltpu.VMEM((t, LANES), F32)],
        compiler_params=_params(("parallel", "parallel")),
        name="deltanet",
    )(u, u, u, conv_w, conv_w, conv_w, a, at_dc)


def _out_kernel(x_ref, ym_ref, yr_ref, ya_ref, yd_ref, zm_ref, zr_ref, za_ref, zd_ref, hn_ref, w_ref,
                mod_ref, pw_ref, o_ref, *, first_block):
    b = pl.program_id(0)
    i = pl.program_id(1) + first_block

    def silu(z_ref):
        z = z_ref[...].astype(F32)
        return z * _sigmoid(z)

    def head_rms(y_ref, g):
        y = y_ref[...]
        parts = []
        for hh in range(N_HEADS):
            p = y[:, hh * LANES:(hh + 1) * LANES]
            parts.append(p * lax.rsqrt(jnp.mean(p * p, axis=-1, keepdims=True) + EPS))
        return jnp.concatenate(parts, axis=1) * hn_ref[:, g * GROUP_W:(g + 1) * GROUP_W]

    merged = jnp.concatenate([
        (head_rms(ym_ref, 0) * silu(zm_ref)).astype(BF16),
        (head_rms(yr_ref, 1) * silu(zr_ref)).astype(BF16),
        (ya_ref[...] * silu(za_ref)).astype(BF16),
        (head_rms(yd_ref, 2) * silu(zd_ref)).astype(BF16)], axis=1)
    o = _dot(merged, w_ref[...])
    o = o * lax.rsqrt(jnp.mean(o * o, axis=-1, keepdims=True) + EPS) * pw_ref[...]
    row = jnp.where(i == 0, 4, b)
    gate = mod_ref[pl.ds(row, 1), 2 * D_MODEL:3 * D_MODEL]
    o_ref[...] = x_ref[...] + gate * o


def _out_projection(xa, ys, u, hn_w, w_out, layer, mod, post_w, ctx_len, latent_only):
    bsz, t, d = xa.shape
    assert ctx_len == CH
    first = 1 if latent_only else 0
    tok = lambda b, i: (b, i + first, 0)
    yspec = pl.BlockSpec((None, CH, GROUP_W), tok)

    def zspec(name):
        blk = CB[name] // N_HEADS
        return pl.BlockSpec((None, CH, GROUP_W), lambda b, i: (b, i + first, blk))

    return pl.pallas_call(
        functools.partial(_out_kernel, first_block=first),
        grid=(bsz, t // CH - first),
        in_specs=[pl.BlockSpec((None, CH, d), tok), yspec, yspec, yspec, yspec,
                  zspec("m_z"), zspec("r_z"), zspec("a_z"), zspec("d_z"),
                  pl.BlockSpec((1, 3 * GROUP_W), lambda b, i: (0, 0)),
                  pl.BlockSpec((None, d, d), lambda b, i: (layer, 0, 0)),
                  pl.BlockSpec((8, 3 * d), lambda b, i: (0, 0)),
                  pl.BlockSpec((1, d), lambda b, i: (0, 0))],
        out_specs=pl.BlockSpec((None, CH, d), lambda b, i: (b, i, 0)),
        out_shape=jax.ShapeDtypeStruct((bsz, t - first * CH, d), F32),
        compiler_params=_params(("parallel", "parallel")),
        name="out_proj",
    )(xa, *ys, u, u, u, u, hn_w.reshape(1, 3 * GROUP_W), w_out, mod, post_w.reshape(1, d))


M_GATE_COL = 5 * GROUP_W
N_M_GATES = 4 * N_HEADS
D_GATE_COL = N_MAIN + N_M_GATES
W_TILE = 512


def _wprep_kernel(a_ref, b_ref, o_ref):
    n = pl.program_id(1)

    @pl.when(n < M_GATE_COL // W_TILE)
    def _():
        o_ref[...] = a_ref[...]

    @pl.when(n >= M_GATE_COL // W_TILE)
    def _():
        o_ref[...] = jnp.concatenate([a_ref[:, N_M_GATES:], b_ref[:, :N_M_GATES]], axis=1)


def _reorder_w_in(w_in):
    depth, d, _ = w_in.shape
    assert M_GATE_COL % W_TILE == 0
    wb = w_in.astype(BF16)
    w_main = pl.pallas_call(
        _wprep_kernel,
        grid=(depth, N_MAIN // W_TILE),
        in_specs=[pl.BlockSpec((None, d, W_TILE), lambda l, n: (l, 0, n)),
                  pl.BlockSpec((None, d, LANES), lambda l, n: (l, 0, (n + 1) * (W_TILE // LANES)))],
        out_specs=pl.BlockSpec((None, d, W_TILE), lambda l, n: (l, 0, n)),
        out_shape=jax.ShapeDtypeStruct((depth, d, N_MAIN), BF16),
        compiler_params=_params(("parallel", "parallel")),
        name="w_in_prep",
    )(wb, wb)
    gates = jnp.concatenate([wb[:, :, M_GATE_COL:M_GATE_COL + N_M_GATES], wb[:, :, D_GATE_COL:]], axis=2)
    w_gate = jnp.pad(gates, ((0, 0), (0, 0), (0, LANES - gates.shape[2])))
    return w_main, w_gate


def _rope_tables(seq, ctx_len):
    rows = seq // GRID_W
    row = jnp.repeat(jnp.arange(rows), GRID_W)
    col = jnp.tile(jnp.arange(GRID_W), rows)
    n_freq = HEAD_DIM // 4
    inv_freq = ROPE_BASE ** (-jnp.arange(n_freq, dtype=F32) / n_freq)
    ar = row[:, None] * inv_freq
    ac = col[:, None] * inv_freq
    cos = jnp.concatenate([jnp.cos(ar), jnp.cos(ar), jnp.cos(ac), jnp.cos(ac)], axis=1)
    sin = jnp.concatenate([-jnp.sin(ar), jnp.sin(ar), -jnp.sin(ac), jnp.sin(ac)], axis=1)
    cos = jnp.concatenate([jnp.ones((ctx_len, HEAD_DIM), F32), cos], axis=0)
    sin = jnp.concatenate([jnp.zeros((ctx_len, HEAD_DIM), F32), sin], axis=0)
    return cos, sin


def kernel(x, c, ctx, c_ctx, ada_w, ada_b, pre_norm_w, post_norm_w, w_in, w_out, mlstm_i_bias, mlstm_f_bias,
           ret_log_gamma, attn_q_norm_w, attn_k_norm_w, dn_conv_w, dn_a_log, dn_dt_bias, head_norm_w):
    bsz, seq, d = x.shape
    ctx_len = ctx.shape[1]
    t = ctx_len + seq
    depth = ada_w.shape[0]
    assert bsz <= 4 and ctx_len % CH == 0 and seq % CH == 0

    xa = jnp.concatenate([ctx, x], axis=1)
    c8 = jnp.zeros((8, d), F32).at[:bsz].set(c).at[4].set(c_ctx)
    mod = _modulation(c8, ada_w, ada_b)
    cos_t, sin_t = _rope_tables(seq, ctx_len)

    w_main, w_gate = _reorder_w_in(w_in)
    w_out_b = w_out.astype(BF16)

    for l in range(depth):
        u, g = _in_projection(xa, mod[l], pre_norm_w[l], w_main, w_gate, l, ctx_len)
        zeros8 = jnp.zeros((8,), F32)
        bias_row = jnp.pad(jnp.concatenate([mlstm_i_bias[l].reshape(-1), mlstm_f_bias[l].reshape(-1),
                                            dn_dt_bias[l].reshape(-1), zeros8]), (0, LANES - 32))
        alog_row = jnp.pad(jnp.concatenate([zeros8, zeros8, dn_a_log[l].reshape(-1), zeros8]), (0, LANES - 32))
        a, at_ch, at_dc = _gate_prep(g, bias_row.reshape(1, LANES), alog_row.reshape(1, LANES))
        y_m = _mlstm(u, a, at_ch, ctx_len)
        y_r = _retention(u, ret_log_gamma[l], cos_t, sin_t, ctx_len)
        y_a = _attention(u, cos_t, sin_t, attn_q_norm_w[l], attn_k_norm_w[l], ctx_len, with_ctx=(l < depth - 1))
        y_d = _deltanet(u, dn_conv_w[l], a, at_dc, ctx_len)
        xa = _out_projection(xa, (y_m, y_r, y_a, y_d), u, head_norm_w[l], w_out_b, l, mod[l],
                             post_norm_w[l], ctx_len, latent_only=(l == depth - 1))
    return xa
```

```python
import functools
import math

import jax
import jax.numpy as jnp
from jax import lax
from jax.experimental import pallas as pl
from jax.experimental.pallas import tpu as pltpu

F32 = jnp.float32
BF16 = jnp.bfloat16

D_MODEL = 2048
GROUP_W = 512
N_HEADS = 4
HEAD_DIM = 128
KV_HEADS = 2
GRID_W = 64
CONV_K = 5
ROPE_BASE = 10000.0
EPS = 1e-6
SCALE = HEAD_DIM ** -0.5
LOG2E = 1.4426950408889634
LOG2_SCALE = -0.5 * math.log2(HEAD_DIM)
NEG = -1e30

LANES = 128
CH = 256
DC = 64
PACK = 4
PRE_GROUPS = 4
N_MAIN = 64 * LANES
VMEM_LIMIT = 56 * 1024 * 1024
IN_PROJ_MAX_TM = 1088

CB = dict(m_q=0, m_k=4, m_v=8, m_o=12, m_z=16, r_q=20, r_k=24, r_v=28, r_z=32,
          a_q=36, a_k=40, a_v=42, a_z=44, d_q=48, d_k=52, d_v=56, d_z=60)


def _dot(a, b, prec=None):
    return jnp.dot(a, b, preferred_element_type=F32, precision=prec)


def _dot_nt(a, b, prec=None):
    return lax.dot_general(a, b, (((1,), (1,)), ((), ())), preferred_element_type=F32, precision=prec)


def _dot_tn(a, b, prec=None):
    return lax.dot_general(a, b, (((0,), (0,)), ((), ())), preferred_element_type=F32, precision=prec)


def _split(a):
    hi = a.astype(BF16)
    return hi, (a - hi.astype(F32)).astype(BF16)


def _dot3(a, b):
    n = a[0].shape[0]
    x = _dot(jnp.concatenate([a[0], a[1]], axis=0), b[0])
    return (x[:n] + x[n:]) + _dot(a[0], b[1])


def _sigmoid(x):
    return 1.0 / (1.0 + jnp.exp(-x))


def _lane_col(blk, idx):
    lane = lax.broadcasted_iota(jnp.int32, blk.shape, 1)
    return jnp.sum(jnp.where(lane == idx, blk, 0.0), axis=1, keepdims=True)


def _sub_row(blk, idx):
    sub = lax.broadcasted_iota(jnp.int32, blk.shape, 0)
    return jnp.sum(jnp.where(sub == idx, blk, 0.0), axis=0, keepdims=True)


def _rope(t, cos, sin_signed):
    lane = lax.broadcasted_iota(jnp.int32, t.shape, 1)
    partner = jnp.where((lane // 32) % 2 == 0, pltpu.roll(t, 96, 1), pltpu.roll(t, 32, 1))
    return t * cos + partner * sin_signed


def _params(sem):
    return pltpu.CompilerParams(dimension_semantics=sem, vmem_limit_bytes=VMEM_LIMIT)


def _mod_kernel(c_ref, w_ref, b_ref, o_ref):
    c = c_ref[...]
    o_ref[...] = _dot3(_split(c * _sigmoid(c)), _split(w_ref[...])) + b_ref[...]


def _modulation(c8, ada_w, ada_b):
    depth, d, n3 = ada_w.shape
    tn = 768
    return pl.pallas_call(
        _mod_kernel,
        grid=(depth, n3 // tn),
        in_specs=[pl.BlockSpec((8, d), lambda l, n: (0, 0)),
                  pl.BlockSpec((None, d, tn), lambda l, n: (l, 0, n)),
                  pl.BlockSpec((None, 1, tn), lambda l, n: (l, 0, n))],
        out_specs=pl.BlockSpec((None, 8, tn), lambda l, n: (l, 0, n)),
        out_shape=jax.ShapeDtypeStruct((depth, 8, n3), F32),
        compiler_params=_params(("parallel", "parallel")),
        name="adaln_mod",
    )(c8, ada_w, ada_b.reshape(depth, 1, n3))


def _inproj_kernel(x_ref, mod_ref, pw_ref, wm_ref, wg_ref, u_ref, g_ref, h_scr, *, tm, ctx_len, per_batch,
                   n_blocks, n_tiles):
    m = pl.program_id(0)
    n = pl.program_id(1)

    d = D_MODEL

    def modulated(x, row):
        y = x * lax.rsqrt(jnp.mean(x * x, axis=-1, keepdims=True) + EPS) * pw_ref[...]
        return (y * (1.0 + mod_ref[pl.ds(row, 1), d:2 * d]) + mod_ref[pl.ds(row, 1), 0:d]).astype(BF16)

    def normalise(blk, slot):
        b = blk // per_batch
        head = modulated(x_ref[0:ctx_len, :], jnp.where(blk % per_batch == 0, 4, b))
        hh = jnp.concatenate([head, modulated(x_ref[ctx_len:, :], b)], axis=0)
        h_scr[slot] = hh
        g_ref[...] = _dot(hh, wg_ref[...])

    @pl.when((m == 0) & (n == 0))
    def _():
        normalise(0, 0)

    @pl.when(n < n_tiles - 1)
    def _():
        u_ref[...] = _dot(h_scr[m % 2], wm_ref[...]).astype(BF16)

    @pl.when(n == n_tiles - 1)
    def _():
        u_ref[...] = _dot(h_scr[m % 2], wm_ref[...]).astype(BF16)
        normalise(jnp.minimum(m + 1, n_blocks - 1), (m + 1) % 2)


def _in_projection(xa, mod, pre_w, w_main, w_gate, layer, ctx_len):
    bsz, t, d = xa.shape
    tm = max(c for c in range(16, IN_PROJ_MAX_TM + 1, 16) if t % c == 0)
    assert tm > ctx_len
    tn = 1024
    per_batch = t // tm
    n_blocks = bsz * per_batch
    n_tiles = N_MAIN // tn
    kern = functools.partial(_inproj_kernel, tm=tm, ctx_len=ctx_len, per_batch=per_batch, n_blocks=n_blocks,
                             n_tiles=n_tiles)
    nxt = lambda m: jnp.minimum(m + 1, n_blocks - 1)
    u, g = pl.pallas_call(
        kern,
        grid=(n_blocks, n_tiles),
        in_specs=[pl.BlockSpec((None, tm, d), lambda m, n: (jnp.where((m == 0) & (n == 0), 0, nxt(m)), 0, 0)),
                  pl.BlockSpec((8, 3 * d), lambda m, n: (0, 0)),
                  pl.BlockSpec((1, d), lambda m, n: (0, 0)),
                  pl.BlockSpec((None, d, tn), lambda m, n: (layer, 0, n)),
                  pl.BlockSpec((None, d, LANES), lambda m, n: (layer, 0, 0))],
        out_specs=[pl.BlockSpec((None, tm, tn), lambda m, n: (m, 0, n)),
                   pl.BlockSpec((None, tm, LANES), lambda m, n: (jnp.where(n == n_tiles - 1, nxt(m), m), 0, 0))],
        out_shape=[jax.ShapeDtypeStruct((n_blocks, tm, N_MAIN), BF16),
                   jax.ShapeDtypeStruct((n_blocks, tm, LANES), F32)],
        scratch_shapes=[pltpu.VMEM((2, tm, d), BF16)],
        compiler_params=_params(("arbitrary", "arbitrary")),
        name="in_proj",
    )(xa.reshape(n_blocks, tm, d), mod, pre_w.reshape(1, d), w_main, w_gate)
    return u.reshape(bsz, t, N_MAIN), g.reshape(bsz, t, LANES)


def _gate_kernel(g_ref, bias_ref, alog_ref, a_ref, at_ref, atp_ref, *, n_chunks):
    lane = lax.broadcasted_iota(jnp.int32, (CH, LANES), 1)
    s_idx = lax.broadcasted_iota(jnp.int32, (CH, 4 * CH), 0)
    col = lax.broadcasted_iota(jnp.int32, (CH, 4 * CH), 1)
    kind = col // CH
    t_idx = col % CH
    ahead = jnp.where(kind % 2 == 0, t_idx - s_idx, s_idx - t_idx)
    other_sub = jnp.where(kind < 2, 0, jnp.abs(t_idx // DC - s_idx // DC))
    sums = jnp.where(ahead - CH * other_sub >= 0, 1.0, 0.0).astype(BF16)
    gate = lax.broadcasted_iota(jnp.int32, (32, CH), 0)
    fwd = (gate % 8) < 4
    bias = bias_ref[...]
    neg_ea = -jnp.exp(alog_ref[...])

    def chunk(ci, carry):
        rows = pl.ds(pl.multiple_of(ci * CH, CH), CH)
        x = g_ref[rows, :] + bias
        l1p = jnp.log(1.0 + jnp.exp(-jnp.abs(x)))
        log_f = jnp.minimum(x, 0.0) - l1p
        log_a = neg_ea * (jnp.maximum(x, 0.0) + l1p)
        v = jnp.where(lane < 8, x, jnp.where(lane < 16, log_f, jnp.where(lane < 24, log_a, _sigmoid(x))))
        v_t = v.T[0:32, :]
        v_t = jnp.where(gate < 24, v_t * LOG2E, v_t)
        v1 = v_t.astype(BF16)
        res1 = v_t - v1.astype(F32)
        v2 = res1.astype(BF16)
        v3 = (res1 - v2.astype(F32)).astype(BF16)
        cums = _dot(jnp.concatenate([v1, v2, v3], axis=0), sums)
        cums = cums[0:32] + (cums[32:64] + cums[64:96])
        cum_ch = jnp.where(fwd, cums[:, 0:CH], cums[:, CH:2 * CH])
        cum_dc = jnp.where(fwd, cums[:, 2 * CH:3 * CH], cums[:, 3 * CH:])
        a_t = jnp.where((gate >= 8) & (gate < 16), cum_ch, jnp.where((gate >= 16) & (gate < 24), cum_dc, v_t))
        a_ref[rows, :] = jnp.concatenate([a_t, jnp.zeros((LANES - 32, CH), F32)], axis=0).T
        at_ref[ci] = a_t
        pairs = CH // (2 * DC)
        for p in range(pairs):
            c0 = a_t[:, 2 * p * DC:(2 * p + 1) * DC]
            c1 = a_t[:, (2 * p + 1) * DC:(2 * p + 2) * DC]
            atp_ref[ci * pairs + p] = jnp.concatenate([c0, c0, c1, c1], axis=1)
        return carry

    lax.fori_loop(0, n_chunks, chunk, 0)


def _gate_prep(g, bias_row, alog_row):
    bsz, t, _ = g.shape
    n_chunks = t // CH
    return pl.pallas_call(
        functools.partial(_gate_kernel, n_chunks=n_chunks),
        grid=(bsz,),
        in_specs=[pl.BlockSpec((None, t, LANES), lambda b: (b, 0, 0)),
                  pl.BlockSpec((1, LANES), lambda b: (0, 0)),
                  pl.BlockSpec((1, LANES), lambda b: (0, 0))],
        out_specs=[pl.BlockSpec((None, t, LANES), lambda b: (b, 0, 0)),
                   pl.BlockSpec((None, n_chunks, 32, CH), lambda b: (b, 0, 0, 0)),
                   pl.BlockSpec((None, t // (2 * DC), 32, PACK * DC), lambda b: (b, 0, 0, 0))],
        out_shape=[jax.ShapeDtypeStruct((bsz, t, LANES), F32),
                   jax.ShapeDtypeStruct((bsz, n_chunks, 32, CH), F32),
                   jax.ShapeDtypeStruct((bsz, t // (2 * DC), 32, PACK * DC), F32)],
        compiler_params=_params(("parallel",)),
        name="gate_prep",
    )(g, bias_row, alog_row)


def _chunk_order(j, d, n_chunks, n_ctx_chunks):
    if d == 0:
        return j
    return jnp.where(j < n_ctx_chunks, n_ctx_chunks - 1 - j, n_chunks + n_ctx_chunks - 1 - j)


def _head_spec(t, name):
    base = CB[name]
    return pl.BlockSpec((None, t, LANES), lambda b, h: (b, 0, base + h))


def _mlstm_kernel(q_ref, k_ref, v_ref, o_ref, a_ref, at_ref, y_ref, c_scr, y1_scr, *, n_chunks, n_ctx_chunks):
    h = pl.program_id(1)
    r = lax.broadcasted_iota(jnp.int32, (CH, CH), 0)
    c = lax.broadcasted_iota(jnp.int32, (CH, CH), 1)
    one_col = (lax.broadcasted_iota(jnp.int32, (CH, LANES), 1) == 0).astype(F32)

    masks = ((c <= r), (c >= r))
    lasts = (CH - 1, 0)
    dirs = range(2)
    c_scr[...] = jnp.zeros_like(c_scr)

    def step(j, ms):
        cis = [_chunk_order(j, d, n_chunks, n_ctx_chunks) for d in dirs]
        rows = [pl.ds(pl.multiple_of(ci * CH, CH), CH) for ci in cis]
        qs = [q_ref[rw, :] for rw in rows]
        ks = [k_ref[rw, :] for rw in rows]
        qk = [_dot_nt(qs[d], ks[d]) for d in dirs]
        pc = [_dot(qs[d], c_scr[d].astype(BF16)) for d in dirs]
        a_blk = [a_ref[rw, :] for rw in rows]
        at_blk = [at_ref[ci] for ci in cis]
        li_col = [_lane_col(a_blk[d], d * 4 + h) for d in dirs]
        b_col = [_lane_col(a_blk[d], 8 + d * 4 + h) for d in dirs]
        li_row = [_sub_row(at_blk[d], d * 4 + h) for d in dirs]
        b_row = [_sub_row(at_blk[d], 8 + d * 4 + h) for d in dirs]
        dm = [jnp.where(masks[d], b_col[d] + (li_row[d] - b_row[d]), NEG) for d in dirs]
        inter = [ms[d] + b_col[d] for d in dirs]
        m_t = [jnp.maximum(inter[d], jnp.max(dm[d], axis=1, keepdims=True)) for d in dirs]
        m_s = [m_t[d] - LOG2_SCALE for d in dirs]
        s = [(qk[d] * jnp.exp2(dm[d] - m_s[d])).astype(BF16) for d in dirs]
        vs = [v_ref[rw, :] for rw in rows]
        sv = [_dot(s[d], jnp.concatenate([vs[d], one_col.astype(BF16)], axis=1)) for d in dirs]

        b_end = [b_col[d][lasts[d]:lasts[d] + 1, :] for d in dirs]
        dec = [(b_end[d] - b_col[d]) + li_col[d] for d in dirs]
        m_new = [jnp.maximum(ms[d] + b_end[d], jnp.max(dec[d], axis=0, keepdims=True)) for d in dirs]
        w_state = [jnp.exp2(dec[d] - m_new[d]) for d in dirs]
        wv = [jnp.concatenate([(w_state[d] * vs[d].astype(F32)).astype(BF16), (w_state[d] * one_col).astype(BF16)],
                              axis=1) for d in dirs]
        upd = [_dot_tn(ks[d], wv[d]) for d in dirs]
        for d in dirs:
            c_scr[d] = jnp.exp2(ms[d] + b_end[d] - m_new[d]) * c_scr[d] + upd[d]

        for d in dirs:
            nd = jnp.exp2(inter[d] - m_s[d]) * pc[d] + sv[d]
            h_out = nd[:, :LANES] / jnp.maximum(jnp.abs(nd[:, LANES:LANES + 1]), jnp.exp2(-m_t[d]))
            if d == 0:
                y_ref[rows[d], :] = h_out
            else:
                y1_scr[rows[d], :] = h_out
        return tuple(m_new)

    zero = jnp.zeros((1, 1), F32)
    lax.fori_loop(0, n_chunks, step, (zero, zero))

    def finish(ci, carry):
        rw = pl.ds(pl.multiple_of(ci * CH, CH), CH)
        y_ref[rw, :] = (y_ref[rw, :] + y1_scr[rw, :]) * _sigmoid(o_ref[rw, :].astype(F32))
        return carry

    lax.fori_loop(0, n_chunks, finish, 0)


def _mlstm(u, a, at_ch, ctx_len):
    bsz, t, _ = u.shape
    n_chunks = t // CH
    kern = functools.partial(_mlstm_kernel, n_chunks=n_chunks, n_ctx_chunks=ctx_len // CH)
    return pl.pallas_call(
        kern,
        grid=(bsz, N_HEADS),
        in_specs=[_head_spec(t, "m_q"), _head_spec(t, "m_k"), _head_spec(t, "m_v"), _head_spec(t, "m_o"),
                  pl.BlockSpec((None, t, LANES), lambda b, h: (b, 0, 0)),
                  pl.BlockSpec((None, n_chunks, 32, CH), lambda b, h: (b, 0, 0, 0))],
        out_specs=pl.BlockSpec((None, t, LANES), lambda b, h: (b, 0, h)),
        out_shape=jax.ShapeDtypeStruct((bsz, t, GROUP_W), F32),
        scratch_shapes=[pltpu.VMEM((2, HEAD_DIM, 2 * LANES), F32), pltpu.VMEM((t, LANES), F32)],
        compiler_params=_params(("parallel", "parallel")),
        name="mlstm",
    )(u, u, u, u, a, at_ch)


def _retention_kernel(lg_ref, q_ref, k_ref, v_ref, cos_ref, sin_ref, y_ref, qp_scr, kp_scr, r_scr, y1_scr, *,
                      n_chunks, n_ctx_chunks):
    h = pl.program_id(1)

    def prep(ci, carry):
        rows = pl.ds(pl.multiple_of(ci * CH, CH), CH)
        cos = cos_ref[rows, :]
        sin = sin_ref[rows, :]
        qp_scr[rows, :] = _rope(q_ref[rows, :].astype(F32), cos, sin).astype(BF16)
        kp_scr[rows, :] = (_rope(k_ref[rows, :].astype(F32), cos, sin) * SCALE).astype(BF16)
        return carry

    lax.fori_loop(0, n_chunks, prep, 0)

    r = lax.broadcasted_iota(jnp.int32, (CH, CH), 0).astype(F32)
    c = lax.broadcasted_iota(jnp.int32, (CH, CH), 1).astype(F32)
    pos = lax.broadcasted_iota(jnp.int32, (CH, 1), 0).astype(F32)

    dirs = range(2)
    decay, q_decay, k_decay, chunk_decay = [], [], [], []
    for d in dirs:
        lg = lg_ref[d * N_HEADS + h]
        rel = (r - c) if d == 0 else (c - r)
        decay.append(jnp.where(rel >= 0, jnp.exp(lg * jnp.maximum(rel, 0.0)), 0.0))
        p_vis = pos if d == 0 else (CH - 1.0) - pos
        q_decay.append(jnp.exp(lg * (p_vis + 1.0)))
        k_decay.append(jnp.exp(lg * ((CH - 1.0) - p_vis)))
        chunk_decay.append(jnp.exp(lg * CH))
    r_scr[...] = jnp.zeros_like(r_scr)

    def step(j, carry):
        cis = [_chunk_order(j, d, n_chunks, n_ctx_chunks) for d in dirs]
        rows = [pl.ds(pl.multiple_of(ci * CH, CH), CH) for ci in cis]
        qs = [qp_scr[rw, :] for rw in rows]
        ks = [kp_scr[rw, :] for rw in rows]
        vs = [v_ref[rw, :] for rw in rows]
        qk = [_dot_nt(qs[d], ks[d]) for d in dirs]
        qr = [_dot(qs[d], r_scr[d].astype(BF16)) for d in dirs]
        sv = [_dot((qk[d] * decay[d]).astype(BF16), vs[d]) for d in dirs]
        upd = [_dot_tn((ks[d].astype(F32) * k_decay[d]).astype(BF16), vs[d]) for d in dirs]
        for d in dirs:
            r_scr[d] = chunk_decay[d] * r_scr[d] + upd[d]
        y_ref[rows[0], :] = sv[0] + q_decay[0] * qr[0]
        y1_scr[rows[1], :] = sv[1] + q_decay[1] * qr[1]
        return carry

    lax.fori_loop(0, n_chunks, step, 0)

    def finish(ci, carry):
        rw = pl.ds(pl.multiple_of(ci * CH, CH), CH)
        y_ref[rw, :] = y_ref[rw, :] + y1_scr[rw, :]
        return carry

    lax.fori_loop(0, n_chunks, finish, 0)


def _retention(u, log_gamma, cos_t, sin_t, ctx_len):
    bsz, t, _ = u.shape
    n_chunks = t // CH
    kern = functools.partial(_retention_kernel, n_chunks=n_chunks, n_ctx_chunks=ctx_len // CH)
    tab = pl.BlockSpec((t, LANES), lambda b, h: (0, 0))
    return pl.pallas_call(
        kern,
        grid=(bsz, N_HEADS),
        in_specs=[pl.BlockSpec(memory_space=pltpu.SMEM),
                  _head_spec(t, "r_q"), _head_spec(t, "r_k"), _head_spec(t, "r_v"), tab, tab],
        out_specs=pl.BlockSpec((None, t, LANES), lambda b, h: (b, 0, h)),
        out_shape=jax.ShapeDtypeStruct((bsz, t, GROUP_W), F32),
        scratch_shapes=[pltpu.VMEM((t, LANES), BF16), pltpu.VMEM((t, LANES), BF16),
                        pltpu.VMEM((2, HEAD_DIM, HEAD_DIM), F32), pltpu.VMEM((t, LANES), F32)],
        compiler_params=_params(("parallel", "parallel")),
        name="retention",
    )(log_gamma.reshape(2 * N_HEADS), u, u, u, cos_t, sin_t)


def _attn_kernel(q_ref, k_ref, v_ref, cos_ref, sin_ref, cosq_ref, sinq_ref, qw_ref, kw_ref, o_ref,
                 kt_scr, s_scr, m_scr, *, n_blocks, with_ctx):
    j = pl.program_id(2)

    def load_q():
        qs = []
        for g in range(2):
            q = q_ref[:, g * LANES:(g + 1) * LANES].astype(F32)
            q = q * lax.rsqrt(jnp.mean(q * q, axis=-1, keepdims=True) + EPS) * qw_ref[...]
            qs.append((_rope(q, cosq_ref[...], sinq_ref[...]) * (SCALE * LOG2E)).astype(BF16))
        return jnp.concatenate(qs, axis=0)

    def finish(o, l_part):
        o = o / jnp.sum(l_part, axis=-1, keepdims=True)
        o_ref[:, 0:LANES] = o[:CH]
        o_ref[:, LANES:2 * LANES] = o[CH:]

    @pl.when(j == 0)
    def _():
        def prep(ci, carry):
            rows = pl.ds(pl.multiple_of(ci * CH, CH), CH)
            k = k_ref[rows, :].astype(F32)
            k = k * lax.rsqrt(jnp.mean(k * k, axis=-1, keepdims=True) + EPS) * kw_ref[...]
            kt_scr[ci] = _rope(k, cos_ref[rows, :], sin_ref[rows, :]).T.astype(BF16)
            return carry

        lax.fori_loop(0, n_blocks, prep, 0)
        if with_ctx:
            s = _dot(load_q(), kt_scr[0])
            p = jnp.exp2(s - jnp.max(s, axis=-1, keepdims=True))
            finish(_dot(p.astype(BF16), v_ref[0:CH, :]), p[:, :LANES] + p[:, LANES:])
        else:
            o_ref[...] = jnp.zeros_like(o_ref)

    @pl.when(j >= 1)
    def _():
        qq = load_q()
        mx = None
        for kb in range(n_blocks):
            s = _dot(qq, kt_scr[kb])
            s_scr[:, kb * CH:(kb + 1) * CH] = s
            m2 = jnp.maximum(s[:, :LANES], s[:, LANES:])
            mx = m2 if mx is None else jnp.maximum(mx, m2)
        m_scr[...] = jnp.broadcast_to(jnp.max(mx, axis=-1, keepdims=True), (2 * CH, LANES))

    def value_pass(_, carry):
        m = m_scr[...]
        l_acc = o = None
        for kb in range(n_blocks):
            cols = slice(kb * CH, (kb + 1) * CH)
            p_lo = jnp.exp2(s_scr[:, kb * CH:kb * CH + LANES] - m)
            p_hi = jnp.exp2(s_scr[:, kb * CH + LANES:(kb + 1) * CH] - m)
            l_acc = (p_lo + p_hi) if l_acc is None else l_acc + (p_lo + p_hi)
            pv = _dot(jnp.concatenate([p_lo, p_hi], axis=1).astype(BF16), v_ref[cols, :])
            o = pv if o is None else o + pv
        finish(o, l_acc)
        return carry

    lax.fori_loop(0, jnp.minimum(j, 1), value_pass, 0)


def _attention(u, cos_t, sin_t, qn_w, kn_w, ctx_len, with_ctx):
    bsz, t, _ = u.shape
    assert ctx_len == CH
    n_blocks = t // CH
    kern = functools.partial(_attn_kernel, n_blocks=n_blocks, with_ctx=with_ctx)
    qb = CB["a_q"] // 2
    kb = CB["a_k"]
    vb = CB["a_v"]
    tab = pl.BlockSpec((t, LANES), lambda b, kv, j: (0, 0))
    tabq = pl.BlockSpec((CH, LANES), lambda b, kv, j: (j, 0))
    vec = pl.BlockSpec((1, LANES), lambda b, kv, j: (0, 0))
    return pl.pallas_call(
        kern,
        grid=(bsz, KV_HEADS, n_blocks),
        in_specs=[pl.BlockSpec((None, CH, 2 * LANES), lambda b, kv, j: (b, j, qb + kv)),
                  pl.BlockSpec((None, t, LANES), lambda b, kv, j: (b, 0, kb + kv)),
                  pl.BlockSpec((None, t, LANES), lambda b, kv, j: (b, 0, vb + kv)),
                  tab, tab, tabq, tabq, vec, vec],
        out_specs=pl.BlockSpec((None, CH, 2 * LANES), lambda b, kv, j: (b, j, kv)),
        out_shape=jax.ShapeDtypeStruct((bsz, t, GROUP_W), F32),
        scratch_shapes=[pltpu.VMEM((n_blocks, HEAD_DIM, CH), BF16), pltpu.VMEM((2 * CH, t), F32),
                        pltpu.VMEM((2 * CH, LANES), F32)],
        compiler_params=_params(("parallel", "parallel", "arbitrary")),
        name="attention",
    )(u, u, u, cos_t, sin_t, cos_t, sin_t, qn_w.reshape(1, LANES), kn_w.reshape(1, LANES))


def _block_diag(y, lane_masks):
    return jnp.concatenate([y * mk for mk in lane_masks], axis=0)


def _packed_dot3(a, b, lane_masks):
    n = a[0].shape[0]
    x = _dot(jnp.concatenate([a[0], a[1]], axis=0), _block_diag(b[0], lane_masks))
    return (x[:n] + x[n:]) + _dot(a[0], _block_diag(b[1], lane_masks))


def _tri_inverse(n_mats, r, cl, lane_masks):
    eye = (r == cl).astype(F32)
    diag16 = (r // 16) == (cl // 16)
    ms = [-jnp.where(diag16, n, 0.0) for n in n_mats]
    ps = [eye + m for m in ms]
    mps = [_split(m) for m in ms]
    mps = [_split(_packed_dot3(mp, mp, lane_masks)) for mp in mps]
    for _ in range(2):
        both = [_packed_dot3(tuple(jnp.concatenate([a, b], axis=0) for a, b in zip(_split(p), mp)), mp, lane_masks)
                for p, mp in zip(ps, mps)]
        ps = [p + x[:DC] for p, x in zip(ps, both)]
        mps = [_split(x[DC:]) for x in both]
    ps = [p + _packed_dot3(_split(p), mp, lane_masks) for p, mp in zip(ps, mps)]
    for w in (32, 64):
        off = ((r // w) == (cl // w)) & ((r // (w // 2)) != (cl // (w // 2)))
        pbs = [p.astype(BF16) for p in ps]
        pds = [_block_diag(pb, lane_masks) for pb in pbs]
        tmp = [_dot(jnp.where(off, n, 0.0).astype(BF16), pd).astype(BF16) for n, pd in zip(n_mats, pds)]
        ps = [p - _dot(pb, _block_diag(a, lane_masks)) for p, pb, a in zip(ps, pbs, tmp)]
    return ps


def _deltanet_kernel(q_ref, k_ref, v_ref, wq_ref, wk_ref, wv_ref, a_ref, at_ref, y_ref,
                     xs_scr, qd_scr, kd_scr, vd_scr, aq_scr, b_scr, ge_scr, y1_scr, *,
                     t, ctx_len):
    h = pl.program_id(1)
    n_chunks = t // DC
    n_ctx_chunks = ctx_len // DC
    pad = 8

    streams = ((q_ref, wq_ref, qd_scr, True, SCALE), (k_ref, wk_ref, kd_scr, True, 1.0),
               (v_ref, wv_ref, vd_scr, False, 1.0))
    zeros = jnp.zeros((pad, LANES), F32)
    for si in range(len(streams)):
        xs_scr[si, 0:pad, :] = zeros
        xs_scr[si, ctx_len + pad:ctx_len + 2 * pad, :] = zeros
        xs_scr[si, t + 2 * pad:t + 3 * pad, :] = zeros

    def seg_row(ci):
        return ci * CH + pad + jnp.where(ci * CH >= ctx_len, pad, 0)

    def load(ci, carry):
        rows = pl.ds(pl.multiple_of(ci * CH, CH), CH)
        for si, (src_ref, _, _, _, _) in enumerate(streams):
            xs_scr[si, pl.ds(pl.multiple_of(seg_row(ci), 8), CH), :] = src_ref[rows, :].astype(F32)
        return carry

    lax.fori_loop(0, t // CH, load, 0)

    def conv(ci, carry):
        base = seg_row(ci) - CONV_K // 2
        for si, (_, w_ref, dst_scr, l2, scale) in enumerate(streams):
            acc = xs_scr[si, pl.ds(base, CH), :] * w_ref[0:1, :]
            for j in range(1, CONV_K):
                acc = acc + xs_scr[si, pl.ds(base + j, CH), :] * w_ref[j:j + 1, :]
            acc = acc * _sigmoid(acc)
            if l2:
                acc = acc * lax.rsqrt(jnp.sum(acc * acc, axis=-1, keepdims=True) + EPS) * scale
            dst_scr[pl.ds(pl.multiple_of(ci * CH, CH), CH), :] = acc
        return carry

    lax.fori_loop(0, t // CH, conv, 0)

    wp = PACK * DC
    r = lax.broadcasted_iota(jnp.int32, (DC, wp), 0)
    lane = lax.broadcasted_iota(jnp.int32, (DC, wp), 1)
    blk = lane // DC
    cl = lane % DC
    fwd = (blk % 2) == 0
    ahead = jnp.where(fwd, r - cl, cl - r)
    incl = ahead >= 0
    strict = ahead > 0
    lane_masks = [(blk == i).astype(BF16) for i in range(PACK)]

    def pick(vals):
        out = vals[PACK - 1]
        for i in range(PACK - 2, -1, -1):
            out = jnp.where(blk == i, vals[i], out)
        return out

    def pre(first_pair, n_groups):
        groups = []
        n_mats = []
        for gg in range(n_groups):
            pair = first_pair + gg
            at_blk = at_ref[pair]
            g_row = jnp.where(fwd[0:1, :], _sub_row(at_blk, 16 + h), _sub_row(at_blk, 20 + h))
            kks, qks, g_cols, betas, chains = [], [], [], [], []
            for cc in range(2):
                ci = pair * 2 + cc
                rows = pl.ds(pl.multiple_of(ci * DC, DC), DC)
                q = qd_scr[rows, :]
                k = kd_scr[rows, :]
                v = vd_scr[rows, :]
                a_blk = a_ref[rows, :]
                kb = k.astype(BF16)
                kb2 = jnp.concatenate([kb, kb], axis=0)
                kks.append(_dot_nt(kb, kb2))
                qks.append(_dot_nt(q.astype(BF16), kb2))
                for d in range(2):
                    last = DC - 1 if d == 0 else 0
                    g_col = _lane_col(a_blk, 16 + d * 4 + h)
                    beta = _lane_col(a_blk, 24 + d * 4 + h)
                    eg = jnp.exp2(g_col)
                    g_end = g_col[last:last + 1, :]
                    rhs = jnp.concatenate([beta * v, (beta * eg) * k], axis=1).astype(BF16)
                    ke = (jnp.exp2(g_end - g_col) * k).astype(BF16)
                    ge_scr[d, ci] = jnp.broadcast_to(jnp.exp2(g_end), (8, LANES))
                    g_cols.append(g_col)
                    betas.append(beta)
                    chains.append((d, ci, rows, rhs, ke, eg * q))
            decay = jnp.exp2(jnp.where(incl, pick(g_cols) - g_row, NEG))
            n_mats.append(jnp.where(strict, pick(betas) * decay * jnp.concatenate(kks, axis=1), 0.0))
            qkd = (jnp.concatenate(qks, axis=1) * decay).astype(BF16)
            groups.append((chains, qkd, jnp.concatenate([ch[3] for ch in chains], axis=0)))
        t_invs = [t_inv.astype(BF16) for t_inv in _tri_inverse(n_mats, r, cl, lane_masks)]
        ws = [[_dot(t_inv * mk, rhs_all).astype(BF16) for mk in lane_masks]
              for t_inv, (_, _, rhs_all) in zip(t_invs, groups)]
        w_all = [jnp.concatenate(w4, axis=0) for w4 in ws]
        kws = [[_dot_tn(ch[4], w) for ch, w in zip(chains, w4)] for (chains, _, _), w4 in zip(groups, ws)]
        qws = [[_dot(qkd * mk, wa) for mk in lane_masks] for (_, qkd, _), wa in zip(groups, w_all)]
        for (chains, _, _), kw4, qw4 in zip(groups, kws, qws):
            for (d, ci, rows, _, _, egq), kw, qw in zip(chains, kw4, qw4):
                aq_scr[d, ci, 0:HEAD_DIM, :] = (-kw[:, LANES:]).astype(BF16)
                aq_scr[d, ci, HEAD_DIM:HEAD_DIM + DC, :] = (egq - qw[:, LANES:]).astype(BF16)
                b_scr[d, ci] = kw[:, :LANES]
                if d == 0:
                    y_ref[rows, :] = qw[:, :LANES]
                else:
                    y1_scr[rows, :] = qw[:, :LANES]

    n_pairs = n_chunks // 2
    n_trips, tail = divmod(n_pairs, PRE_GROUPS)

    def pre_trip(i, carry):
        pre(i * PRE_GROUPS, PRE_GROUPS)
        return carry

    lax.fori_loop(0, n_trips, pre_trip, 0)
    if tail:
        pre(n_trips * PRE_GROUPS, tail)

    def scan(j, carry):
        cis = [_chunk_order(j, d, n_chunks, n_ctx_chunks) for d in range(2)]
        xs = [_dot(aq_scr[d, cis[d]], carry[d].astype(BF16)) for d in range(2)]
        new = tuple(ge_scr[d, cis[d]][0:1, :] * carry[d] + (xs[d][:HEAD_DIM, :] + b_scr[d, cis[d]])
                    for d in range(2))
        rows = [pl.ds(pl.multiple_of(cis[d] * DC, DC), DC) for d in range(2)]
        y_ref[rows[0], :] = y_ref[rows[0], :] + xs[0][HEAD_DIM:, :]
        y1_scr[rows[1], :] = y1_scr[rows[1], :] + xs[1][HEAD_DIM:, :]
        return new

    zero = jnp.zeros((HEAD_DIM, HEAD_DIM), F32)
    lax.fori_loop(0, n_chunks, scan, (zero, zero))

    def add(ci, carry):
        rows = pl.ds(pl.multiple_of(ci * CH, CH), CH)
        y_ref[rows, :] = y_ref[rows, :] + y1_scr[rows, :]
        return carry

    lax.fori_loop(0, t // CH, add, 0)


def _deltanet(u, conv_w, a, at_dc, ctx_len):
    bsz, t, _ = u.shape
    n_chunks = t // DC
    kern = functools.partial(_deltanet_kernel, t=t, ctx_len=ctx_len)

    def wspec(off):
        return pl.BlockSpec((CONV_K, LANES), lambda b, h: (0, off + h))

    return pl.pallas_call(
        kern,
        grid=(bsz, N_HEADS),
        in_specs=[_head_spec(t, "d_q"), _head_spec(t, "d_k"), _head_spec(t, "d_v"),
                  wspec(0), wspec(N_HEADS), wspec(2 * N_HEADS),
                  pl.BlockSpec((None, t, LANES), lambda b, h: (b, 0, 0)),
                  pl.BlockSpec((None, n_chunks // 2, 32, PACK * DC), lambda b, h: (b, 0, 0, 0))],
        out_specs=pl.BlockSpec((None, t, LANES), lambda b, h: (b, 0, h)),
        out_shape=jax.ShapeDtypeStruct((bsz, t, GROUP_W), F32),
        scratch_shapes=[pltpu.VMEM((3, t + 24, LANES), F32),
                        pltpu.VMEM((t, LANES), F32), pltpu.VMEM((t, LANES), F32), pltpu.VMEM((t, LANES), F32),
                        pltpu.VMEM((2, n_chunks, HEAD_DIM + DC, LANES), BF16),
                        pltpu.VMEM((2, n_chunks, HEAD_DIM, LANES), F32),
                        pltpu.VMEM((2, n_chunks, 8, LANES), F32),
                        pltpu.VMEM((t, LANES), F32)],
        compiler_params=_params(("parallel", "parallel")),
        name="deltanet",
    )(u, u, u, conv_w, conv_w, conv_w, a, at_dc)


def _out_kernel(x_ref, ym_ref, yr_ref, ya_ref, yd_ref, zm_ref, zr_ref, za_ref, zd_ref, hn_ref, w_ref,
                mod_ref, pw_ref, o_ref, *, first_block):
    b = pl.program_id(0)
    i = pl.program_id(1) + first_block

    def silu(z_ref):
        z = z_ref[...].astype(F32)
        return z * _sigmoid(z)

    def head_rms(y_ref, g):
        y = y_ref[...]
        parts = []
        for hh in range(N_HEADS):
            p = y[:, hh * LANES:(hh + 1) * LANES]
            parts.append(p * lax.rsqrt(jnp.mean(p * p, axis=-1, keepdims=True) + EPS))
        return jnp.concatenate(parts, axis=1) * hn_ref[:, g * GROUP_W:(g + 1) * GROUP_W]

    merged = jnp.concatenate([
        (head_rms(ym_ref, 0) * silu(zm_ref)).astype(BF16),
        (head_rms(yr_ref, 1) * silu(zr_ref)).astype(BF16),
        (ya_ref[...] * silu(za_ref)).astype(BF16),
        (head_rms(yd_ref, 2) * silu(zd_ref)).astype(BF16)], axis=1)
    o = _dot(merged, w_ref[...])
    o = o * lax.rsqrt(jnp.mean(o * o, axis=-1, keepdims=True) + EPS) * pw_ref[...]
    row = jnp.where(i == 0, 4, b)
    gate = mod_ref[pl.ds(row, 1), 2 * D_MODEL:3 * D_MODEL]
    o_ref[...] = x_ref[...] + gate * o


def _out_projection(xa, ys, u, hn_w, w_out, layer, mod, post_w, ctx_len, latent_only):
    bsz, t, d = xa.shape
    assert ctx_len == CH
    first = 1 if latent_only else 0
    tok = lambda b, i: (b, i + first, 0)
    yspec = pl.BlockSpec((None, CH, GROUP_W), tok)

    def zspec(name):
        blk = CB[name] // N_HEADS
        return pl.BlockSpec((None, CH, GROUP_W), lambda b, i: (b, i + first, blk))

    return pl.pallas_call(
        functools.partial(_out_kernel, first_block=first),
        grid=(bsz, t // CH - first),
        in_specs=[pl.BlockSpec((None, CH, d), tok), yspec, yspec, yspec, yspec,
                  zspec("m_z"), zspec("r_z"), zspec("a_z"), zspec("d_z"),
                  pl.BlockSpec((1, 3 * GROUP_W), lambda b, i: (0, 0)),
                  pl.BlockSpec((None, d, d), lambda b, i: (layer, 0, 0)),
                  pl.BlockSpec((8, 3 * d), lambda b, i: (0, 0)),
                  pl.BlockSpec((1, d), lambda b, i: (0, 0))],
        out_specs=pl.BlockSpec((None, CH, d), lambda b, i: (b, i, 0)),
        out_shape=jax.ShapeDtypeStruct((bsz, t - first * CH, d), F32),
        compiler_params=_params(("parallel", "parallel")),
        name="out_proj",
    )(xa, *ys, u, u, u, u, hn_w.reshape(1, 3 * GROUP_W), w_out, mod, post_w.reshape(1, d))


M_GATE_COL = 5 * GROUP_W
N_M_GATES = 4 * N_HEADS
D_GATE_COL = N_MAIN + N_M_GATES
W_TILE = 512


def _wprep_kernel(a_ref, b_ref, o_ref):
    n = pl.program_id(1)

    @pl.when(n < M_GATE_COL // W_TILE)
    def _():
        o_ref[...] = a_ref[...]

    @pl.when(n >= M_GATE_COL // W_TILE)
    def _():
        o_ref[...] = jnp.concatenate([a_ref[:, N_M_GATES:], b_ref[:, :N_M_GATES]], axis=1)


def _reorder_w_in(w_in):
    depth, d, _ = w_in.shape
    assert M_GATE_COL % W_TILE == 0
    wb = w_in.astype(BF16)
    w_main = pl.pallas_call(
        _wprep_kernel,
        grid=(depth, N_MAIN // W_TILE),
        in_specs=[pl.BlockSpec((None, d, W_TILE), lambda l, n: (l, 0, n)),
                  pl.BlockSpec((None, d, LANES), lambda l, n: (l, 0, (n + 1) * (W_TILE // LANES)))],
        out_specs=pl.BlockSpec((None, d, W_TILE), lambda l, n: (l, 0, n)),
        out_shape=jax.ShapeDtypeStruct((depth, d, N_MAIN), BF16),
        compiler_params=_params(("parallel", "parallel")),
        name="w_in_prep",
    )(wb, wb)
    gates = jnp.concatenate([wb[:, :, M_GATE_COL:M_GATE_COL + N_M_GATES], wb[:, :, D_GATE_COL:]], axis=2)
    w_gate = jnp.pad(gates, ((0, 0), (0, 0), (0, LANES - gates.shape[2])))
    return w_main, w_gate


def _rope_tables(seq, ctx_len):
    rows = seq // GRID_W
    row = jnp.repeat(jnp.arange(rows), GRID_W)
    col = jnp.tile(jnp.arange(GRID_W), rows)
    n_freq = HEAD_DIM // 4
    inv_freq = ROPE_BASE ** (-jnp.arange(n_freq, dtype=F32) / n_freq)
    ar = row[:, None] * inv_freq
    ac = col[:, None] * inv_freq
    cos = jnp.concatenate([jnp.cos(ar), jnp.cos(ar), jnp.cos(ac), jnp.cos(ac)], axis=1)
    sin = jnp.concatenate([-jnp.sin(ar), jnp.sin(ar), -jnp.sin(ac), jnp.sin(ac)], axis=1)
    cos = jnp.concatenate([jnp.ones((ctx_len, HEAD_DIM), F32), cos], axis=0)
    sin = jnp.concatenate([jnp.zeros((ctx_len, HEAD_DIM), F32), sin], axis=0)
    return cos, sin


def kernel(x, c, ctx, c_ctx, ada_w, ada_b, pre_norm_w, post_norm_w, w_in, w_out, mlstm_i_bias, mlstm_f_bias,
           ret_log_gamma, attn_q_norm_w, attn_k_norm_w, dn_conv_w, dn_a_log, dn_dt_bias, head_norm_w):
    bsz, seq, d = x.shape
    ctx_len = ctx.shape[1]
    t = ctx_len + seq
    depth = ada_w.shape[0]
    assert bsz <= 4 and ctx_len % CH == 0 and seq % CH == 0

    xa = jnp.concatenate([ctx, x], axis=1)
    c8 = jnp.zeros((8, d), F32).at[:bsz].set(c).at[4].set(c_ctx)
    mod = _modulation(c8, ada_w, ada_b)
    cos_t, sin_t = _rope_tables(seq, ctx_len)

    w_main, w_gate = _reorder_w_in(w_in)
    w_out_b = w_out.astype(BF16)

    for l in range(depth):
        u, g = _in_projection(xa, mod[l], pre_norm_w[l], w_main, w_gate, l, ctx_len)
        zeros8 = jnp.zeros((8,), F32)
        bias_row = jnp.pad(jnp.concatenate([mlstm_i_bias[l].reshape(-1), mlstm_f_bias[l].reshape(-1),
                                            dn_dt_bias[l].reshape(-1), zeros8]), (0, LANES - 32))
        alog_row = jnp.pad(jnp.concatenate([zeros8, zeros8, dn_a_log[l].reshape(-1), zeros8]), (0, LANES - 32))
        a, at_ch, at_dc = _gate_prep(g, bias_row.reshape(1, LANES), alog_row.reshape(1, LANES))
        y_m = _mlstm(u, a, at_ch, ctx_len)
        y_r = _retention(u, ret_log_gamma[l], cos_t, sin_t, ctx_len)
        y_a = _attention(u, cos_t, sin_t, attn_q_norm_w[l], attn_k_norm_w[l], ctx_len, with_ctx=(l < depth - 1))
        y_d = _deltanet(u, dn_conv_w[l], a, at_dc, ctx_len)
        xa = _out_projection(xa, (y_m, y_r, y_a, y_d), u, head_norm_w[l], w_out_b, l, mod[l],
                             post_norm_w[l], ctx_len, latent_only=(l == depth - 1))
    return xa
```

```python
import functools
import math

import jax
import jax.numpy as jnp
from jax import lax
from jax.experimental import pallas as pl
from jax.experimental.pallas import tpu as pltpu

F32 = jnp.float32
BF16 = jnp.bfloat16

D_MODEL = 2048
GROUP_W = 512
N_HEADS = 4
HEAD_DIM = 128
KV_HEADS = 2
GRID_W = 64
CONV_K = 5
ROPE_BASE = 10000.0
EPS = 1e-6
SCALE = HEAD_DIM ** -0.5
LOG2E = 1.4426950408889634
LOG2_SCALE = -0.5 * math.log2(HEAD_DIM)
NEG = -1e30

LANES = 128
CH = 256
DC = 64
PACK = 4
PRE_GROUPS = 8
N_MAIN = 64 * LANES
VMEM_LIMIT = 56 * 1024 * 1024
IN_PROJ_MAX_TM = 1088

CB = dict(m_q=0, m_k=4, m_v=8, m_o=12, m_z=16, r_q=20, r_k=24, r_v=28, r_z=32,
          a_q=36, a_k=40, a_v=42, a_z=44, d_q=48, d_k=52, d_v=56, d_z=60)


def _dot(a, b, prec=None):
    return jnp.dot(a, b, preferred_element_type=F32, precision=prec)


def _dot_nt(a, b, prec=None):
    return lax.dot_general(a, b, (((1,), (1,)), ((), ())), preferred_element_type=F32, precision=prec)


def _dot_tn(a, b, prec=None):
    return lax.dot_general(a, b, (((0,), (0,)), ((), ())), preferred_element_type=F32, precision=prec)


def _split(a):
    hi = a.astype(BF16)
    return hi, (a - hi.astype(F32)).astype(BF16)


def _dot3(a, b):
    n = a[0].shape[0]
    x = _dot(jnp.concatenate([a[0], a[1]], axis=0), b[0])
    return (x[:n] + x[n:]) + _dot(a[0], b[1])


def _sigmoid(x):
    return 1.0 / (1.0 + jnp.exp(-x))


def _lane_col(blk, idx):
    lane = lax.broadcasted_iota(jnp.int32, blk.shape, 1)
    return jnp.sum(jnp.where(lane == idx, blk, 0.0), axis=1, keepdims=True)


def _sub_row(blk, idx):
    sub = lax.broadcasted_iota(jnp.int32, blk.shape, 0)
    return jnp.sum(jnp.where(sub == idx, blk, 0.0), axis=0, keepdims=True)


def _rope(t, cos, sin_signed):
    lane = lax.broadcasted_iota(jnp.int32, t.shape, 1)
    partner = jnp.where((lane // 32) % 2 == 0, pltpu.roll(t, 96, 1), pltpu.roll(t, 32, 1))
    return t * cos + partner * sin_signed


def _params(sem):
    return pltpu.CompilerParams(dimension_semantics=sem, vmem_limit_bytes=VMEM_LIMIT)


def _mod_kernel(c_ref, w_ref, b_ref, o_ref):
    c = c_ref[...]
    o_ref[...] = _dot3(_split(c * _sigmoid(c)), _split(w_ref[...])) + b_ref[...]


def _modulation(c8, ada_w, ada_b):
    depth, d, n3 = ada_w.shape
    tn = 768
    return pl.pallas_call(
        _mod_kernel,
        grid=(depth, n3 // tn),
        in_specs=[pl.BlockSpec((8, d), lambda l, n: (0, 0)),
                  pl.BlockSpec((None, d, tn), lambda l, n: (l, 0, n)),
                  pl.BlockSpec((None, 1, tn), lambda l, n: (l, 0, n))],
        out_specs=pl.BlockSpec((None, 8, tn), lambda l, n: (l, 0, n)),
        out_shape=jax.ShapeDtypeStruct((depth, 8, n3), F32),
        compiler_params=_params(("parallel", "parallel")),
        name="adaln_mod",
    )(c8, ada_w, ada_b.reshape(depth, 1, n3))


def _inproj_kernel(x_ref, mod_ref, pw_ref, wm_ref, wg_ref, u_ref, g_ref, h_scr, *, tm, ctx_len, per_batch,
                   n_blocks, n_tiles):
    m = pl.program_id(0)
    n = pl.program_id(1)

    d = D_MODEL

    def modulated(x, row):
        y = x * lax.rsqrt(jnp.mean(x * x, axis=-1, keepdims=True) + EPS) * pw_ref[...]
        return (y * (1.0 + mod_ref[pl.ds(row, 1), d:2 * d]) + mod_ref[pl.ds(row, 1), 0:d]).astype(BF16)

    def normalise(blk, slot):
        b = blk // per_batch
        head = modulated(x_ref[0:ctx_len, :], jnp.where(blk % per_batch == 0, 4, b))
        hh = jnp.concatenate([head, modulated(x_ref[ctx_len:, :], b)], axis=0)
        h_scr[slot] = hh
        g_ref[...] = _dot(hh, wg_ref[...])

    @pl.when((m == 0) & (n == 0))
    def _():
        normalise(0, 0)

    @pl.when(n < n_tiles - 1)
    def _():
        u_ref[...] = _dot(h_scr[m % 2], wm_ref[...]).astype(BF16)

    @pl.when(n == n_tiles - 1)
    def _():
        u_ref[...] = _dot(h_scr[m % 2], wm_ref[...]).astype(BF16)
        normalise(jnp.minimum(m + 1, n_blocks - 1), (m + 1) % 2)


def _in_projection(xa, mod, pre_w, w_main, w_gate, layer, ctx_len):
    bsz, t, d = xa.shape
    tm = max(c for c in range(16, IN_PROJ_MAX_TM + 1, 16) if t % c == 0)
    assert tm > ctx_len
    tn = 1024
    per_batch = t // tm
    n_blocks = bsz * per_batch
    n_tiles = N_MAIN // tn
    kern = functools.partial(_inproj_kernel, tm=tm, ctx_len=ctx_len, per_batch=per_batch, n_blocks=n_blocks,
                             n_tiles=n_tiles)
    nxt = lambda m: jnp.minimum(m + 1, n_blocks - 1)
    u, g = pl.pallas_call(
        kern,
        grid=(n_blocks, n_tiles),
        in_specs=[pl.BlockSpec((None, tm, d), lambda m, n: (jnp.where((m == 0) & (n == 0), 0, nxt(m)), 0, 0)),
                  pl.BlockSpec((8, 3 * d), lambda m, n: (0, 0)),
                  pl.BlockSpec((1, d), lambda m, n: (0, 0)),
                  pl.BlockSpec((None, d, tn), lambda m, n: (layer, 0, n)),
                  pl.BlockSpec((None, d, LANES), lambda m, n: (layer, 0, 0))],
        out_specs=[pl.BlockSpec((None, tm, tn), lambda m, n: (m, 0, n)),
                   pl.BlockSpec((None, tm, LANES), lambda m, n: (jnp.where(n == n_tiles - 1, nxt(m), m), 0, 0))],
        out_shape=[jax.ShapeDtypeStruct((n_blocks, tm, N_MAIN), BF16),
                   jax.ShapeDtypeStruct((n_blocks, tm, LANES), F32)],
        scratch_shapes=[pltpu.VMEM((2, tm, d), BF16)],
        compiler_params=_params(("arbitrary", "arbitrary")),
        name="in_proj",
    )(xa.reshape(n_blocks, tm, d), mod, pre_w.reshape(1, d), w_main, w_gate)
    return u.reshape(bsz, t, N_MAIN), g.reshape(bsz, t, LANES)


def _gate_kernel(g_ref, bias_ref, alog_ref, a_ref, at_ref, atp_ref, *, n_chunks):
    lane = lax.broadcasted_iota(jnp.int32, (CH, LANES), 1)
    s_idx = lax.broadcasted_iota(jnp.int32, (CH, 4 * CH), 0)
    col = lax.broadcasted_iota(jnp.int32, (CH, 4 * CH), 1)
    kind = col // CH
    t_idx = col % CH
    ahead = jnp.where(kind % 2 == 0, t_idx - s_idx, s_idx - t_idx)
    other_sub = jnp.where(kind < 2, 0, jnp.abs(t_idx // DC - s_idx // DC))
    sums = jnp.where(ahead - CH * other_sub >= 0, 1.0, 0.0).astype(BF16)
    gate = lax.broadcasted_iota(jnp.int32, (32, CH), 0)
    fwd = (gate % 8) < 4
    bias = bias_ref[...]
    neg_ea = -jnp.exp(alog_ref[...])

    def chunk(ci, carry):
        rows = pl.ds(pl.multiple_of(ci * CH, CH), CH)
        x = g_ref[rows, :] + bias
        l1p = jnp.log(1.0 + jnp.exp(-jnp.abs(x)))
        log_f = jnp.minimum(x, 0.0) - l1p
        log_a = neg_ea * (jnp.maximum(x, 0.0) + l1p)
        v = jnp.where(lane < 8, x, jnp.where(lane < 16, log_f, jnp.where(lane < 24, log_a, _sigmoid(x))))
        v_t = v.T[0:32, :]
        v_t = jnp.where(gate < 24, v_t * LOG2E, v_t)
        v1 = v_t.astype(BF16)
        res1 = v_t - v1.astype(F32)
        v2 = res1.astype(BF16)
        v3 = (res1 - v2.astype(F32)).astype(BF16)
        cums = _dot(jnp.concatenate([v1, v2, v3], axis=0), sums)
        cums = cums[0:32] + (cums[32:64] + cums[64:96])
        cum_ch = jnp.where(fwd, cums[:, 0:CH], cums[:, CH:2 * CH])
        cum_dc = jnp.where(fwd, cums[:, 2 * CH:3 * CH], cums[:, 3 * CH:])
        a_t = jnp.where((gate >= 8) & (gate < 16), cum_ch, jnp.where((gate >= 16) & (gate < 24), cum_dc, v_t))
        a_ref[rows, :] = jnp.concatenate([a_t, jnp.zeros((LANES - 32, CH), F32)], axis=0).T
        at_ref[ci] = a_t
        pairs = CH // (2 * DC)
        for p in range(pairs):
            c0 = a_t[:, 2 * p * DC:(2 * p + 1) * DC]
            c1 = a_t[:, (2 * p + 1) * DC:(2 * p + 2) * DC]
            atp_ref[ci * pairs + p] = jnp.concatenate([c0, c0, c1, c1], axis=1)
        return carry

    lax.fori_loop(0, n_chunks, chunk, 0)


def _gate_prep(g, bias_row, alog_row):
    bsz, t, _ = g.shape
    n_chunks = t // CH
    return pl.pallas_call(
        functools.partial(_gate_kernel, n_chunks=n_chunks),
        grid=(bsz,),
        in_specs=[pl.BlockSpec((None, t, LANES), lambda b: (b, 0, 0)),
                  pl.BlockSpec((1, LANES), lambda b: (0, 0)),
                  pl.BlockSpec((1, LANES), lambda b: (0, 0))],
        out_specs=[pl.BlockSpec((None, t, LANES), lambda b: (b, 0, 0)),
                   pl.BlockSpec((None, n_chunks, 32, CH), lambda b: (b, 0, 0, 0)),
                   pl.BlockSpec((None, t // (2 * DC), 32, PACK * DC), lambda b: (b, 0, 0, 0))],
        out_shape=[jax.ShapeDtypeStruct((bsz, t, LANES), F32),
                   jax.ShapeDtypeStruct((bsz, n_chunks, 32, CH), F32),
                   jax.ShapeDtypeStruct((bsz, t // (2 * DC), 32, PACK * DC), F32)],
        compiler_params=_params(("parallel",)),
        name="gate_prep",
    )(g, bias_row, alog_row)


def _chunk_order(j, d, n_chunks, n_ctx_chunks):
    if d == 0:
        return j
    return jnp.where(j < n_ctx_chunks, n_ctx_chunks - 1 - j, n_chunks + n_ctx_chunks - 1 - j)


def _head_spec(t, name):
    base = CB[name]
    return pl.BlockSpec((None, t, LANES), lambda b, h: (b, 0, base + h))


def _mlstm_kernel(q_ref, k_ref, v_ref, o_ref, a_ref, at_ref, y_ref, c_scr, y1_scr, *, n_chunks, n_ctx_chunks):
    h = pl.program_id(1)
    r = lax.broadcasted_iota(jnp.int32, (CH, CH), 0)
    c = lax.broadcasted_iota(jnp.int32, (CH, CH), 1)
    one_col = (lax.broadcasted_iota(jnp.int32, (CH, LANES), 1) == 0).astype(F32)

    masks = ((c <= r), (c >= r))
    lasts = (CH - 1, 0)
    dirs = range(2)
    c_scr[...] = jnp.zeros_like(c_scr)

    def step(j, ms):
        cis = [_chunk_order(j, d, n_chunks, n_ctx_chunks) for d in dirs]
        rows = [pl.ds(pl.multiple_of(ci * CH, CH), CH) for ci in cis]
        qs = [q_ref[rw, :] for rw in rows]
        ks = [k_ref[rw, :] for rw in rows]
        qk = [_dot_nt(qs[d], ks[d]) for d in dirs]
        pc = [_dot(qs[d], c_scr[d].astype(BF16)) for d in dirs]
        a_blk = [a_ref[rw, :] for rw in rows]
        at_blk = [at_ref[ci] for ci in cis]
        li_col = [_lane_col(a_blk[d], d * 4 + h) for d in dirs]
        b_col = [_lane_col(a_blk[d], 8 + d * 4 + h) for d in dirs]
        li_row = [_sub_row(at_blk[d], d * 4 + h) for d in dirs]
        b_row = [_sub_row(at_blk[d], 8 + d * 4 + h) for d in dirs]
        dm = [jnp.where(masks[d], b_col[d] + (li_row[d] - b_row[d]), NEG) for d in dirs]
        inter = [ms[d] + b_col[d] for d in dirs]
        m_t = [jnp.maximum(inter[d], jnp.max(dm[d], axis=1, keepdims=True)) for d in dirs]
        m_s = [m_t[d] - LOG2_SCALE for d in dirs]
        s = [(qk[d] * jnp.exp2(dm[d] - m_s[d])).astype(BF16) for d in dirs]
        vs = [v_ref[rw, :] for rw in rows]
        sv = [_dot(s[d], jnp.concatenate([vs[d], one_col.astype(BF16)], axis=1)) for d in dirs]

        b_end = [b_col[d][lasts[d]:lasts[d] + 1, :] for d in dirs]
        dec = [(b_end[d] - b_col[d]) + li_col[d] for d in dirs]
        m_new = [jnp.maximum(ms[d] + b_end[d], jnp.max(dec[d], axis=0, keepdims=True)) for d in dirs]
        w_state = [jnp.exp2(dec[d] - m_new[d]) for d in dirs]
        wv = [jnp.concatenate([(w_state[d] * vs[d].astype(F32)).astype(BF16), (w_state[d] * one_col).astype(BF16)],
                              axis=1) for d in dirs]
        upd = [_dot_tn(ks[d], wv[d]) for d in dirs]
        for d in dirs:
            c_scr[d] = jnp.exp2(ms[d] + b_end[d] - m_new[d]) * c_scr[d] + upd[d]

        for d in dirs:
            nd = jnp.exp2(inter[d] - m_s[d]) * pc[d] + sv[d]
            h_out = nd[:, :LANES] / jnp.maximum(jnp.abs(nd[:, LANES:LANES + 1]), jnp.exp2(-m_t[d]))
            if d == 0:
                y_ref[rows[d], :] = h_out
            else:
                y1_scr[rows[d], :] = h_out
        return tuple(m_new)

    zero = jnp.zeros((1, 1), F32)
    lax.fori_loop(0, n_chunks, step, (zero, zero))

    def finish(ci, carry):
        rw = pl.ds(pl.multiple_of(ci * CH, CH), CH)
        y_ref[rw, :] = (y_ref[rw, :] + y1_scr[rw, :]) * _sigmoid(o_ref[rw, :].astype(F32))
        return carry

    lax.fori_loop(0, n_chunks, finish, 0)


def _mlstm(u, a, at_ch, ctx_len):
    bsz, t, _ = u.shape
    n_chunks = t // CH
    kern = functools.partial(_mlstm_kernel, n_chunks=n_chunks, n_ctx_chunks=ctx_len // CH)
    return pl.pallas_call(
        kern,
        grid=(bsz, N_HEADS),
        in_specs=[_head_spec(t, "m_q"), _head_spec(t, "m_k"), _head_spec(t, "m_v"), _head_spec(t, "m_o"),
                  pl.BlockSpec((None, t, LANES), lambda b, h: (b, 0, 0)),
                  pl.BlockSpec((None, n_chunks, 32, CH), lambda b, h: (b, 0, 0, 0))],
        out_specs=pl.BlockSpec((None, t, LANES), lambda b, h: (b, 0, h)),
        out_shape=jax.ShapeDtypeStruct((bsz, t, GROUP_W), F32),
        scratch_shapes=[pltpu.VMEM((2, HEAD_DIM, 2 * LANES), F32), pltpu.VMEM((t, LANES), F32)],
        compiler_params=_params(("parallel", "parallel")),
        name="mlstm",
    )(u, u, u, u, a, at_ch)


def _retention_kernel(lg_ref, q_ref, k_ref, v_ref, cos_ref, sin_ref, y_ref, qp_scr, kp_scr, r_scr, y1_scr, *,
                      n_chunks, n_ctx_chunks):
    h = pl.program_id(1)

    def prep(ci, carry):
        rows = pl.ds(pl.multiple_of(ci * CH, CH), CH)
        cos = cos_ref[rows, :]
        sin = sin_ref[rows, :]
        qp_scr[rows, :] = _rope(q_ref[rows, :].astype(F32), cos, sin).astype(BF16)
        kp_scr[rows, :] = (_rope(k_ref[rows, :].astype(F32), cos, sin) * SCALE).astype(BF16)
        return carry

    lax.fori_loop(0, n_chunks, prep, 0)

    r = lax.broadcasted_iota(jnp.int32, (CH, CH), 0).astype(F32)
    c = lax.broadcasted_iota(jnp.int32, (CH, CH), 1).astype(F32)
    pos = lax.broadcasted_iota(jnp.int32, (CH, 1), 0).astype(F32)

    dirs = range(2)
    decay, q_decay, k_decay, chunk_decay = [], [], [], []
    for d in dirs:
        lg = lg_ref[d * N_HEADS + h]
        rel = (r - c) if d == 0 else (c - r)
        decay.append(jnp.where(rel >= 0, jnp.exp(lg * jnp.maximum(rel, 0.0)), 0.0))
        p_vis = pos if d == 0 else (CH - 1.0) - pos
        q_decay.append(jnp.exp(lg * (p_vis + 1.0)))
        k_decay.append(jnp.exp(lg * ((CH - 1.0) - p_vis)))
        chunk_decay.append(jnp.exp(lg * CH))
    r_scr[...] = jnp.zeros_like(r_scr)

    def step(j, carry):
        cis = [_chunk_order(j, d, n_chunks, n_ctx_chunks) for d in dirs]
        rows = [pl.ds(pl.multiple_of(ci * CH, CH), CH) for ci in cis]
        qs = [qp_scr[rw, :] for rw in rows]
        ks = [kp_scr[rw, :] for rw in rows]
        vs = [v_ref[rw, :] for rw in rows]
        qk = [_dot_nt(qs[d], ks[d]) for d in dirs]
        qr = [_dot(qs[d], r_scr[d].astype(BF16)) for d in dirs]
        sv = [_dot((qk[d] * decay[d]).astype(BF16), vs[d]) for d in dirs]
        upd = [_dot_tn((ks[d].astype(F32) * k_decay[d]).astype(BF16), vs[d]) for d in dirs]
        for d in dirs:
            r_scr[d] = chunk_decay[d] * r_scr[d] + upd[d]
        y_ref[rows[0], :] = sv[0] + q_decay[0] * qr[0]
        y1_scr[rows[1], :] = sv[1] + q_decay[1] * qr[1]
        return carry

    lax.fori_loop(0, n_chunks, step, 0)

    def finish(ci, carry):
        rw = pl.ds(pl.multiple_of(ci * CH, CH), CH)
        y_ref[rw, :] = y_ref[rw, :] + y1_scr[rw, :]
        return carry

    lax.fori_loop(0, n_chunks, finish, 0)


def _retention(u, log_gamma, cos_t, sin_t, ctx_len):
    bsz, t, _ = u.shape
    n_chunks = t // CH
    kern = functools.partial(_retention_kernel, n_chunks=n_chunks, n_ctx_chunks=ctx_len // CH)
    tab = pl.BlockSpec((t, LANES), lambda b, h: (0, 0))
    return pl.pallas_call(
        kern,
        grid=(bsz, N_HEADS),
        in_specs=[pl.BlockSpec(memory_space=pltpu.SMEM),
                  _head_spec(t, "r_q"), _head_spec(t, "r_k"), _head_spec(t, "r_v"), tab, tab],
        out_specs=pl.BlockSpec((None, t, LANES), lambda b, h: (b, 0, h)),
        out_shape=jax.ShapeDtypeStruct((bsz, t, GROUP_W), F32),
        scratch_shapes=[pltpu.VMEM((t, LANES), BF16), pltpu.VMEM((t, LANES), BF16),
                        pltpu.VMEM((2, HEAD_DIM, HEAD_DIM), F32), pltpu.VMEM((t, LANES), F32)],
        compiler_params=_params(("parallel", "parallel")),
        name="retention",
    )(log_gamma.reshape(2 * N_HEADS), u, u, u, cos_t, sin_t)


def _attn_kernel(q_ref, k_ref, v_ref, cos_ref, sin_ref, cosq_ref, sinq_ref, qw_ref, kw_ref, o_ref,
                 kt_scr, s_scr, m_scr, *, n_blocks, with_ctx):
    j = pl.program_id(2)

    def load_q():
        qs = []
        for g in range(2):
            q = q_ref[:, g * LANES:(g + 1) * LANES].astype(F32)
            q = q * lax.rsqrt(jnp.mean(q * q, axis=-1, keepdims=True) + EPS) * qw_ref[...]
            qs.append((_rope(q, cosq_ref[...], sinq_ref[...]) * (SCALE * LOG2E)).astype(BF16))
        return jnp.concatenate(qs, axis=0)

    def finish(o, l_part):
        o = o / jnp.sum(l_part, axis=-1, keepdims=True)
        o_ref[:, 0:LANES] = o[:CH]
        o_ref[:, LANES:2 * LANES] = o[CH:]

    @pl.when(j == 0)
    def _():
        def prep(ci, carry):
            rows = pl.ds(pl.multiple_of(ci * CH, CH), CH)
            k = k_ref[rows, :].astype(F32)
            k = k * lax.rsqrt(jnp.mean(k * k, axis=-1, keepdims=True) + EPS) * kw_ref[...]
            kt_scr[ci] = _rope(k, cos_ref[rows, :], sin_ref[rows, :]).T.astype(BF16)
            return carry

        lax.fori_loop(0, n_blocks, prep, 0)
        if with_ctx:
            s = _dot(load_q(), kt_scr[0])
            p = jnp.exp2(s - jnp.max(s, axis=-1, keepdims=True))
            finish(_dot(p.astype(BF16), v_ref[0:CH, :]), p[:, :LANES] + p[:, LANES:])
        else:
            o_ref[...] = jnp.zeros_like(o_ref)

    @pl.when(j >= 1)
    def _():
        qq = load_q()
        mx = None
        for kb in range(n_blocks):
            s = _dot(qq, kt_scr[kb])
            s_scr[:, kb * CH:(kb + 1) * CH] = s
            m2 = jnp.maximum(s[:, :LANES], s[:, LANES:])
            mx = m2 if mx is None else jnp.maximum(mx, m2)
        m_scr[...] = jnp.broadcast_to(jnp.max(mx, axis=-1, keepdims=True), (2 * CH, LANES))

    def value_pass(_, carry):
        m = m_scr[...]
        one_col = (lax.broadcasted_iota(jnp.int32, (CH, LANES), 1) == 0).astype(BF16)
        o = None
        for kb in range(n_blocks):
            cols = slice(kb * CH, (kb + 1) * CH)
            p_lo = jnp.exp2(s_scr[:, kb * CH:kb * CH + LANES] - m)
            p_hi = jnp.exp2(s_scr[:, kb * CH + LANES:(kb + 1) * CH] - m)
            pv = _dot(jnp.concatenate([p_lo, p_hi], axis=1).astype(BF16),
                      jnp.concatenate([v_ref[cols, :], one_col], axis=1))
            o = pv if o is None else o + pv
        finish(o[:, :LANES], o[:, LANES:])
        return carry

    lax.fori_loop(0, jnp.minimum(j, 1), value_pass, 0)


def _attention(u, cos_t, sin_t, qn_w, kn_w, ctx_len, with_ctx):
    bsz, t, _ = u.shape
    assert ctx_len == CH
    n_blocks = t // CH
    kern = functools.partial(_attn_kernel, n_blocks=n_blocks, with_ctx=with_ctx)
    qb = CB["a_q"] // 2
    kb = CB["a_k"]
    vb = CB["a_v"]
    tab = pl.BlockSpec((t, LANES), lambda b, kv, j: (0, 0))
    tabq = pl.BlockSpec((CH, LANES), lambda b, kv, j: (j, 0))
    vec = pl.BlockSpec((1, LANES), lambda b, kv, j: (0, 0))
    return pl.pallas_call(
        kern,
        grid=(bsz, KV_HEADS, n_blocks),
        in_specs=[pl.BlockSpec((None, CH, 2 * LANES), lambda b, kv, j: (b, j, qb + kv)),
                  pl.BlockSpec((None, t, LANES), lambda b, kv, j: (b, 0, kb + kv)),
                  pl.BlockSpec((None, t, LANES), lambda b, kv, j: (b, 0, vb + kv)),
                  tab, tab, tabq, tabq, vec, vec],
        out_specs=pl.BlockSpec((None, CH, 2 * LANES), lambda b, kv, j: (b, j, kv)),
        out_shape=jax.ShapeDtypeStruct((bsz, t, GROUP_W), F32),
        scratch_shapes=[pltpu.VMEM((n_blocks, HEAD_DIM, CH), BF16), pltpu.VMEM((2 * CH, t), F32),
                        pltpu.VMEM((2 * CH, LANES), F32)],
        compiler_params=_params(("parallel", "parallel", "arbitrary")),
        name="attention",
    )(u, u, u, cos_t, sin_t, cos_t, sin_t, qn_w.reshape(1, LANES), kn_w.reshape(1, LANES))


def _block_diag(y, lane_masks):
    return jnp.concatenate([y * mk for mk in lane_masks], axis=0)


def _packed_dot3(a, b, lane_masks):
    n = a[0].shape[0]
    x = _dot(jnp.concatenate([a[0], a[1]], axis=0), _block_diag(b[0], lane_masks))
    return (x[:n] + x[n:]) + _dot(a[0], _block_diag(b[1], lane_masks))


def _tri_inverse(n_mats, r, cl, lane_masks):
    eye = (r == cl).astype(F32)
    diag16 = (r // 16) == (cl // 16)
    ms = [-jnp.where(diag16, n, 0.0) for n in n_mats]
    ps = [eye + m for m in ms]
    mps = [_split(m) for m in ms]
    mps = [_split(_packed_dot3(mp, mp, lane_masks)) for mp in mps]
    for _ in range(2):
        both = [_packed_dot3(tuple(jnp.concatenate([a, b], axis=0) for a, b in zip(_split(p), mp)), mp, lane_masks)
                for p, mp in zip(ps, mps)]
        ps = [p + x[:DC] for p, x in zip(ps, both)]
        mps = [_split(x[DC:]) for x in both]
    ps = [p + _packed_dot3(_split(p), mp, lane_masks) for p, mp in zip(ps, mps)]
    for w in (32, 64):
        off = ((r // w) == (cl // w)) & ((r // (w // 2)) != (cl // (w // 2)))
        pbs = [p.astype(BF16) for p in ps]
        pds = [_block_diag(pb, lane_masks) for pb in pbs]
        tmp = [_dot(jnp.where(off, n, 0.0).astype(BF16), pd).astype(BF16) for n, pd in zip(n_mats, pds)]
        ps = [p - _dot(pb, _block_diag(a, lane_masks)) for p, pb, a in zip(ps, pbs, tmp)]
    return ps


def _deltanet_kernel(q_ref, k_ref, v_ref, wq_ref, wk_ref, wv_ref, a_ref, at_ref, y_ref,
                     xs_scr, qd_scr, kd_scr, vd_scr, aq_scr, b_scr, ge_scr, y1_scr, *,
                     t, ctx_len):
    h = pl.program_id(1)
    n_chunks = t // DC
    n_ctx_chunks = ctx_len // DC
    pad = 8

    streams = ((q_ref, wq_ref, qd_scr, True, SCALE), (k_ref, wk_ref, kd_scr, True, 1.0),
               (v_ref, wv_ref, vd_scr, False, 1.0))
    zeros = jnp.zeros((pad, LANES), F32)
    for si in range(len(streams)):
        xs_scr[si, 0:pad, :] = zeros
        xs_scr[si, ctx_len + pad:ctx_len + 2 * pad, :] = zeros
        xs_scr[si, t + 2 * pad:t + 3 * pad, :] = zeros

    def seg_row(ci):
        return ci * CH + pad + jnp.where(ci * CH >= ctx_len, pad, 0)

    def load(ci, carry):
        rows = pl.ds(pl.multiple_of(ci * CH, CH), CH)
        for si, (src_ref, _, _, _, _) in enumerate(streams):
            xs_scr[si, pl.ds(pl.multiple_of(seg_row(ci), 8), CH), :] = src_ref[rows, :].astype(F32)
        return carry

    lax.fori_loop(0, t // CH, load, 0)

    def conv(ci, carry):
        base = seg_row(ci) - CONV_K // 2
        for si, (_, w_ref, dst_scr, l2, scale) in enumerate(streams):
            acc = xs_scr[si, pl.ds(base, CH), :] * w_ref[0:1, :]
            for j in range(1, CONV_K):
                acc = acc + xs_scr[si, pl.ds(base + j, CH), :] * w_ref[j:j + 1, :]
            acc = acc * _sigmoid(acc)
            if l2:
                acc = acc * lax.rsqrt(jnp.sum(acc * acc, axis=-1, keepdims=True) + EPS) * scale
            dst_scr[pl.ds(pl.multiple_of(ci * CH, CH), CH), :] = acc
        return carry

    lax.fori_loop(0, t // CH, conv, 0)

    wp = PACK * DC
    r = lax.broadcasted_iota(jnp.int32, (DC, wp), 0)
    lane = lax.broadcasted_iota(jnp.int32, (DC, wp), 1)
    blk = lane // DC
    cl = lane % DC
    fwd = (blk % 2) == 0
    ahead = jnp.where(fwd, r - cl, cl - r)
    incl = ahead >= 0
    strict = ahead > 0
    lane_masks = [(blk == i).astype(BF16) for i in range(PACK)]

    def pick(vals):
        out = vals[PACK - 1]
        for i in range(PACK - 2, -1, -1):
            out = jnp.where(blk == i, vals[i], out)
        return out

    def pre(first_pair, n_groups):
        groups = []
        n_mats = []
        for gg in range(n_groups):
            pair = first_pair + gg
            at_blk = at_ref[pair]
            g_row = jnp.where(fwd[0:1, :], _sub_row(at_blk, 16 + h), _sub_row(at_blk, 20 + h))
            kks, qks, g_cols, betas, chains = [], [], [], [], []
            for cc in range(2):
                ci = pair * 2 + cc
                rows = pl.ds(pl.multiple_of(ci * DC, DC), DC)
                q = qd_scr[rows, :]
                k = kd_scr[rows, :]
                v = vd_scr[rows, :]
                a_blk = a_ref[rows, :]
                kb = k.astype(BF16)
                kb2 = jnp.concatenate([kb, kb], axis=0)
                kks.append(_dot_nt(kb, kb2))
                qks.append(_dot_nt(q.astype(BF16), kb2))
                for d in range(2):
                    last = DC - 1 if d == 0 else 0
                    g_col = _lane_col(a_blk, 16 + d * 4 + h)
                    beta = _lane_col(a_blk, 24 + d * 4 + h)
                    eg = jnp.exp2(g_col)
                    g_end = g_col[last:last + 1, :]
                    rhs = jnp.concatenate([beta * v, (beta * eg) * k], axis=1).astype(BF16)
                    ke = (jnp.exp2(g_end - g_col) * k).astype(BF16)
                    ge_scr[d, ci] = jnp.broadcast_to(jnp.exp2(g_end), (8, LANES))
                    g_cols.append(g_col)
                    betas.append(beta)
                    chains.append((d, ci, rows, rhs, ke, eg * q))
            decay = jnp.exp2(jnp.where(incl, pick(g_cols) - g_row, NEG))
            n_mats.append(jnp.where(strict, pick(betas) * decay * jnp.concatenate(kks, axis=1), 0.0))
            qkd = (jnp.concatenate(qks, axis=1) * decay).astype(BF16)
            groups.append((chains, qkd, jnp.concatenate([ch[3] for ch in chains], axis=0)))
        t_invs = [t_inv.astype(BF16) for t_inv in _tri_inverse(n_mats, r, cl, lane_masks)]
        ws = [[_dot(t_inv * mk, rhs_all).astype(BF16) for mk in lane_masks]
              for t_inv, (_, _, rhs_all) in zip(t_invs, groups)]
        w_all = [jnp.concatenate(w4, axis=0) for w4 in ws]
        kws = [[_dot_tn(ch[4], w) for ch, w in zip(chains, w4)] for (chains, _, _), w4 in zip(groups, ws)]
        qws = [[_dot(qkd * mk, wa) for mk in lane_masks] for (_, qkd, _), wa in zip(groups, w_all)]
        for (chains, _, _), kw4, qw4 in zip(groups, kws, qws):
            for (d, ci, rows, _, _, egq), kw, qw in zip(chains, kw4, qw4):
                aq_scr[d, ci, 0:HEAD_DIM, :] = (-kw[:, LANES:]).astype(BF16)
                aq_scr[d, ci, HEAD_DIM:HEAD_DIM + DC, :] = (egq - qw[:, LANES:]).astype(BF16)
                b_scr[d, ci] = kw[:, :LANES]
                if d == 0:
                    y_ref[rows, :] = qw[:, :LANES]
                else:
                    y1_scr[rows, :] = qw[:, :LANES]

    n_pairs = n_chunks // 2
    n_trips, tail = divmod(n_pairs, PRE_GROUPS)

    def pre_trip(i, carry):
        pre(i * PRE_GROUPS, PRE_GROUPS)
        return carry

    lax.fori_loop(0, n_trips, pre_trip, 0)
    if tail:
        pre(n_trips * PRE_GROUPS, tail)

    def scan(j, carry):
        cis = [_chunk_order(j, d, n_chunks, n_ctx_chunks) for d in range(2)]
        xs = [_dot(aq_scr[d, cis[d]], carry[d].astype(BF16)) for d in range(2)]
        new = tuple(ge_scr[d, cis[d]][0:1, :] * carry[d] + (xs[d][:HEAD_DIM, :] + b_scr[d, cis[d]])
                    for d in range(2))
        rows = [pl.ds(pl.multiple_of(cis[d] * DC, DC), DC) for d in range(2)]
        y_ref[rows[0], :] = y_ref[rows[0], :] + xs[0][HEAD_DIM:, :]
        y1_scr[rows[1], :] = y1_scr[rows[1], :] + xs[1][HEAD_DIM:, :]
        return new

    zero = jnp.zeros((HEAD_DIM, HEAD_DIM), F32)
    lax.fori_loop(0, n_chunks, scan, (zero, zero))

    def add(ci, carry):
        rows = pl.ds(pl.multiple_of(ci * CH, CH), CH)
        y_ref[rows, :] = y_ref[rows, :] + y1_scr[rows, :]
        return carry

    lax.fori_loop(0, t // CH, add, 0)


def _deltanet(u, conv_w, a, at_dc, ctx_len):
    bsz, t, _ = u.shape
    n_chunks = t // DC
    kern = functools.partial(_deltanet_kernel, t=t, ctx_len=ctx_len)

    def wspec(off):
        return pl.BlockSpec((CONV_K, LANES), lambda b, h: (0, off + h))

    return pl.pallas_call(
        kern,
        grid=(bsz, N_HEADS),
        in_specs=[_head_spec(t, "d_q"), _head_spec(t, "d_k"), _head_spec(t, "d_v"),
                  wspec(0), wspec(N_HEADS), wspec(2 * N_HEADS),
                  pl.BlockSpec((None, t, LANES), lambda b, h: (b, 0, 0)),
                  pl.BlockSpec((None, n_chunks // 2, 32, PACK * DC), lambda b, h: (b, 0, 0, 0))],
        out_specs=pl.BlockSpec((None, t, LANES), lambda b, h: (b, 0, h)),
        out_shape=jax.ShapeDtypeStruct((bsz, t, GROUP_W), F32),
        scratch_shapes=[pltpu.VMEM((3, t + 24, LANES), F32),
                        pltpu.VMEM((t, LANES), F32), pltpu.VMEM((t, LANES), F32), pltpu.VMEM((t, LANES), F32),
                        pltpu.VMEM((2, n_chunks, HEAD_DIM + DC, LANES), BF16),
                        pltpu.VMEM((2, n_chunks, HEAD_DIM, LANES), F32),
                        pltpu.VMEM((2, n_chunks, 8, LANES), F32),
                        pltpu.VMEM((t, LANES), F32)],
        compiler_params=_params(("parallel", "parallel")),
        name="deltanet",
    )(u, u, u, conv_w, conv_w, conv_w, a, at_dc)


def _out_kernel(x_ref, ym_ref, yr_ref, ya_ref, yd_ref, zm_ref, zr_ref, za_ref, zd_ref, hn_ref, w_ref,
                mod_ref, pw_ref, o_ref, *, first_block):
    b = pl.program_id(0)
    i = pl.program_id(1) + first_block

    def silu(z_ref):
        z = z_ref[...].astype(F32)
        return z * _sigmoid(z)

    def head_rms(y_ref, g):
        y = y_ref[...]
        parts = []
        for hh in range(N_HEADS):
            p = y[:, hh * LANES:(hh + 1) * LANES]
            parts.append(p * lax.rsqrt(jnp.mean(p * p, axis=-1, keepdims=True) + EPS))
        return jnp.concatenate(parts, axis=1) * hn_ref[:, g * GROUP_W:(g + 1) * GROUP_W]

    merged = jnp.concatenate([
        (head_rms(ym_ref, 0) * silu(zm_ref)).astype(BF16),
        (head_rms(yr_ref, 1) * silu(zr_ref)).astype(BF16),
        (ya_ref[...] * silu(za_ref)).astype(BF16),
        (head_rms(yd_ref, 2) * silu(zd_ref)).astype(BF16)], axis=1)
    o = _dot(merged, w_ref[...])
    o = o * lax.rsqrt(jnp.mean(o * o, axis=-1, keepdims=True) + EPS) * pw_ref[...]
    row = jnp.where(i == 0, 4, b)
    gate = mod_ref[pl.ds(row, 1), 2 * D_MODEL:3 * D_MODEL]
    o_ref[...] = x_ref[...] + gate * o


def _out_projection(xa, ys, u, hn_w, w_out, layer, mod, post_w, ctx_len, latent_only):
    bsz, t, d = xa.shape
    assert ctx_len == CH
    first = 1 if latent_only else 0
    tok = lambda b, i: (b, i + first, 0)
    yspec = pl.BlockSpec((None, CH, GROUP_W), tok)

    def zspec(name):
        blk = CB[name] // N_HEADS
        return pl.BlockSpec((None, CH, GROUP_W), lambda b, i: (b, i + first, blk))

    return pl.pallas_call(
        functools.partial(_out_kernel, first_block=first),
        grid=(bsz, t // CH - first),
        in_specs=[pl.BlockSpec((None, CH, d), tok), yspec, yspec, yspec, yspec,
                  zspec("m_z"), zspec("r_z"), zspec("a_z"), zspec("d_z"),
                  pl.BlockSpec((1, 3 * GROUP_W), lambda b, i: (0, 0)),
                  pl.BlockSpec((None, d, d), lambda b, i: (layer, 0, 0)),
                  pl.BlockSpec((8, 3 * d), lambda b, i: (0, 0)),
                  pl.BlockSpec((1, d), lambda b, i: (0, 0))],
        out_specs=pl.BlockSpec((None, CH, d), lambda b, i: (b, i, 0)),
        out_shape=jax.ShapeDtypeStruct((bsz, t - first * CH, d), F32),
        compiler_params=_params(("parallel", "parallel")),
        name="out_proj",
    )(xa, *ys, u, u, u, u, hn_w.reshape(1, 3 * GROUP_W), w_out, mod, post_w.reshape(1, d))


M_GATE_COL = 5 * GROUP_W
N_M_GATES = 4 * N_HEADS
D_GATE_COL = N_MAIN + N_M_GATES
W_TILE = 512


def _wprep_kernel(a_ref, b_ref, o_ref):
    n = pl.program_id(1)

    @pl.when(n < M_GATE_COL // W_TILE)
    def _():
        o_ref[...] = a_ref[...]

    @pl.when(n >= M_GATE_COL // W_TILE)
    def _():
        o_ref[...] = jnp.concatenate([a_ref[:, N_M_GATES:], b_ref[:, :N_M_GATES]], axis=1)


def _reorder_w_in(w_in):
    depth, d, _ = w_in.shape
    assert M_GATE_COL % W_TILE == 0
    wb = w_in.astype(BF16)
    w_main = pl.pallas_call(
        _wprep_kernel,
        grid=(depth, N_MAIN // W_TILE),
        in_specs=[pl.BlockSpec((None, d, W_TILE), lambda l, n: (l, 0, n)),
                  pl.BlockSpec((None, d, LANES), lambda l, n: (l, 0, (n + 1) * (W_TILE // LANES)))],
        out_specs=pl.BlockSpec((None, d, W_TILE), lambda l, n: (l, 0, n)),
        out_shape=jax.ShapeDtypeStruct((depth, d, N_MAIN), BF16),
        compiler_params=_params(("parallel", "parallel")),
        name="w_in_prep",
    )(wb, wb)
    gates = jnp.concatenate([wb[:, :, M_GATE_COL:M_GATE_COL + N_M_GATES], wb[:, :, D_GATE_COL:]], axis=2)
    w_gate = jnp.pad(gates, ((0, 0), (0, 0), (0, LANES - gates.shape[2])))
    return w_main, w_gate


def _rope_tables(seq, ctx_len):
    rows = seq // GRID_W
    row = jnp.repeat(jnp.arange(rows), GRID_W)
    col = jnp.tile(jnp.arange(GRID_W), rows)
    n_freq = HEAD_DIM // 4
    inv_freq = ROPE_BASE ** (-jnp.arange(n_freq, dtype=F32) / n_freq)
    ar = row[:, None] * inv_freq
    ac = col[:, None] * inv_freq
    cos = jnp.concatenate([jnp.cos(ar), jnp.cos(ar), jnp.cos(ac), jnp.cos(ac)], axis=1)
    sin = jnp.concatenate([-jnp.sin(ar), jnp.sin(ar), -jnp.sin(ac), jnp.sin(ac)], axis=1)
    cos = jnp.concatenate([jnp.ones((ctx_len, HEAD_DIM), F32), cos], axis=0)
    sin = jnp.concatenate([jnp.zeros((ctx_len, HEAD_DIM), F32), sin], axis=0)
    return cos, sin


def kernel(x, c, ctx, c_ctx, ada_w, ada_b, pre_norm_w, post_norm_w, w_in, w_out, mlstm_i_bias, mlstm_f_bias,
           ret_log_gamma, attn_q_norm_w, attn_k_norm_w, dn_conv_w, dn_a_log, dn_dt_bias, head_norm_w):
    bsz, seq, d = x.shape
    ctx_len = ctx.shape[1]
    t = ctx_len + seq
    depth = ada_w.shape[0]
    assert bsz <= 4 and ctx_len % CH == 0 and seq % CH == 0

    xa = jnp.concatenate([ctx, x], axis=1)
    c8 = jnp.zeros((8, d), F32).at[:bsz].set(c).at[4].set(c_ctx)
    mod = _modulation(c8, ada_w, ada_b)
    cos_t, sin_t = _rope_tables(seq, ctx_len)

    w_main, w_gate = _reorder_w_in(w_in)
    w_out_b = w_out.astype(BF16)

    for l in range(depth):
        u, g = _in_projection(xa, mod[l], pre_norm_w[l], w_main, w_gate, l, ctx_len)
        zeros8 = jnp.zeros((8,), F32)
        bias_row = jnp.pad(jnp.concatenate([mlstm_i_bias[l].reshape(-1), mlstm_f_bias[l].reshape(-1),
                                            dn_dt_bias[l].reshape(-1), zeros8]), (0, LANES - 32))
        alog_row = jnp.pad(jnp.concatenate([zeros8, zeros8, dn_a_log[l].reshape(-1), zeros8]), (0, LANES - 32))
        a, at_ch, at_dc = _gate_prep(g, bias_row.reshape(1, LANES), alog_row.reshape(1, LANES))
        y_m = _mlstm(u, a, at_ch, ctx_len)
        y_r = _retention(u, ret_log_gamma[l], cos_t, sin_t, ctx_len)
        y_a = _attention(u, cos_t, sin_t, attn_q_norm_w[l], attn_k_norm_w[l], ctx_len, with_ctx=(l < depth - 1))
        y_d = _deltanet(u, dn_conv_w[l], a, at_dc, ctx_len)
        xa = _out_projection(xa, (y_m, y_r, y_a, y_d), u, head_norm_w[l], w_out_b, l, mod[l],
                             post_norm_w[l], ctx_len, latent_only=(l == depth - 1))
    return xa
```

```python
import functools
import math

import jax
import jax.numpy as jnp
from jax import lax
from jax.experimental import pallas as pl
from jax.experimental.pallas import tpu as pltpu

F32 = jnp.float32
BF16 = jnp.bfloat16

D_MODEL = 2048
GROUP_W = 512
N_HEADS = 4
HEAD_DIM = 128
KV_HEADS = 2
GRID_W = 64
CONV_K = 5
ROPE_BASE = 10000.0
EPS = 1e-6
SCALE = HEAD_DIM ** -0.5
LOG2E = 1.4426950408889634
LOG2_SCALE = -0.5 * math.log2(HEAD_DIM)
NEG = -1e30

LANES = 128
CH = 256
DC = 64
PACK = 4
PRE_GROUPS = 8
N_MAIN = 64 * LANES
VMEM_LIMIT = 56 * 1024 * 1024
IN_PROJ_MAX_TM = 1088

CB = dict(m_q=0, m_k=4, m_v=8, m_o=12, m_z=16, r_q=20, r_k=24, r_v=28, r_z=32,
          a_q=36, a_k=40, a_v=42, a_z=44, d_q=48, d_k=52, d_v=56, d_z=60)


def _dot(a, b, prec=None):
    return jnp.dot(a, b, preferred_element_type=F32, precision=prec)


def _dot_nt(a, b, prec=None):
    return lax.dot_general(a, b, (((1,), (1,)), ((), ())), preferred_element_type=F32, precision=prec)


def _dot_tn(a, b, prec=None):
    return lax.dot_general(a, b, (((0,), (0,)), ((), ())), preferred_element_type=F32, precision=prec)


def _split(a):
    hi = a.astype(BF16)
    return hi, (a - hi.astype(F32)).astype(BF16)


def _dot3(a, b):
    n = a[0].shape[0]
    x = _dot(jnp.concatenate([a[0], a[1]], axis=0), b[0])
    return (x[:n] + x[n:]) + _dot(a[0], b[1])


def _sigmoid(x):
    return 1.0 / (1.0 + jnp.exp(-x))


def _lane_col(blk, idx):
    lane = lax.broadcasted_iota(jnp.int32, blk.shape, 1)
    return jnp.sum(jnp.where(lane == idx, blk, 0.0), axis=1, keepdims=True)


def _sub_row(blk, idx):
    sub = lax.broadcasted_iota(jnp.int32, blk.shape, 0)
    return jnp.sum(jnp.where(sub == idx, blk, 0.0), axis=0, keepdims=True)


def _rope(t, cos, sin_signed):
    lane = lax.broadcasted_iota(jnp.int32, t.shape, 1)
    partner = jnp.where((lane // 32) % 2 == 0, pltpu.roll(t, 96, 1), pltpu.roll(t, 32, 1))
    return t * cos + partner * sin_signed


def _params(sem):
    return pltpu.CompilerParams(dimension_semantics=sem, vmem_limit_bytes=VMEM_LIMIT)


def _mod_kernel(c_ref, w_ref, b_ref, o_ref):
    c = c_ref[...]
    o_ref[...] = _dot3(_split(c * _sigmoid(c)), _split(w_ref[...])) + b_ref[...]


def _modulation(c8, ada_w, ada_b):
    depth, d, n3 = ada_w.shape
    tn = 768
    return pl.pallas_call(
        _mod_kernel,
        grid=(depth, n3 // tn),
        in_specs=[pl.BlockSpec((8, d), lambda l, n: (0, 0)),
                  pl.BlockSpec((None, d, tn), lambda l, n: (l, 0, n)),
                  pl.BlockSpec((None, 1, tn), lambda l, n: (l, 0, n))],
        out_specs=pl.BlockSpec((None, 8, tn), lambda l, n: (l, 0, n)),
        out_shape=jax.ShapeDtypeStruct((depth, 8, n3), F32),
        compiler_params=_params(("parallel", "parallel")),
        name="adaln_mod",
    )(c8, ada_w, ada_b.reshape(depth, 1, n3))


def _inproj_kernel(x_ref, mod_ref, pw_ref, wm_ref, wg_ref, u_ref, g_ref, h_scr, *, tm, ctx_len, per_batch,
                   n_blocks, n_tiles):
    m = pl.program_id(0)
    n = pl.program_id(1)

    d = D_MODEL

    def modulated(x, row):
        y = x * lax.rsqrt(jnp.mean(x * x, axis=-1, keepdims=True) + EPS) * pw_ref[...]
        return (y * (1.0 + mod_ref[pl.ds(row, 1), d:2 * d]) + mod_ref[pl.ds(row, 1), 0:d]).astype(BF16)

    def normalise(blk, slot):
        b = blk // per_batch
        head = modulated(x_ref[0:ctx_len, :], jnp.where(blk % per_batch == 0, 4, b))
        hh = jnp.concatenate([head, modulated(x_ref[ctx_len:, :], b)], axis=0)
        h_scr[slot] = hh
        g_ref[...] = _dot(hh, wg_ref[...])

    @pl.when((m == 0) & (n == 0))
    def _():
        normalise(0, 0)

    @pl.when(n < n_tiles - 1)
    def _():
        u_ref[...] = _dot(h_scr[m % 2], wm_ref[...]).astype(BF16)

    @pl.when(n == n_tiles - 1)
    def _():
        u_ref[...] = _dot(h_scr[m % 2], wm_ref[...]).astype(BF16)
        normalise(jnp.minimum(m + 1, n_blocks - 1), (m + 1) % 2)


def _in_projection(xa, mod, pre_w, w_main, w_gate, layer, ctx_len):
    bsz, t, d = xa.shape
    tm = max(c for c in range(16, IN_PROJ_MAX_TM + 1, 16) if t % c == 0)
    assert tm > ctx_len
    tn = 1024
    per_batch = t // tm
    n_blocks = bsz * per_batch
    n_tiles = N_MAIN // tn
    kern = functools.partial(_inproj_kernel, tm=tm, ctx_len=ctx_len, per_batch=per_batch, n_blocks=n_blocks,
                             n_tiles=n_tiles)
    nxt = lambda m: jnp.minimum(m + 1, n_blocks - 1)
    u, g = pl.pallas_call(
        kern,
        grid=(n_blocks, n_tiles),
        in_specs=[pl.BlockSpec((None, tm, d), lambda m, n: (jnp.where((m == 0) & (n == 0), 0, nxt(m)), 0, 0)),
                  pl.BlockSpec((8, 3 * d), lambda m, n: (0, 0)),
                  pl.BlockSpec((1, d), lambda m, n: (0, 0)),
                  pl.BlockSpec((None, d, tn), lambda m, n: (layer, 0, n)),
                  pl.BlockSpec((None, d, LANES), lambda m, n: (layer, 0, 0))],
        out_specs=[pl.BlockSpec((None, tm, tn), lambda m, n: (m, 0, n)),
                   pl.BlockSpec((None, tm, LANES), lambda m, n: (jnp.where(n == n_tiles - 1, nxt(m), m), 0, 0))],
        out_shape=[jax.ShapeDtypeStruct((n_blocks, tm, N_MAIN), BF16),
                   jax.ShapeDtypeStruct((n_blocks, tm, LANES), F32)],
        scratch_shapes=[pltpu.VMEM((2, tm, d), BF16)],
        compiler_params=_params(("arbitrary", "arbitrary")),
        name="in_proj",
    )(xa.reshape(n_blocks, tm, d), mod, pre_w.reshape(1, d), w_main, w_gate)
    return u.reshape(bsz, t, N_MAIN), g.reshape(bsz, t, LANES)


def _gate_kernel(g_ref, bias_ref, alog_ref, a_ref, at_ref, atp_ref, *, n_chunks):
    lane = lax.broadcasted_iota(jnp.int32, (CH, LANES), 1)
    s_idx = lax.broadcasted_iota(jnp.int32, (CH, 4 * CH), 0)
    col = lax.broadcasted_iota(jnp.int32, (CH, 4 * CH), 1)
    kind = col // CH
    t_idx = col % CH
    ahead = jnp.where(kind % 2 == 0, t_idx - s_idx, s_idx - t_idx)
    other_sub = jnp.where(kind < 2, 0, jnp.abs(t_idx // DC - s_idx // DC))
    sums = jnp.where(ahead - CH * other_sub >= 0, 1.0, 0.0).astype(BF16)
    gate = lax.broadcasted_iota(jnp.int32, (32, CH), 0)
    fwd = (gate % 8) < 4
    bias = bias_ref[...]
    neg_ea = -jnp.exp(alog_ref[...])

    def chunk(ci, carry):
        rows = pl.ds(pl.multiple_of(ci * CH, CH), CH)
        x = g_ref[rows, :] + bias
        l1p = jnp.log(1.0 + jnp.exp(-jnp.abs(x)))
        log_f = jnp.minimum(x, 0.0) - l1p
        log_a = neg_ea * (jnp.maximum(x, 0.0) + l1p)
        v = jnp.where(lane < 8, x, jnp.where(lane < 16, log_f, jnp.where(lane < 24, log_a, _sigmoid(x))))
        v_t = v.T[0:32, :]
        v_t = jnp.where(gate < 24, v_t * LOG2E, v_t)
        v1 = v_t.astype(BF16)
        res1 = v_t - v1.astype(F32)
        v2 = res1.astype(BF16)
        v3 = (res1 - v2.astype(F32)).astype(BF16)
        cums = _dot(jnp.concatenate([v1, v2, v3], axis=0), sums)
        cums = cums[0:32] + (cums[32:64] + cums[64:96])
        cum_ch = jnp.where(fwd, cums[:, 0:CH], cums[:, CH:2 * CH])
        cum_dc = jnp.where(fwd, cums[:, 2 * CH:3 * CH], cums[:, 3 * CH:])
        a_t = jnp.where((gate >= 8) & (gate < 16), cum_ch, jnp.where((gate >= 16) & (gate < 24), cum_dc, v_t))
        a_ref[rows, :] = jnp.concatenate([a_t, jnp.zeros((LANES - 32, CH), F32)], axis=0).T
        at_ref[ci] = a_t
        pairs = CH // (2 * DC)
        for p in range(pairs):
            c0 = a_t[:, 2 * p * DC:(2 * p + 1) * DC]
            c1 = a_t[:, (2 * p + 1) * DC:(2 * p + 2) * DC]
            atp_ref[ci * pairs + p] = jnp.concatenate([c0, c0, c1, c1], axis=1)
        return carry

    lax.fori_loop(0, n_chunks, chunk, 0)


def _gate_prep(g, bias_row, alog_row):
    bsz, t, _ = g.shape
    n_chunks = t // CH
    return pl.pallas_call(
        functools.partial(_gate_kernel, n_chunks=n_chunks),
        grid=(bsz,),
        in_specs=[pl.BlockSpec((None, t, LANES), lambda b: (b, 0, 0)),
                  pl.BlockSpec((1, LANES), lambda b: (0, 0)),
                  pl.BlockSpec((1, LANES), lambda b: (0, 0))],
        out_specs=[pl.BlockSpec((None, t, LANES), lambda b: (b, 0, 0)),
                   pl.BlockSpec((None, n_chunks, 32, CH), lambda b: (b, 0, 0, 0)),
                   pl.BlockSpec((None, t // (2 * DC), 32, PACK * DC), lambda b: (b, 0, 0, 0))],
        out_shape=[jax.ShapeDtypeStruct((bsz, t, LANES), F32),
                   jax.ShapeDtypeStruct((bsz, n_chunks, 32, CH), F32),
                   jax.ShapeDtypeStruct((bsz, t // (2 * DC), 32, PACK * DC), F32)],
        compiler_params=_params(("parallel",)),
        name="gate_prep",
    )(g, bias_row, alog_row)


def _chunk_order(j, d, n_chunks, n_ctx_chunks):
    if d == 0:
        return j
    return jnp.where(j < n_ctx_chunks, n_ctx_chunks - 1 - j, n_chunks + n_ctx_chunks - 1 - j)


def _head_spec(t, name):
    base = CB[name]
    return pl.BlockSpec((None, t, LANES), lambda b, h: (b, 0, base + h))


def _mlstm_kernel(q_ref, k_ref, v_ref, o_ref, a_ref, at_ref, y_ref, c_scr, y1_scr, *, n_chunks, n_ctx_chunks):
    h = pl.program_id(1)
    r = lax.broadcasted_iota(jnp.int32, (CH, CH), 0)
    c = lax.broadcasted_iota(jnp.int32, (CH, CH), 1)
    one_col = (lax.broadcasted_iota(jnp.int32, (CH, LANES), 1) == 0).astype(F32)

    masks = ((c <= r), (c >= r))
    lasts = (CH - 1, 0)
    dirs = range(2)
    c_scr[...] = jnp.zeros_like(c_scr)

    def step(j, ms):
        cis = [_chunk_order(j, d, n_chunks, n_ctx_chunks) for d in dirs]
        rows = [pl.ds(pl.multiple_of(ci * CH, CH), CH) for ci in cis]
        qs = [q_ref[rw, :] for rw in rows]
        ks = [k_ref[rw, :] for rw in rows]
        qk = [_dot_nt(qs[d], ks[d]) for d in dirs]
        pc = [_dot(qs[d], c_scr[d].astype(BF16)) for d in dirs]
        a_blk = [a_ref[rw, :] for rw in rows]
        at_blk = [at_ref[ci] for ci in cis]
        li_col = [_lane_col(a_blk[d], d * 4 + h) for d in dirs]
        b_col = [_lane_col(a_blk[d], 8 + d * 4 + h) for d in dirs]
        li_row = [_sub_row(at_blk[d], d * 4 + h) for d in dirs]
        b_row = [_sub_row(at_blk[d], 8 + d * 4 + h) for d in dirs]
        dm = [jnp.where(masks[d], b_col[d] + (li_row[d] - b_row[d]), NEG) for d in dirs]
        inter = [ms[d] + b_col[d] for d in dirs]
        m_t = [jnp.maximum(inter[d], jnp.max(dm[d], axis=1, keepdims=True)) for d in dirs]
        m_s = [m_t[d] - LOG2_SCALE for d in dirs]
        s = [(qk[d] * jnp.exp2(dm[d] - m_s[d])).astype(BF16) for d in dirs]
        vs = [v_ref[rw, :] for rw in rows]
        sv = [_dot(s[d], jnp.concatenate([vs[d], one_col.astype(BF16)], axis=1)) for d in dirs]

        b_end = [b_col[d][lasts[d]:lasts[d] + 1, :] for d in dirs]
        dec = [(b_end[d] - b_col[d]) + li_col[d] for d in dirs]
        m_new = [jnp.maximum(ms[d] + b_end[d], jnp.max(dec[d], axis=0, keepdims=True)) for d in dirs]
        w_state = [jnp.exp2(dec[d] - m_new[d]) for d in dirs]
        wv = [jnp.concatenate([(w_state[d] * vs[d].astype(F32)).astype(BF16), (w_state[d] * one_col).astype(BF16)],
                              axis=1) for d in dirs]
        upd = [_dot_tn(ks[d], wv[d]) for d in dirs]
        for d in dirs:
            c_scr[d] = jnp.exp2(ms[d] + b_end[d] - m_new[d]) * c_scr[d] + upd[d]

        for d in dirs:
            nd = jnp.exp2(inter[d] - m_s[d]) * pc[d] + sv[d]
            h_out = nd[:, :LANES] / jnp.maximum(jnp.abs(nd[:, LANES:LANES + 1]), jnp.exp2(-m_t[d]))
            if d == 0:
                y_ref[rows[d], :] = h_out
            else:
                y1_scr[rows[d], :] = h_out
        return tuple(m_new)

    zero = jnp.zeros((1, 1), F32)
    lax.fori_loop(0, n_chunks, step, (zero, zero))

    def finish(ci, carry):
        rw = pl.ds(pl.multiple_of(ci * CH, CH), CH)
        y_ref[rw, :] = (y_ref[rw, :] + y1_scr[rw, :]) * _sigmoid(o_ref[rw, :].astype(F32))
        return carry

    lax.fori_loop(0, n_chunks, finish, 0)


def _mlstm(u, a, at_ch, ctx_len):
    bsz, t, _ = u.shape
    n_chunks = t // CH
    kern = functools.partial(_mlstm_kernel, n_chunks=n_chunks, n_ctx_chunks=ctx_len // CH)
    return pl.pallas_call(
        kern,
        grid=(bsz, N_HEADS),
        in_specs=[_head_spec(t, "m_q"), _head_spec(t, "m_k"), _head_spec(t, "m_v"), _head_spec(t, "m_o"),
                  pl.BlockSpec((None, t, LANES), lambda b, h: (b, 0, 0)),
                  pl.BlockSpec((None, n_chunks, 32, CH), lambda b, h: (b, 0, 0, 0))],
        out_specs=pl.BlockSpec((None, t, LANES), lambda b, h: (b, 0, h)),
        out_shape=jax.ShapeDtypeStruct((bsz, t, GROUP_W), F32),
        scratch_shapes=[pltpu.VMEM((2, HEAD_DIM, 2 * LANES), F32), pltpu.VMEM((t, LANES), F32)],
        compiler_params=_params(("parallel", "parallel")),
        name="mlstm",
    )(u, u, u, u, a, at_ch)


def _retention_kernel(lg_ref, q_ref, k_ref, v_ref, cos_ref, sin_ref, y_ref, qp_scr, kp_scr, r_scr, y1_scr, *,
                      n_chunks, n_ctx_chunks):
    h = pl.program_id(1)

    def prep(ci, carry):
        rows = pl.ds(pl.multiple_of(ci * CH, CH), CH)
        cos = cos_ref[rows, :]
        sin = sin_ref[rows, :]
        qp_scr[rows, :] = _rope(q_ref[rows, :].astype(F32), cos, sin).astype(BF16)
        kp_scr[rows, :] = (_rope(k_ref[rows, :].astype(F32), cos, sin) * SCALE).astype(BF16)
        return carry

    lax.fori_loop(0, n_chunks, prep, 0)

    r = lax.broadcasted_iota(jnp.int32, (CH, CH), 0).astype(F32)
    c = lax.broadcasted_iota(jnp.int32, (CH, CH), 1).astype(F32)
    pos = lax.broadcasted_iota(jnp.int32, (CH, 1), 0).astype(F32)

    dirs = range(2)
    decay, q_decay, k_decay, chunk_decay = [], [], [], []
    for d in dirs:
        lg = lg_ref[d * N_HEADS + h]
        rel = (r - c) if d == 0 else (c - r)
        decay.append(jnp.where(rel >= 0, jnp.exp(lg * jnp.maximum(rel, 0.0)), 0.0))
        p_vis = pos if d == 0 else (CH - 1.0) - pos
        q_decay.append(jnp.exp(lg * (p_vis + 1.0)))
        k_decay.append(jnp.exp(lg * ((CH - 1.0) - p_vis)))
        chunk_decay.append(jnp.exp(lg * CH))
    r_scr[...] = jnp.zeros_like(r_scr)

    def step(j, carry):
        cis = [_chunk_order(j, d, n_chunks, n_ctx_chunks) for d in dirs]
        rows = [pl.ds(pl.multiple_of(ci * CH, CH), CH) for ci in cis]
        qs = [qp_scr[rw, :] for rw in rows]
        ks = [kp_scr[rw, :] for rw in rows]
        vs = [v_ref[rw, :] for rw in rows]
        qk = [_dot_nt(qs[d], ks[d]) for d in dirs]
        qr = [_dot(qs[d], r_scr[d].astype(BF16)) for d in dirs]
        sv = [_dot((qk[d] * decay[d]).astype(BF16), vs[d]) for d in dirs]
        upd = [_dot_tn((ks[d].astype(F32) * k_decay[d]).astype(BF16), vs[d]) for d in dirs]
        for d in dirs:
            r_scr[d] = chunk_decay[d] * r_scr[d] + upd[d]
        y_ref[rows[0], :] = sv[0] + q_decay[0] * qr[0]
        y1_scr[rows[1], :] = sv[1] + q_decay[1] * qr[1]
        return carry

    lax.fori_loop(0, n_chunks, step, 0)

    def finish(ci, carry):
        rw = pl.ds(pl.multiple_of(ci * CH, CH), CH)
        y_ref[rw, :] = y_ref[rw, :] + y1_scr[rw, :]
        return carry

    lax.fori_loop(0, n_chunks, finish, 0)


def _retention(u, log_gamma, cos_t, sin_t, ctx_len):
    bsz, t, _ = u.shape
    n_chunks = t // CH
    kern = functools.partial(_retention_kernel, n_chunks=n_chunks, n_ctx_chunks=ctx_len // CH)
    tab = pl.BlockSpec((t, LANES), lambda b, h: (0, 0))
    return pl.pallas_call(
        kern,
        grid=(bsz, N_HEADS),
        in_specs=[pl.BlockSpec(memory_space=pltpu.SMEM),
                  _head_spec(t, "r_q"), _head_spec(t, "r_k"), _head_spec(t, "r_v"), tab, tab],
        out_specs=pl.BlockSpec((None, t, LANES), lambda b, h: (b, 0, h)),
        out_shape=jax.ShapeDtypeStruct((bsz, t, GROUP_W), F32),
        scratch_shapes=[pltpu.VMEM((t, LANES), BF16), pltpu.VMEM((t, LANES), BF16),
                        pltpu.VMEM((2, HEAD_DIM, HEAD_DIM), F32), pltpu.VMEM((t, LANES), F32)],
        compiler_params=_params(("parallel", "parallel")),
        name="retention",
    )(log_gamma.reshape(2 * N_HEADS), u, u, u, cos_t, sin_t)


def _attn_kernel(q_ref, k_ref, v_ref, cos_ref, sin_ref, cosq_ref, sinq_ref, qw_ref, kw_ref, o_ref,
                 kt_scr, s_scr, m_scr, *, n_blocks, with_ctx):
    j = pl.program_id(2)

    def load_q():
        qs = []
        for g in range(2):
            q = q_ref[:, g * LANES:(g + 1) * LANES].astype(F32)
            q = q * lax.rsqrt(jnp.mean(q * q, axis=-1, keepdims=True) + EPS) * qw_ref[...]
            qs.append((_rope(q, cosq_ref[...], sinq_ref[...]) * (SCALE * LOG2E)).astype(BF16))
        return jnp.concatenate(qs, axis=0)

    def finish(o, l_part):
        o = o / jnp.sum(l_part, axis=-1, keepdims=True)
        o_ref[:, 0:LANES] = o[:CH]
        o_ref[:, LANES:2 * LANES] = o[CH:]

    @pl.when(j == 0)
    def _():
        def prep(ci, carry):
            rows = pl.ds(pl.multiple_of(ci * CH, CH), CH)
            k = k_ref[rows, :].astype(F32)
            k = k * lax.rsqrt(jnp.mean(k * k, axis=-1, keepdims=True) + EPS) * kw_ref[...]
            kt_scr[ci] = _rope(k, cos_ref[rows, :], sin_ref[rows, :]).T.astype(BF16)
            return carry

        lax.fori_loop(0, n_blocks, prep, 0)
        if with_ctx:
            s = _dot(load_q(), kt_scr[0])
            p = jnp.exp2(s - jnp.max(s, axis=-1, keepdims=True))
            finish(_dot(p.astype(BF16), v_ref[0:CH, :]), p[:, :LANES] + p[:, LANES:])
        else:
            o_ref[...] = jnp.zeros_like(o_ref)

    @pl.when(j >= 1)
    def _():
        qq = load_q()
        mx = None
        for kb in range(n_blocks):
            s = _dot(qq, kt_scr[kb])
            s_scr[:, kb * CH:(kb + 1) * CH] = s
            m2 = jnp.maximum(s[:, :LANES], s[:, LANES:])
            mx = m2 if mx is None else jnp.maximum(mx, m2)
        m_scr[...] = jnp.broadcast_to(jnp.max(mx, axis=-1, keepdims=True), (2 * CH, LANES))

    def value_pass(_, carry):
        m = m_scr[...]
        one_col = (lax.broadcasted_iota(jnp.int32, (CH, LANES), 1) == 0).astype(BF16)
        o = None
        for kb in range(n_blocks):
            cols = slice(kb * CH, (kb + 1) * CH)
            p_lo = jnp.exp2(s_scr[:, kb * CH:kb * CH + LANES] - m)
            p_hi = jnp.exp2(s_scr[:, kb * CH + LANES:(kb + 1) * CH] - m)
            pv = _dot(jnp.concatenate([p_lo, p_hi], axis=1).astype(BF16),
                      jnp.concatenate([v_ref[cols, :], one_col], axis=1))
            o = pv if o is None else o + pv
        finish(o[:, :LANES], o[:, LANES:])
        return carry

    lax.fori_loop(0, jnp.minimum(j, 1), value_pass, 0)


def _attention(u, cos_t, sin_t, qn_w, kn_w, ctx_len, with_ctx):
    bsz, t, _ = u.shape
    assert ctx_len == CH
    n_blocks = t // CH
    kern = functools.partial(_attn_kernel, n_blocks=n_blocks, with_ctx=with_ctx)
    qb = CB["a_q"] // 2
    kb = CB["a_k"]
    vb = CB["a_v"]
    tab = pl.BlockSpec((t, LANES), lambda b, kv, j: (0, 0))
    tabq = pl.BlockSpec((CH, LANES), lambda b, kv, j: (j, 0))
    vec = pl.BlockSpec((1, LANES), lambda b, kv, j: (0, 0))
    return pl.pallas_call(
        kern,
        grid=(bsz, KV_HEADS, n_blocks),
        in_specs=[pl.BlockSpec((None, CH, 2 * LANES), lambda b, kv, j: (b, j, qb + kv)),
                  pl.BlockSpec((None, t, LANES), lambda b, kv, j: (b, 0, kb + kv)),
                  pl.BlockSpec((None, t, LANES), lambda b, kv, j: (b, 0, vb + kv)),
                  tab, tab, tabq, tabq, vec, vec],
        out_specs=pl.BlockSpec((None, CH, 2 * LANES), lambda b, kv, j: (b, j, kv)),
        out_shape=jax.ShapeDtypeStruct((bsz, t, GROUP_W), F32),
        scratch_shapes=[pltpu.VMEM((n_blocks, HEAD_DIM, CH), BF16), pltpu.VMEM((2 * CH, t), F32),
                        pltpu.VMEM((2 * CH, LANES), F32)],
        compiler_params=_params(("parallel", "parallel", "arbitrary")),
        name="attention",
    )(u, u, u, cos_t, sin_t, cos_t, sin_t, qn_w.reshape(1, LANES), kn_w.reshape(1, LANES))


def _block_diag(y, lane_masks):
    return jnp.concatenate([y * mk for mk in lane_masks], axis=0)


def _packed_dot3(a, b, lane_masks):
    n = a[0].shape[0]
    x = _dot(jnp.concatenate([a[0], a[1]], axis=0), _block_diag(b[0], lane_masks))
    return (x[:n] + x[n:]) + _dot(a[0], _block_diag(b[1], lane_masks))


def _tri_inverse(n_mats, r, cl, lane_masks):
    eye = (r == cl).astype(F32)
    diag16 = (r // 16) == (cl // 16)
    ms = [-jnp.where(diag16, n, 0.0) for n in n_mats]
    ps = [eye + m for m in ms]
    mps = [_split(m) for m in ms]
    mps = [_split(_packed_dot3(mp, mp, lane_masks)) for mp in mps]
    for _ in range(2):
        both = [_packed_dot3(tuple(jnp.concatenate([a, b], axis=0) for a, b in zip(_split(p), mp)), mp, lane_masks)
                for p, mp in zip(ps, mps)]
        ps = [p + x[:DC] for p, x in zip(ps, both)]
        mps = [_split(x[DC:]) for x in both]
    ps = [p + _packed_dot3(_split(p), mp, lane_masks) for p, mp in zip(ps, mps)]
    for w in (32, 64):
        off = ((r // w) == (cl // w)) & ((r // (w // 2)) != (cl // (w // 2)))
        pbs = [p.astype(BF16) for p in ps]
        pds = [_block_diag(pb, lane_masks) for pb in pbs]
        tmp = [_dot(jnp.where(off, n, 0.0).astype(BF16), pd).astype(BF16) for n, pd in zip(n_mats, pds)]
        ps = [p - _dot(pb, _block_diag(a, lane_masks)) for p, pb, a in zip(ps, pbs, tmp)]
    return ps


def _deltanet_kernel(q_ref, k_ref, v_ref, wq_ref, wk_ref, wv_ref, a_ref, at_ref, y_ref,
                     xs_scr, qd_scr, kd_scr, vd_scr, aq_scr, b_scr, ge_scr, y1_scr, *,
                     t, ctx_len):
    h = pl.program_id(1)
    n_chunks = t // DC
    n_ctx_chunks = ctx_len // DC
    pad = 8

    streams = ((q_ref, wq_ref, qd_scr, True, SCALE), (k_ref, wk_ref, kd_scr, True, 1.0),
               (v_ref, wv_ref, vd_scr, False, 1.0))
    zeros = jnp.zeros((pad, LANES), F32)
    for si in range(len(streams)):
        xs_scr[si, 0:pad, :] = zeros
        xs_scr[si, ctx_len + pad:ctx_len + 2 * pad, :] = zeros
        xs_scr[si, t + 2 * pad:t + 3 * pad, :] = zeros

    def seg_row(ci):
        return ci * CH + pad + jnp.where(ci * CH >= ctx_len, pad, 0)

    def load(ci, carry):
        rows = pl.ds(pl.multiple_of(ci * CH, CH), CH)
        for si, (src_ref, _, _, _, _) in enumerate(streams):
            xs_scr[si, pl.ds(pl.multiple_of(seg_row(ci), 8), CH), :] = src_ref[rows, :].astype(F32)
        return carry

    lax.fori_loop(0, t // CH, load, 0)

    def conv(ci, carry):
        base = seg_row(ci) - CONV_K // 2
        for si, (_, w_ref, dst_scr, l2, scale) in enumerate(streams):
            acc = xs_scr[si, pl.ds(base, CH), :] * w_ref[0:1, :]
            for j in range(1, CONV_K):
                acc = acc + xs_scr[si, pl.ds(base + j, CH), :] * w_ref[j:j + 1, :]
            acc = acc * _sigmoid(acc)
            if l2:
                acc = acc * lax.rsqrt(jnp.sum(acc * acc, axis=-1, keepdims=True) + EPS) * scale
            dst_scr[pl.ds(pl.multiple_of(ci * CH, CH), CH), :] = acc
        return carry

    lax.fori_loop(0, t // CH, conv, 0)

    wp = PACK * DC
    r = lax.broadcasted_iota(jnp.int32, (DC, wp), 0)
    lane = lax.broadcasted_iota(jnp.int32, (DC, wp), 1)
    blk = lane // DC
    cl = lane % DC
    fwd = (blk % 2) == 0
    ahead = jnp.where(fwd, r - cl, cl - r)
    incl = ahead >= 0
    strict = ahead > 0
    lane_masks = [(blk == i).astype(BF16) for i in range(PACK)]

    def pick(vals):
        out = vals[PACK - 1]
        for i in range(PACK - 2, -1, -1):
            out = jnp.where(blk == i, vals[i], out)
        return out

    def pre(first_pair, n_groups):
        groups = []
        n_mats = []
        for gg in range(n_groups):
            pair = first_pair + gg
            at_blk = at_ref[pair]
            g_row = jnp.where(fwd[0:1, :], _sub_row(at_blk, 16 + h), _sub_row(at_blk, 20 + h))
            kks, qks, g_cols, betas, chains = [], [], [], [], []
            for cc in range(2):
                ci = pair * 2 + cc
                rows = pl.ds(pl.multiple_of(ci * DC, DC), DC)
                q = qd_scr[rows, :]
                k = kd_scr[rows, :]
                v = vd_scr[rows, :]
                a_blk = a_ref[rows, :]
                kb = k.astype(BF16)
                kb2 = jnp.concatenate([kb, kb], axis=0)
                kks.append(_dot_nt(kb, kb2))
                qks.append(_dot_nt(q.astype(BF16), kb2))
                for d in range(2):
                    last = DC - 1 if d == 0 else 0
                    g_col = _lane_col(a_blk, 16 + d * 4 + h)
                    beta = _lane_col(a_blk, 24 + d * 4 + h)
                    eg = jnp.exp2(g_col)
                    g_end = g_col[last:last + 1, :]
                    rhs = jnp.concatenate([beta * v, (beta * eg) * k], axis=1).astype(BF16)
                    ke = (jnp.exp2(g_end - g_col) * k).astype(BF16)
                    ge_scr[d, ci] = jnp.broadcast_to(jnp.exp2(g_end), (8, LANES))
                    g_cols.append(g_col)
                    betas.append(beta)
                    chains.append((d, ci, rows, rhs, ke, eg * q))
            decay = jnp.exp2(jnp.where(incl, pick(g_cols) - g_row, NEG))
            n_mats.append(jnp.where(strict, pick(betas) * decay * jnp.concatenate(kks, axis=1), 0.0))
            qkd = (jnp.concatenate(qks, axis=1) * decay).astype(BF16)
            groups.append((chains, qkd, jnp.concatenate([ch[3] for ch in chains], axis=0)))
        t_invs = [t_inv.astype(BF16) for t_inv in _tri_inverse(n_mats, r, cl, lane_masks)]
        ws = [[_dot(t_inv * mk, rhs_all).astype(BF16) for mk in lane_masks]
              for t_inv, (_, _, rhs_all) in zip(t_invs, groups)]
        w_all = [jnp.concatenate(w4, axis=0) for w4 in ws]
        kws = [[_dot_tn(ch[4], w) for ch, w in zip(chains, w4)] for (chains, _, _), w4 in zip(groups, ws)]
        qws = [[_dot(qkd * mk, wa) for mk in lane_masks] for (_, qkd, _), wa in zip(groups, w_all)]
        for (chains, _, _), kw4, qw4 in zip(groups, kws, qws):
            for (d, ci, rows, _, _, egq), kw, qw in zip(chains, kw4, qw4):
                aq_scr[d, ci, 0:HEAD_DIM, :] = (-kw[:, LANES:]).astype(BF16)
                aq_scr[d, ci, HEAD_DIM:HEAD_DIM + DC, :] = (egq - qw[:, LANES:]).astype(BF16)
                b_scr[d, ci] = kw[:, :LANES]
                if d == 0:
                    y_ref[rows, :] = qw[:, :LANES]
                else:
                    y1_scr[rows, :] = qw[:, :LANES]

    n_pairs = n_chunks // 2
    n_trips = max(1, n_pairs // PRE_GROUPS)
    base, extra = divmod(n_pairs, n_trips)

    def trips(first_pair, count, size):
        def body(i, carry):
            pre(first_pair + i * size, size)
            return carry

        if count:
            lax.fori_loop(0, count, body, 0)

    trips(0, extra, base + 1)
    trips(extra * (base + 1), n_trips - extra, base)

    def scan(j, carry):
        cis = [_chunk_order(j, d, n_chunks, n_ctx_chunks) for d in range(2)]
        xs = [_dot(aq_scr[d, cis[d]], carry[d].astype(BF16)) for d in range(2)]
        new = tuple(ge_scr[d, cis[d]][0:1, :] * carry[d] + (xs[d][:HEAD_DIM, :] + b_scr[d, cis[d]])
                    for d in range(2))
        rows = [pl.ds(pl.multiple_of(cis[d] * DC, DC), DC) for d in range(2)]
        y_ref[rows[0], :] = y_ref[rows[0], :] + xs[0][HEAD_DIM:, :]
        y1_scr[rows[1], :] = y1_scr[rows[1], :] + xs[1][HEAD_DIM:, :]
        return new

    zero = jnp.zeros((HEAD_DIM, HEAD_DIM), F32)
    lax.fori_loop(0, n_chunks, scan, (zero, zero))

    def add(ci, carry):
        rows = pl.ds(pl.multiple_of(ci * CH, CH), CH)
        y_ref[rows, :] = y_ref[rows, :] + y1_scr[rows, :]
        return carry

    lax.fori_loop(0, t // CH, add, 0)


def _deltanet(u, conv_w, a, at_dc, ctx_len):
    bsz, t, _ = u.shape
    n_chunks = t // DC
    kern = functools.partial(_deltanet_kernel, t=t, ctx_len=ctx_len)

    def wspec(off):
        return pl.BlockSpec((CONV_K, LANES), lambda b, h: (0, off + h))

    return pl.pallas_call(
        kern,
        grid=(bsz, N_HEADS),
        in_specs=[_head_spec(t, "d_q"), _head_spec(t, "d_k"), _head_spec(t, "d_v"),
                  wspec(0), wspec(N_HEADS), wspec(2 * N_HEADS),
                  pl.BlockSpec((None, t, LANES), lambda b, h: (b, 0, 0)),
                  pl.BlockSpec((None, n_chunks // 2, 32, PACK * DC), lambda b, h: (b, 0, 0, 0))],
        out_specs=pl.BlockSpec((None, t, LANES), lambda b, h: (b, 0, h)),
        out_shape=jax.ShapeDtypeStruct((bsz, t, GROUP_W), F32),
        scratch_shapes=[pltpu.VMEM((3, t + 24, LANES), F32),
                        pltpu.VMEM((t, LANES), F32), pltpu.VMEM((t, LANES), F32), pltpu.VMEM((t, LANES), F32),
                        pltpu.VMEM((2, n_chunks, HEAD_DIM + DC, LANES), BF16),
                        pltpu.VMEM((2, n_chunks, HEAD_DIM, LANES), F32),
                        pltpu.VMEM((2, n_chunks, 8, LANES), F32),
                        pltpu.VMEM((t, LANES), F32)],
        compiler_params=_params(("parallel", "parallel")),
        name="deltanet",
    )(u, u, u, conv_w, conv_w, conv_w, a, at_dc)


def _out_kernel(x_ref, ym_ref, yr_ref, ya_ref, yd_ref, zm_ref, zr_ref, za_ref, zd_ref, hn_ref, w_ref,
                mod_ref, pw_ref, o_ref, *, first_block):
    b = pl.program_id(0)
    i = pl.program_id(1) + first_block

    def silu(z_ref):
        z = z_ref[...].astype(F32)
        return z * _sigmoid(z)

    def head_rms(y_ref, g):
        y = y_ref[...]
        parts = []
        for hh in range(N_HEADS):
            p = y[:, hh * LANES:(hh + 1) * LANES]
            parts.append(p * lax.rsqrt(jnp.mean(p * p, axis=-1, keepdims=True) + EPS))
        return jnp.concatenate(parts, axis=1) * hn_ref[:, g * GROUP_W:(g + 1) * GROUP_W]

    merged = jnp.concatenate([
        (head_rms(ym_ref, 0) * silu(zm_ref)).astype(BF16),
        (head_rms(yr_ref, 1) * silu(zr_ref)).astype(BF16),
        (ya_ref[...] * silu(za_ref)).astype(BF16),
        (head_rms(yd_ref, 2) * silu(zd_ref)).astype(BF16)], axis=1)
    o = _dot(merged, w_ref[...])
    o = o * lax.rsqrt(jnp.mean(o * o, axis=-1, keepdims=True) + EPS) * pw_ref[...]
    row = jnp.where(i == 0, 4, b)
    gate = mod_ref[pl.ds(row, 1), 2 * D_MODEL:3 * D_MODEL]
    o_ref[...] = x_ref[...] + gate * o


def _out_projection(xa, ys, u, hn_w, w_out, layer, mod, post_w, ctx_len, latent_only):
    bsz, t, d = xa.shape
    assert ctx_len == CH
    first = 1 if latent_only else 0
    tok = lambda b, i: (b, i + first, 0)
    yspec = pl.BlockSpec((None, CH, GROUP_W), tok)

    def zspec(name):
        blk = CB[name] // N_HEADS
        return pl.BlockSpec((None, CH, GROUP_W), lambda b, i: (b, i + first, blk))

    return pl.pallas_call(
        functools.partial(_out_kernel, first_block=first),
        grid=(bsz, t // CH - first),
        in_specs=[pl.BlockSpec((None, CH, d), tok), yspec, yspec, yspec, yspec,
                  zspec("m_z"), zspec("r_z"), zspec("a_z"), zspec("d_z"),
                  pl.BlockSpec((1, 3 * GROUP_W), lambda b, i: (0, 0)),
                  pl.BlockSpec((None, d, d), lambda b, i: (layer, 0, 0)),
                  pl.BlockSpec((8, 3 * d), lambda b, i: (0, 0)),
                  pl.BlockSpec((1, d), lambda b, i: (0, 0))],
        out_specs=pl.BlockSpec((None, CH, d), lambda b, i: (b, i, 0)),
        out_shape=jax.ShapeDtypeStruct((bsz, t - first * CH, d), F32),
        compiler_params=_params(("parallel", "parallel")),
        name="out_proj",
    )(xa, *ys, u, u, u, u, hn_w.reshape(1, 3 * GROUP_W), w_out, mod, post_w.reshape(1, d))


M_GATE_COL = 5 * GROUP_W
N_M_GATES = 4 * N_HEADS
D_GATE_COL = N_MAIN + N_M_GATES
W_TILE = 512


def _wprep_kernel(a_ref, b_ref, o_ref):
    n = pl.program_id(1)

    @pl.when(n < M_GATE_COL // W_TILE)
    def _():
        o_ref[...] = a_ref[...]

    @pl.when(n >= M_GATE_COL // W_TILE)
    def _():
        o_ref[...] = jnp.concatenate([a_ref[:, N_M_GATES:], b_ref[:, :N_M_GATES]], axis=1)


def _reorder_w_in(w_in):
    depth, d, _ = w_in.shape
    assert M_GATE_COL % W_TILE == 0
    wb = w_in.astype(BF16)
    w_main = pl.pallas_call(
        _wprep_kernel,
        grid=(depth, N_MAIN // W_TILE),
        in_specs=[pl.BlockSpec((None, d, W_TILE), lambda l, n: (l, 0, n)),
                  pl.BlockSpec((None, d, LANES), lambda l, n: (l, 0, (n + 1) * (W_TILE // LANES)))],
        out_specs=pl.BlockSpec((None, d, W_TILE), lambda l, n: (l, 0, n)),
        out_shape=jax.ShapeDtypeStruct((depth, d, N_MAIN), BF16),
        compiler_params=_params(("parallel", "parallel")),
        name="w_in_prep",
    )(wb, wb)
    gates = jnp.concatenate([wb[:, :, M_GATE_COL:M_GATE_COL + N_M_GATES], wb[:, :, D_GATE_COL:]], axis=2)
    w_gate = jnp.pad(gates, ((0, 0), (0, 0), (0, LANES - gates.shape[2])))
    return w_main, w_gate


def _rope_tables(seq, ctx_len):
    rows = seq // GRID_W
    row = jnp.repeat(jnp.arange(rows), GRID_W)
    col = jnp.tile(jnp.arange(GRID_W), rows)
    n_freq = HEAD_DIM // 4
    inv_freq = ROPE_BASE ** (-jnp.arange(n_freq, dtype=F32) / n_freq)
    ar = row[:, None] * inv_freq
    ac = col[:, None] * inv_freq
    cos = jnp.concatenate([jnp.cos(ar), jnp.cos(ar), jnp.cos(ac), jnp.cos(ac)], axis=1)
    sin = jnp.concatenate([-jnp.sin(ar), jnp.sin(ar), -jnp.sin(ac), jnp.sin(ac)], axis=1)
    cos = jnp.concatenate([jnp.ones((ctx_len, HEAD_DIM), F32), cos], axis=0)
    sin = jnp.concatenate([jnp.zeros((ctx_len, HEAD_DIM), F32), sin], axis=0)
    return cos, sin


def kernel(x, c, ctx, c_ctx, ada_w, ada_b, pre_norm_w, post_norm_w, w_in, w_out, mlstm_i_bias, mlstm_f_bias,
           ret_log_gamma, attn_q_norm_w, attn_k_norm_w, dn_conv_w, dn_a_log, dn_dt_bias, head_norm_w):
    bsz, seq, d = x.shape
    ctx_len = ctx.shape[1]
    t = ctx_len + seq
    depth = ada_w.shape[0]
    assert bsz <= 4 and ctx_len % CH == 0 and seq % CH == 0

    xa = jnp.concatenate([ctx, x], axis=1)
    c8 = jnp.zeros((8, d), F32).at[:bsz].set(c).at[4].set(c_ctx)
    mod = _modulation(c8, ada_w, ada_b)
    cos_t, sin_t = _rope_tables(seq, ctx_len)

    w_main, w_gate = _reorder_w_in(w_in)
    w_out_b = w_out.astype(BF16)

    for l in range(depth):
        u, g = _in_projection(xa, mod[l], pre_norm_w[l], w_main, w_gate, l, ctx_len)
        zeros8 = jnp.zeros((8,), F32)
        bias_row = jnp.pad(jnp.concatenate([mlstm_i_bias[l].reshape(-1), mlstm_f_bias[l].reshape(-1),
                                            dn_dt_bias[l].reshape(-1), zeros8]), (0, LANES - 32))
        alog_row = jnp.pad(jnp.concatenate([zeros8, zeros8, dn_a_log[l].reshape(-1), zeros8]), (0, LANES - 32))
        a, at_ch, at_dc = _gate_prep(g, bias_row.reshape(1, LANES), alog_row.reshape(1, LANES))
        y_m = _mlstm(u, a, at_ch, ctx_len)
        y_r = _retention(u, ret_log_gamma[l], cos_t, sin_t, ctx_len)
        y_a = _attention(u, cos_t, sin_t, attn_q_norm_w[l], attn_k_norm_w[l], ctx_len, with_ctx=(l < depth - 1))
        y_d = _deltanet(u, dn_conv_w[l], a, at_dc, ctx_len)
        xa = _out_projection(xa, (y_m, y_r, y_a, y_d), u, head_norm_w[l], w_out_b, l, mod[l],
                             post_norm_w[l], ctx_len, latent_only=(l == depth - 1))
    return xa
```

```python
import functools
import math

import jax
import jax.numpy as jnp
from jax import lax
from jax.experimental import pallas as pl
from jax.experimental.pallas import tpu as pltpu

F32 = jnp.float32
BF16 = jnp.bfloat16

D_MODEL = 2048
GROUP_W = 512
N_HEADS = 4
HEAD_DIM = 128
KV_HEADS = 2
GRID_W = 64
CONV_K = 5
ROPE_BASE = 10000.0
EPS = 1e-6
SCALE = HEAD_DIM ** -0.5
LOG2E = 1.4426950408889634
LOG2_SCALE = -0.5 * math.log2(HEAD_DIM)
NEG = -1e30

LANES = 128
CH = 256
DC = 64
MLSTM_HEADS = 2
PACK = 4
PRE_GROUPS = 8
N_MAIN = 64 * LANES
VMEM_LIMIT = 56 * 1024 * 1024
IN_PROJ_MAX_TM = 1088

CB = dict(m_q=0, m_k=4, m_v=8, m_o=12, m_z=16, r_q=20, r_k=24, r_v=28, r_z=32,
          a_q=36, a_k=40, a_v=42, a_z=44, d_q=48, d_k=52, d_v=56, d_z=60)


def _dot(a, b, prec=None):
    return jnp.dot(a, b, preferred_element_type=F32, precision=prec)


def _dot_nt(a, b, prec=None):
    return lax.dot_general(a, b, (((1,), (1,)), ((), ())), preferred_element_type=F32, precision=prec)


def _dot_tn(a, b, prec=None):
    return lax.dot_general(a, b, (((0,), (0,)), ((), ())), preferred_element_type=F32, precision=prec)


def _split(a):
    hi = a.astype(BF16)
    return hi, (a - hi.astype(F32)).astype(BF16)


def _dot3(a, b):
    n = a[0].shape[0]
    x = _dot(jnp.concatenate([a[0], a[1]], axis=0), b[0])
    return (x[:n] + x[n:]) + _dot(a[0], b[1])


def _sigmoid(x):
    return 1.0 / (1.0 + jnp.exp(-x))


def _lane_col(blk, idx):
    lane = lax.broadcasted_iota(jnp.int32, blk.shape, 1)
    return jnp.sum(jnp.where(lane == idx, blk, 0.0), axis=1, keepdims=True)


def _sub_row(blk, idx):
    sub = lax.broadcasted_iota(jnp.int32, blk.shape, 0)
    return jnp.sum(jnp.where(sub == idx, blk, 0.0), axis=0, keepdims=True)


def _rope(t, cos, sin_signed):
    lane = lax.broadcasted_iota(jnp.int32, t.shape, 1)
    partner = jnp.where((lane // 32) % 2 == 0, pltpu.roll(t, 96, 1), pltpu.roll(t, 32, 1))
    return t * cos + partner * sin_signed


def _params(sem):
    return pltpu.CompilerParams(dimension_semantics=sem, vmem_limit_bytes=VMEM_LIMIT)


def _mod_kernel(c_ref, w_ref, b_ref, o_ref):
    c = c_ref[...]
    o_ref[...] = _dot3(_split(c * _sigmoid(c)), _split(w_ref[...])) + b_ref[...]


def _modulation(c8, ada_w, ada_b):
    depth, d, n3 = ada_w.shape
    tn = 768
    return pl.pallas_call(
        _mod_kernel,
        grid=(depth, n3 // tn),
        in_specs=[pl.BlockSpec((8, d), lambda l, n: (0, 0)),
                  pl.BlockSpec((None, d, tn), lambda l, n: (l, 0, n)),
                  pl.BlockSpec((None, 1, tn), lambda l, n: (l, 0, n))],
        out_specs=pl.BlockSpec((None, 8, tn), lambda l, n: (l, 0, n)),
        out_shape=jax.ShapeDtypeStruct((depth, 8, n3), F32),
        compiler_params=_params(("parallel", "parallel")),
        name="adaln_mod",
    )(c8, ada_w, ada_b.reshape(depth, 1, n3))


def _inproj_kernel(x_ref, mod_ref, pw_ref, wm_ref, wg_ref, u_ref, g_ref, h_scr, *, tm, ctx_len, per_batch,
                   n_blocks, n_tiles):
    m = pl.program_id(0)
    n = pl.program_id(1)

    d = D_MODEL

    def modulated(x, row):
        y = x * lax.rsqrt(jnp.mean(x * x, axis=-1, keepdims=True) + EPS) * pw_ref[...]
        return (y * (1.0 + mod_ref[pl.ds(row, 1), d:2 * d]) + mod_ref[pl.ds(row, 1), 0:d]).astype(BF16)

    def normalise(blk, slot):
        b = blk // per_batch
        head = modulated(x_ref[0:ctx_len, :], jnp.where(blk % per_batch == 0, 4, b))
        hh = jnp.concatenate([head, modulated(x_ref[ctx_len:, :], b)], axis=0)
        h_scr[slot] = hh
        g_ref[...] = _dot(hh, wg_ref[...])

    @pl.when((m == 0) & (n == 0))
    def _():
        normalise(0, 0)

    @pl.when(n < n_tiles - 1)
    def _():
        u_ref[...] = _dot(h_scr[m % 2], wm_ref[...]).astype(BF16)

    @pl.when(n == n_tiles - 1)
    def _():
        u_ref[...] = _dot(h_scr[m % 2], wm_ref[...]).astype(BF16)
        normalise(jnp.minimum(m + 1, n_blocks - 1), (m + 1) % 2)


def _in_projection(xa, mod, pre_w, w_main, w_gate, layer, ctx_len):
    bsz, t, d = xa.shape
    tm = max(c for c in range(16, IN_PROJ_MAX_TM + 1, 16) if t % c == 0)
    assert tm > ctx_len
    tn = 1024
    per_batch = t // tm
    n_blocks = bsz * per_batch
    n_tiles = N_MAIN // tn
    kern = functools.partial(_inproj_kernel, tm=tm, ctx_len=ctx_len, per_batch=per_batch, n_blocks=n_blocks,
                             n_tiles=n_tiles)
    nxt = lambda m: jnp.minimum(m + 1, n_blocks - 1)
    u, g = pl.pallas_call(
        kern,
        grid=(n_blocks, n_tiles),
        in_specs=[pl.BlockSpec((None, tm, d), lambda m, n: (jnp.where((m == 0) & (n == 0), 0, nxt(m)), 0, 0)),
                  pl.BlockSpec((8, 3 * d), lambda m, n: (0, 0)),
                  pl.BlockSpec((1, d), lambda m, n: (0, 0)),
                  pl.BlockSpec((None, d, tn), lambda m, n: (layer, 0, n)),
                  pl.BlockSpec((None, d, LANES), lambda m, n: (layer, 0, 0))],
        out_specs=[pl.BlockSpec((None, tm, tn), lambda m, n: (m, 0, n)),
                   pl.BlockSpec((None, tm, LANES), lambda m, n: (jnp.where(n == n_tiles - 1, nxt(m), m), 0, 0))],
        out_shape=[jax.ShapeDtypeStruct((n_blocks, tm, N_MAIN), BF16),
                   jax.ShapeDtypeStruct((n_blocks, tm, LANES), F32)],
        scratch_shapes=[pltpu.VMEM((2, tm, d), BF16)],
        compiler_params=_params(("arbitrary", "arbitrary")),
        name="in_proj",
    )(xa.reshape(n_blocks, tm, d), mod, pre_w.reshape(1, d), w_main, w_gate)
    return u.reshape(bsz, t, N_MAIN), g.reshape(bsz, t, LANES)


def _gate_kernel(g_ref, bias_ref, alog_ref, a_ref, at_ref, atp_ref, *, n_chunks):
    lane = lax.broadcasted_iota(jnp.int32, (CH, LANES), 1)
    s_idx = lax.broadcasted_iota(jnp.int32, (CH, 4 * CH), 0)
    col = lax.broadcasted_iota(jnp.int32, (CH, 4 * CH), 1)
    kind = col // CH
    t_idx = col % CH
    ahead = jnp.where(kind % 2 == 0, t_idx - s_idx, s_idx - t_idx)
    other_sub = jnp.where(kind < 2, 0, jnp.abs(t_idx // DC - s_idx // DC))
    sums = jnp.where(ahead - CH * other_sub >= 0, 1.0, 0.0).astype(BF16)
    gate = lax.broadcasted_iota(jnp.int32, (32, CH), 0)
    fwd = (gate % 8) < 4
    bias = bias_ref[...]
    neg_ea = -jnp.exp(alog_ref[...])

    def chunk(ci, carry):
        rows = pl.ds(pl.multiple_of(ci * CH, CH), CH)
        x = g_ref[rows, :] + bias
        l1p = jnp.log(1.0 + jnp.exp(-jnp.abs(x)))
        log_f = jnp.minimum(x, 0.0) - l1p
        log_a = neg_ea * (jnp.maximum(x, 0.0) + l1p)
        v = jnp.where(lane < 8, x, jnp.where(lane < 16, log_f, jnp.where(lane < 24, log_a, _sigmoid(x))))
        v_t = v.T[0:32, :]
        v_t = jnp.where(gate < 24, v_t * LOG2E, v_t)
        v1 = v_t.astype(BF16)
        res1 = v_t - v1.astype(F32)
        v2 = res1.astype(BF16)
        v3 = (res1 - v2.astype(F32)).astype(BF16)
        cums = _dot(jnp.concatenate([v1, v2, v3], axis=0), sums)
        cums = cums[0:32] + (cums[32:64] + cums[64:96])
        cum_ch = jnp.where(fwd, cums[:, 0:CH], cums[:, CH:2 * CH])
        cum_dc = jnp.where(fwd, cums[:, 2 * CH:3 * CH], cums[:, 3 * CH:])
        a_t = jnp.where((gate >= 8) & (gate < 16), cum_ch, jnp.where((gate >= 16) & (gate < 24), cum_dc, v_t))
        a_ref[rows, :] = jnp.concatenate([a_t, jnp.zeros((LANES - 32, CH), F32)], axis=0).T
        at_ref[ci] = a_t
        pairs = CH // (2 * DC)
        for p in range(pairs):
            c0 = a_t[:, 2 * p * DC:(2 * p + 1) * DC]
            c1 = a_t[:, (2 * p + 1) * DC:(2 * p + 2) * DC]
            atp_ref[ci * pairs + p] = jnp.concatenate([c0, c0, c1, c1], axis=1)
        return carry

    lax.fori_loop(0, n_chunks, chunk, 0)


def _gate_prep(g, bias_row, alog_row):
    bsz, t, _ = g.shape
    n_chunks = t // CH
    return pl.pallas_call(
        functools.partial(_gate_kernel, n_chunks=n_chunks),
        grid=(bsz,),
        in_specs=[pl.BlockSpec((None, t, LANES), lambda b: (b, 0, 0)),
                  pl.BlockSpec((1, LANES), lambda b: (0, 0)),
                  pl.BlockSpec((1, LANES), lambda b: (0, 0))],
        out_specs=[pl.BlockSpec((None, t, LANES), lambda b: (b, 0, 0)),
                   pl.BlockSpec((None, n_chunks, 32, CH), lambda b: (b, 0, 0, 0)),
                   pl.BlockSpec((None, t // (2 * DC), 32, PACK * DC), lambda b: (b, 0, 0, 0))],
        out_shape=[jax.ShapeDtypeStruct((bsz, t, LANES), F32),
                   jax.ShapeDtypeStruct((bsz, n_chunks, 32, CH), F32),
                   jax.ShapeDtypeStruct((bsz, t // (2 * DC), 32, PACK * DC), F32)],
        compiler_params=_params(("parallel",)),
        name="gate_prep",
    )(g, bias_row, alog_row)


def _chunk_order(j, d, n_chunks, n_ctx_chunks):
    if d == 0:
        return j
    return jnp.where(j < n_ctx_chunks, n_ctx_chunks - 1 - j, n_chunks + n_ctx_chunks - 1 - j)


def _head_spec(t, name):
    base = CB[name]
    return pl.BlockSpec((None, t, LANES), lambda b, h: (b, 0, base + h))


def _mlstm_kernel(q_ref, k_ref, v_ref, o_ref, a_ref, at_ref, y_ref, c_scr, y1_scr, *, n_chunks, n_ctx_chunks):
    r = lax.broadcasted_iota(jnp.int32, (CH, CH), 0)
    c = lax.broadcasted_iota(jnp.int32, (CH, CH), 1)
    one_col = (lax.broadcasted_iota(jnp.int32, (CH, LANES), 1) == 0).astype(F32)

    dirs = range(2 * MLSTM_HEADS)
    masks = [(c <= r) if i % 2 == 0 else (c >= r) for i in dirs]
    lasts = [CH - 1 if i % 2 == 0 else 0 for i in dirs]
    lanes = [slice((i // 2) * LANES, (i // 2 + 1) * LANES) for i in dirs]
    gate = [(i % 2) * 4 + pl.program_id(1) * MLSTM_HEADS + i // 2 for i in dirs]
    c_scr[...] = jnp.zeros_like(c_scr)

    def step(j, ms):
        cis = [_chunk_order(j, i % 2, n_chunks, n_ctx_chunks) for i in dirs]
        rows = [pl.ds(pl.multiple_of(ci * CH, CH), CH) for ci in cis]
        qs = [q_ref[rows[i], lanes[i]] for i in dirs]
        ks = [k_ref[rows[i], lanes[i]] for i in dirs]
        qk = [_dot_nt(qs[d], ks[d]) for d in dirs]
        pc = [_dot(qs[d], c_scr[d].astype(BF16)) for d in dirs]
        a_blk = [a_ref[rw, :] for rw in rows]
        at_blk = [at_ref[ci] for ci in cis]
        li_col = [_lane_col(a_blk[d], gate[d]) for d in dirs]
        b_col = [_lane_col(a_blk[d], 8 + gate[d]) for d in dirs]
        li_row = [_sub_row(at_blk[d], gate[d]) for d in dirs]
        b_row = [_sub_row(at_blk[d], 8 + gate[d]) for d in dirs]
        dm = [jnp.where(masks[d], b_col[d] + (li_row[d] - b_row[d]), NEG) for d in dirs]
        inter = [ms[d] + b_col[d] for d in dirs]
        m_t = [jnp.maximum(inter[d], jnp.max(dm[d], axis=1, keepdims=True)) for d in dirs]
        m_s = [m_t[d] - LOG2_SCALE for d in dirs]
        s = [(qk[d] * jnp.exp2(dm[d] - m_s[d])).astype(BF16) for d in dirs]
        vs = [v_ref[rows[i], lanes[i]] for i in dirs]
        sv = [_dot(s[d], jnp.concatenate([vs[d], one_col.astype(BF16)], axis=1)) for d in dirs]

        b_end = [b_col[d][lasts[d]:lasts[d] + 1, :] for d in dirs]
        dec = [(b_end[d] - b_col[d]) + li_col[d] for d in dirs]
        m_new = [jnp.maximum(ms[d] + b_end[d], jnp.max(dec[d], axis=0, keepdims=True)) for d in dirs]
        w_state = [jnp.exp2(dec[d] - m_new[d]) for d in dirs]
        wv = [jnp.concatenate([(w_state[d] * vs[d].astype(F32)).astype(BF16), (w_state[d] * one_col).astype(BF16)],
                              axis=1) for d in dirs]
        upd = [_dot_tn(ks[d], wv[d]) for d in dirs]
        for d in dirs:
            c_scr[d] = jnp.exp2(ms[d] + b_end[d] - m_new[d]) * c_scr[d] + upd[d]

        for d in dirs:
            nd = jnp.exp2(inter[d] - m_s[d]) * pc[d] + sv[d]
            h_out = nd[:, :LANES] / jnp.maximum(jnp.abs(nd[:, LANES:LANES + 1]), jnp.exp2(-m_t[d]))
            if d % 2 == 0:
                y_ref[rows[d], lanes[d]] = h_out
            else:
                y1_scr[rows[d], lanes[d]] = h_out
        return tuple(m_new)

    zero = jnp.zeros((1, 1), F32)
    lax.fori_loop(0, n_chunks, step, (zero,) * len(dirs))

    def finish(ci, carry):
        rw = pl.ds(pl.multiple_of(ci * CH, CH), CH)
        y_ref[rw, :] = (y_ref[rw, :] + y1_scr[rw, :]) * _sigmoid(o_ref[rw, :].astype(F32))
        return carry

    lax.fori_loop(0, n_chunks, finish, 0)


def _mlstm(u, a, at_ch, ctx_len):
    bsz, t, _ = u.shape
    n_chunks = t // CH
    kern = functools.partial(_mlstm_kernel, n_chunks=n_chunks, n_ctx_chunks=ctx_len // CH)
    hp = MLSTM_HEADS
    wide = hp * LANES

    def spec(name):
        blk = CB[name] // hp
        return pl.BlockSpec((None, t, wide), lambda b, h: (b, 0, blk + h))

    return pl.pallas_call(
        kern,
        grid=(bsz, N_HEADS // hp),
        in_specs=[spec("m_q"), spec("m_k"), spec("m_v"), spec("m_o"),
                  pl.BlockSpec((None, t, LANES), lambda b, h: (b, 0, 0)),
                  pl.BlockSpec((None, n_chunks, 32, CH), lambda b, h: (b, 0, 0, 0))],
        out_specs=pl.BlockSpec((None, t, wide), lambda b, h: (b, 0, h)),
        out_shape=jax.ShapeDtypeStruct((bsz, t, GROUP_W), F32),
        scratch_shapes=[pltpu.VMEM((2 * hp, HEAD_DIM, 2 * LANES), F32), pltpu.VMEM((t, wide), F32)],
        compiler_params=_params(("parallel", "parallel")),
        name="mlstm",
    )(u, u, u, u, a, at_ch)


def _retention_kernel(lg_ref, q_ref, k_ref, v_ref, cos_ref, sin_ref, y_ref, qp_scr, kp_scr, r_scr, y1_scr, *,
                      n_chunks, n_ctx_chunks):
    h = pl.program_id(1)

    def prep(ci, carry):
        rows = pl.ds(pl.multiple_of(ci * CH, CH), CH)
        cos = cos_ref[rows, :]
        sin = sin_ref[rows, :]
        qp_scr[rows, :] = _rope(q_ref[rows, :].astype(F32), cos, sin).astype(BF16)
        kp_scr[rows, :] = (_rope(k_ref[rows, :].astype(F32), cos, sin) * SCALE).astype(BF16)
        return carry

    lax.fori_loop(0, n_chunks, prep, 0)

    r = lax.broadcasted_iota(jnp.int32, (CH, CH), 0).astype(F32)
    c = lax.broadcasted_iota(jnp.int32, (CH, CH), 1).astype(F32)
    pos = lax.broadcasted_iota(jnp.int32, (CH, 1), 0).astype(F32)

    dirs = range(2)
    decay, q_decay, k_decay, chunk_decay = [], [], [], []
    for d in dirs:
        lg = lg_ref[d * N_HEADS + h]
        rel = (r - c) if d == 0 else (c - r)
        decay.append(jnp.where(rel >= 0, jnp.exp(lg * jnp.maximum(rel, 0.0)), 0.0))
        p_vis = pos if d == 0 else (CH - 1.0) - pos
        q_decay.append(jnp.exp(lg * (p_vis + 1.0)))
        k_decay.append(jnp.exp(lg * ((CH - 1.0) - p_vis)))
        chunk_decay.append(jnp.exp(lg * CH))
    r_scr[...] = jnp.zeros_like(r_scr)

    def step(j, carry):
        cis = [_chunk_order(j, d, n_chunks, n_ctx_chunks) for d in dirs]
        rows = [pl.ds(pl.multiple_of(ci * CH, CH), CH) for ci in cis]
        qs = [qp_scr[rw, :] for rw in rows]
        ks = [kp_scr[rw, :] for rw in rows]
        vs = [v_ref[rw, :] for rw in rows]
        qk = [_dot_nt(qs[d], ks[d]) for d in dirs]
        qr = [_dot(qs[d], r_scr[d].astype(BF16)) for d in dirs]
        sv = [_dot((qk[d] * decay[d]).astype(BF16), vs[d]) for d in dirs]
        upd = [_dot_tn((ks[d].astype(F32) * k_decay[d]).astype(BF16), vs[d]) for d in dirs]
        for d in dirs:
            r_scr[d] = chunk_decay[d] * r_scr[d] + upd[d]
        y_ref[rows[0], :] = sv[0] + q_decay[0] * qr[0]
        y1_scr[rows[1], :] = sv[1] + q_decay[1] * qr[1]
        return carry

    lax.fori_loop(0, n_chunks, step, 0)

    def finish(ci, carry):
        rw = pl.ds(pl.multiple_of(ci * CH, CH), CH)
        y_ref[rw, :] = y_ref[rw, :] + y1_scr[rw, :]
        return carry

    lax.fori_loop(0, n_chunks, finish, 0)


def _retention(u, log_gamma, cos_t, sin_t, ctx_len):
    bsz, t, _ = u.shape
    n_chunks = t // CH
    kern = functools.partial(_retention_kernel, n_chunks=n_chunks, n_ctx_chunks=ctx_len // CH)
    tab = pl.BlockSpec((t, LANES), lambda b, h: (0, 0))
    return pl.pallas_call(
        kern,
        grid=(bsz, N_HEADS),
        in_specs=[pl.BlockSpec(memory_space=pltpu.SMEM),
                  _head_spec(t, "r_q"), _head_spec(t, "r_k"), _head_spec(t, "r_v"), tab, tab],
        out_specs=pl.BlockSpec((None, t, LANES), lambda b, h: (b, 0, h)),
        out_shape=jax.ShapeDtypeStruct((bsz, t, GROUP_W), F32),
        scratch_shapes=[pltpu.VMEM((t, LANES), BF16), pltpu.VMEM((t, LANES), BF16),
                        pltpu.VMEM((2, HEAD_DIM, HEAD_DIM), F32), pltpu.VMEM((t, LANES), F32)],
        compiler_params=_params(("parallel", "parallel")),
        name="retention",
    )(log_gamma.reshape(2 * N_HEADS), u, u, u, cos_t, sin_t)


def _attn_kernel(q_ref, k_ref, v_ref, cos_ref, sin_ref, cosq_ref, sinq_ref, qw_ref, kw_ref, o_ref,
                 kt_scr, s_scr, m_scr, *, n_blocks, with_ctx):
    j = pl.program_id(2)

    def load_q():
        qs = []
        for g in range(2):
            q = q_ref[:, g * LANES:(g + 1) * LANES].astype(F32)
            q = q * lax.rsqrt(jnp.mean(q * q, axis=-1, keepdims=True) + EPS) * qw_ref[...]
            qs.append((_rope(q, cosq_ref[...], sinq_ref[...]) * (SCALE * LOG2E)).astype(BF16))
        return jnp.concatenate(qs, axis=0)

    def finish(o, l_part):
        o = o / jnp.sum(l_part, axis=-1, keepdims=True)
        o_ref[:, 0:LANES] = o[:CH]
        o_ref[:, LANES:2 * LANES] = o[CH:]

    @pl.when(j == 0)
    def _():
        def prep(ci, carry):
            rows = pl.ds(pl.multiple_of(ci * CH, CH), CH)
            k = k_ref[rows, :].astype(F32)
            k = k * lax.rsqrt(jnp.mean(k * k, axis=-1, keepdims=True) + EPS) * kw_ref[...]
            kt_scr[ci] = _rope(k, cos_ref[rows, :], sin_ref[rows, :]).T.astype(BF16)
            return carry

        lax.fori_loop(0, n_blocks, prep, 0)
        if with_ctx:
            s = _dot(load_q(), kt_scr[0])
            p = jnp.exp2(s - jnp.max(s, axis=-1, keepdims=True))
            finish(_dot(p.astype(BF16), v_ref[0:CH, :]), p[:, :LANES] + p[:, LANES:])
        else:
            o_ref[...] = jnp.zeros_like(o_ref)

    @pl.when(j >= 1)
    def _():
        qq = load_q()
        mx = None
        for kb in range(n_blocks):
            s = _dot(qq, kt_scr[kb])
            s_scr[:, kb * CH:(kb + 1) * CH] = s
            m2 = jnp.maximum(s[:, :LANES], s[:, LANES:])
            mx = m2 if mx is None else jnp.maximum(mx, m2)
        m_scr[...] = jnp.broadcast_to(jnp.max(mx, axis=-1, keepdims=True), (2 * CH, LANES))

    def value_pass(_, carry):
        m = m_scr[...]
        one_col = (lax.broadcasted_iota(jnp.int32, (CH, LANES), 1) == 0).astype(BF16)
        o = None
        for kb in range(n_blocks):
            cols = slice(kb * CH, (kb + 1) * CH)
            p_lo = jnp.exp2(s_scr[:, kb * CH:kb * CH + LANES] - m)
            p_hi = jnp.exp2(s_scr[:, kb * CH + LANES:(kb + 1) * CH] - m)
            pv = _dot(jnp.concatenate([p_lo, p_hi], axis=1).astype(BF16),
                      jnp.concatenate([v_ref[cols, :], one_col], axis=1))
            o = pv if o is None else o + pv
        finish(o[:, :LANES], o[:, LANES:])
        return carry

    lax.fori_loop(0, jnp.minimum(j, 1), value_pass, 0)


def _attention(u, cos_t, sin_t, qn_w, kn_w, ctx_len, with_ctx):
    bsz, t, _ = u.shape
    assert ctx_len == CH
    n_blocks = t // CH
    kern = functools.partial(_attn_kernel, n_blocks=n_blocks, with_ctx=with_ctx)
    qb = CB["a_q"] // 2
    kb = CB["a_k"]
    vb = CB["a_v"]
    tab = pl.BlockSpec((t, LANES), lambda b, kv, j: (0, 0))
    tabq = pl.BlockSpec((CH, LANES), lambda b, kv, j: (j, 0))
    vec = pl.BlockSpec((1, LANES), lambda b, kv, j: (0, 0))
    return pl.pallas_call(
        kern,
        grid=(bsz, KV_HEADS, n_blocks),
        in_specs=[pl.BlockSpec((None, CH, 2 * LANES), lambda b, kv, j: (b, j, qb + kv)),
                  pl.BlockSpec((None, t, LANES), lambda b, kv, j: (b, 0, kb + kv)),
                  pl.BlockSpec((None, t, LANES), lambda b, kv, j: (b, 0, vb + kv)),
                  tab, tab, tabq, tabq, vec, vec],
        out_specs=pl.BlockSpec((None, CH, 2 * LANES), lambda b, kv, j: (b, j, kv)),
        out_shape=jax.ShapeDtypeStruct((bsz, t, GROUP_W), F32),
        scratch_shapes=[pltpu.VMEM((n_blocks, HEAD_DIM, CH), BF16), pltpu.VMEM((2 * CH, t), F32),
                        pltpu.VMEM((2 * CH, LANES), F32)],
        compiler_params=_params(("parallel", "parallel", "arbitrary")),
        name="attention",
    )(u, u, u, cos_t, sin_t, cos_t, sin_t, qn_w.reshape(1, LANES), kn_w.reshape(1, LANES))


def _block_diag(y, lane_masks):
    return jnp.concatenate([y * mk for mk in lane_masks], axis=0)


def _packed_dot3(a, b, lane_masks):
    n = a[0].shape[0]
    x = _dot(jnp.concatenate([a[0], a[1]], axis=0), _block_diag(b[0], lane_masks))
    return (x[:n] + x[n:]) + _dot(a[0], _block_diag(b[1], lane_masks))


def _tri_inverse(n_mats, r, cl, lane_masks):
    eye = (r == cl).astype(F32)
    diag16 = (r // 16) == (cl // 16)
    ms = [-jnp.where(diag16, n, 0.0) for n in n_mats]
    ps = [eye + m for m in ms]
    mps = [_split(m) for m in ms]
    mps = [_split(_packed_dot3(mp, mp, lane_masks)) for mp in mps]
    for _ in range(2):
        both = [_packed_dot3(tuple(jnp.concatenate([a, b], axis=0) for a, b in zip(_split(p), mp)), mp, lane_masks)
                for p, mp in zip(ps, mps)]
        ps = [p + x[:DC] for p, x in zip(ps, both)]
        mps = [_split(x[DC:]) for x in both]
    ps = [p + _packed_dot3(_split(p), mp, lane_masks) for p, mp in zip(ps, mps)]
    for w in (32, 64):
        off = ((r // w) == (cl // w)) & ((r // (w // 2)) != (cl // (w // 2)))
        pbs = [p.astype(BF16) for p in ps]
        pds = [_block_diag(pb, lane_masks) for pb in pbs]
        tmp = [_dot(jnp.where(off, n, 0.0).astype(BF16), pd).astype(BF16) for n, pd in zip(n_mats, pds)]
        ps = [p - _dot(pb, _block_diag(a, lane_masks)) for p, pb, a in zip(ps, pbs, tmp)]
    return ps


def _deltanet_kernel(q_ref, k_ref, v_ref, wq_ref, wk_ref, wv_ref, a_ref, at_ref, y_ref,
                     xs_scr, qd_scr, kd_scr, vd_scr, aq_scr, b_scr, ge_scr, y1_scr, *,
                     t, ctx_len):
    h = pl.program_id(1)
    n_chunks = t // DC
    n_ctx_chunks = ctx_len // DC
    pad = 8

    streams = ((q_ref, wq_ref, qd_scr, True, SCALE), (k_ref, wk_ref, kd_scr, True, 1.0),
               (v_ref, wv_ref, vd_scr, False, 1.0))
    zeros = jnp.zeros((pad, LANES), F32)
    for si in range(len(streams)):
        xs_scr[si, 0:pad, :] = zeros
        xs_scr[si, ctx_len + pad:ctx_len + 2 * pad, :] = zeros
        xs_scr[si, t + 2 * pad:t + 3 * pad, :] = zeros

    def seg_row(ci):
        return ci * CH + pad + jnp.where(ci * CH >= ctx_len, pad, 0)

    def load(ci, carry):
        rows = pl.ds(pl.multiple_of(ci * CH, CH), CH)
        for si, (src_ref, _, _, _, _) in enumerate(streams):
            xs_scr[si, pl.ds(pl.multiple_of(seg_row(ci), 8), CH), :] = src_ref[rows, :].astype(F32)
        return carry

    lax.fori_loop(0, t // CH, load, 0)

    def conv(ci, carry):
        base = seg_row(ci) - CONV_K // 2
        for si, (_, w_ref, dst_scr, l2, scale) in enumerate(streams):
            acc = xs_scr[si, pl.ds(base, CH), :] * w_ref[0:1, :]
            for j in range(1, CONV_K):
                acc = acc + xs_scr[si, pl.ds(base + j, CH), :] * w_ref[j:j + 1, :]
            acc = acc * _sigmoid(acc)
            if l2:
                acc = acc * lax.rsqrt(jnp.sum(acc * acc, axis=-1, keepdims=True) + EPS) * scale
            dst_scr[pl.ds(pl.multiple_of(ci * CH, CH), CH), :] = acc
        return carry

    lax.fori_loop(0, t // CH, conv, 0)

    wp = PACK * DC
    r = lax.broadcasted_iota(jnp.int32, (DC, wp), 0)
    lane = lax.broadcasted_iota(jnp.int32, (DC, wp), 1)
    blk = lane // DC
    cl = lane % DC
    fwd = (blk % 2) == 0
    ahead = jnp.where(fwd, r - cl, cl - r)
    incl = ahead >= 0
    strict = ahead > 0
    lane_masks = [(blk == i).astype(BF16) for i in range(PACK)]

    def pick(vals):
        out = vals[PACK - 1]
        for i in range(PACK - 2, -1, -1):
            out = jnp.where(blk == i, vals[i], out)
        return out

    def pre(first_pair, n_groups):
        groups = []
        n_mats = []
        for gg in range(n_groups):
            pair = first_pair + gg
            at_blk = at_ref[pair]
            g_row = jnp.where(fwd[0:1, :], _sub_row(at_blk, 16 + h), _sub_row(at_blk, 20 + h))
            kks, qks, g_cols, betas, chains = [], [], [], [], []
            for cc in range(2):
                ci = pair * 2 + cc
                rows = pl.ds(pl.multiple_of(ci * DC, DC), DC)
                q = qd_scr[rows, :]
                k = kd_scr[rows, :]
                v = vd_scr[rows, :]
                a_blk = a_ref[rows, :]
                kb = k.astype(BF16)
                kb2 = jnp.concatenate([kb, kb], axis=0)
                kks.append(_dot_nt(kb, kb2))
                qks.append(_dot_nt(q.astype(BF16), kb2))
                for d in range(2):
                    last = DC - 1 if d == 0 else 0
                    g_col = _lane_col(a_blk, 16 + d * 4 + h)
                    beta = _lane_col(a_blk, 24 + d * 4 + h)
                    eg = jnp.exp2(g_col)
                    g_end = g_col[last:last + 1, :]
                    rhs = jnp.concatenate([beta * v, (beta * eg) * k], axis=1).astype(BF16)
                    ke = (jnp.exp2(g_end - g_col) * k).astype(BF16)
                    ge_scr[d, ci] = jnp.broadcast_to(jnp.exp2(g_end), (8, LANES))
                    g_cols.append(g_col)
                    betas.append(beta)
                    chains.append((d, ci, rows, rhs, ke, eg * q))
            decay = jnp.exp2(jnp.where(incl, pick(g_cols) - g_row, NEG))
            n_mats.append(jnp.where(strict, pick(betas) * decay * jnp.concatenate(kks, axis=1), 0.0))
            qkd = (jnp.concatenate(qks, axis=1) * decay).astype(BF16)
            groups.append((chains, qkd, jnp.concatenate([ch[3] for ch in chains], axis=0)))
        t_invs = [t_inv.astype(BF16) for t_inv in _tri_inverse(n_mats, r, cl, lane_masks)]
        ws = [[_dot(t_inv * mk, rhs_all).astype(BF16) for mk in lane_masks]
              for t_inv, (_, _, rhs_all) in zip(t_invs, groups)]
        w_all = [jnp.concatenate(w4, axis=0) for w4 in ws]
        kws = [[_dot_tn(ch[4], w) for ch, w in zip(chains, w4)] for (chains, _, _), w4 in zip(groups, ws)]
        qws = [[_dot(qkd * mk, wa) for mk in lane_masks] for (_, qkd, _), wa in zip(groups, w_all)]
        for (chains, _, _), kw4, qw4 in zip(groups, kws, qws):
            for (d, ci, rows, _, _, egq), kw, qw in zip(chains, kw4, qw4):
                aq_scr[d, ci, 0:HEAD_DIM, :] = (-kw[:, LANES:]).astype(BF16)
                aq_scr[d, ci, HEAD_DIM:HEAD_DIM + DC, :] = (egq - qw[:, LANES:]).astype(BF16)
                b_scr[d, ci] = kw[:, :LANES]
                if d == 0:
                    y_ref[rows, :] = qw[:, :LANES]
                else:
                    y1_scr[rows, :] = qw[:, :LANES]

    n_pairs = n_chunks // 2
    n_trips = max(1, n_pairs // PRE_GROUPS)
    base, extra = divmod(n_pairs, n_trips)

    def trips(first_pair, count, size):
        def body(i, carry):
            pre(first_pair + i * size, size)
            return carry

        if count:
            lax.fori_loop(0, count, body, 0)

    trips(0, extra, base + 1)
    trips(extra * (base + 1), n_trips - extra, base)

    def scan(j, carry):
        cis = [_chunk_order(j, d, n_chunks, n_ctx_chunks) for d in range(2)]
        xs = [_dot(aq_scr[d, cis[d]], carry[d].astype(BF16)) for d in range(2)]
        new = tuple(ge_scr[d, cis[d]][0:1, :] * carry[d] + (xs[d][:HEAD_DIM, :] + b_scr[d, cis[d]])
                    for d in range(2))
        rows = [pl.ds(pl.multiple_of(cis[d] * DC, DC), DC) for d in range(2)]
        y_ref[rows[0], :] = y_ref[rows[0], :] + xs[0][HEAD_DIM:, :]
        y1_scr[rows[1], :] = y1_scr[rows[1], :] + xs[1][HEAD_DIM:, :]
        return new

    zero = jnp.zeros((HEAD_DIM, HEAD_DIM), F32)
    lax.fori_loop(0, n_chunks, scan, (zero, zero))

    def add(ci, carry):
        rows = pl.ds(pl.multiple_of(ci * CH, CH), CH)
        y_ref[rows, :] = y_ref[rows, :] + y1_scr[rows, :]
        return carry

    lax.fori_loop(0, t // CH, add, 0)


def _deltanet(u, conv_w, a, at_dc, ctx_len):
    bsz, t, _ = u.shape
    n_chunks = t // DC
    kern = functools.partial(_deltanet_kernel, t=t, ctx_len=ctx_len)

    def wspec(off):
        return pl.BlockSpec((CONV_K, LANES), lambda b, h: (0, off + h))

    return pl.pallas_call(
        kern,
        grid=(bsz, N_HEADS),
        in_specs=[_head_spec(t, "d_q"), _head_spec(t, "d_k"), _head_spec(t, "d_v"),
                  wspec(0), wspec(N_HEADS), wspec(2 * N_HEADS),
                  pl.BlockSpec((None, t, LANES), lambda b, h: (b, 0, 0)),
                  pl.BlockSpec((None, n_chunks // 2, 32, PACK * DC), lambda b, h: (b, 0, 0, 0))],
        out_specs=pl.BlockSpec((None, t, LANES), lambda b, h: (b, 0, h)),
        out_shape=jax.ShapeDtypeStruct((bsz, t, GROUP_W), F32),
        scratch_shapes=[pltpu.VMEM((3, t + 24, LANES), F32),
                        pltpu.VMEM((t, LANES), F32), pltpu.VMEM((t, LANES), F32), pltpu.VMEM((t, LANES), F32),
                        pltpu.VMEM((2, n_chunks, HEAD_DIM + DC, LANES), BF16),
                        pltpu.VMEM((2, n_chunks, HEAD_DIM, LANES), F32),
                        pltpu.VMEM((2, n_chunks, 8, LANES), F32),
                        pltpu.VMEM((t, LANES), F32)],
        compiler_params=_params(("parallel", "parallel")),
        name="deltanet",
    )(u, u, u, conv_w, conv_w, conv_w, a, at_dc)


def _out_kernel(x_ref, ym_ref, yr_ref, ya_ref, yd_ref, zm_ref, zr_ref, za_ref, zd_ref, hn_ref, w_ref,
                mod_ref, pw_ref, o_ref, *, first_block):
    b = pl.program_id(0)
    i = pl.program_id(1) + first_block

    def silu(z_ref):
        z = z_ref[...].astype(F32)
        return z * _sigmoid(z)

    def head_rms(y_ref, g):
        y = y_ref[...]
        parts = []
        for hh in range(N_HEADS):
            p = y[:, hh * LANES:(hh + 1) * LANES]
            parts.append(p * lax.rsqrt(jnp.mean(p * p, axis=-1, keepdims=True) + EPS))
        return jnp.concatenate(parts, axis=1) * hn_ref[:, g * GROUP_W:(g + 1) * GROUP_W]

    merged = jnp.concatenate([
        (head_rms(ym_ref, 0) * silu(zm_ref)).astype(BF16),
        (head_rms(yr_ref, 1) * silu(zr_ref)).astype(BF16),
        (ya_ref[...] * silu(za_ref)).astype(BF16),
        (head_rms(yd_ref, 2) * silu(zd_ref)).astype(BF16)], axis=1)
    o = _dot(merged, w_ref[...])
    o = o * lax.rsqrt(jnp.mean(o * o, axis=-1, keepdims=True) + EPS) * pw_ref[...]
    row = jnp.where(i == 0, 4, b)
    gate = mod_ref[pl.ds(row, 1), 2 * D_MODEL:3 * D_MODEL]
    o_ref[...] = x_ref[...] + gate * o


def _out_projection(xa, ys, u, hn_w, w_out, layer, mod, post_w, ctx_len, latent_only):
    bsz, t, d = xa.shape
    assert ctx_len == CH
    first = 1 if latent_only else 0
    tok = lambda b, i: (b, i + first, 0)
    yspec = pl.BlockSpec((None, CH, GROUP_W), tok)

    def zspec(name):
        blk = CB[name] // N_HEADS
        return pl.BlockSpec((None, CH, GROUP_W), lambda b, i: (b, i + first, blk))

    return pl.pallas_call(
        functools.partial(_out_kernel, first_block=first),
        grid=(bsz, t // CH - first),
        in_specs=[pl.BlockSpec((None, CH, d), tok), yspec, yspec, yspec, yspec,
                  zspec("m_z"), zspec("r_z"), zspec("a_z"), zspec("d_z"),
                  pl.BlockSpec((1, 3 * GROUP_W), lambda b, i: (0, 0)),
                  pl.BlockSpec((None, d, d), lambda b, i: (layer, 0, 0)),
                  pl.BlockSpec((8, 3 * d), lambda b, i: (0, 0)),
                  pl.BlockSpec((1, d), lambda b, i: (0, 0))],
        out_specs=pl.BlockSpec((None, CH, d), lambda b, i: (b, i, 0)),
        out_shape=jax.ShapeDtypeStruct((bsz, t - first * CH, d), F32),
        compiler_params=_params(("parallel", "parallel")),
        name="out_proj",
    )(xa, *ys, u, u, u, u, hn_w.reshape(1, 3 * GROUP_W), w_out, mod, post_w.reshape(1, d))


M_GATE_COL = 5 * GROUP_W
N_M_GATES = 4 * N_HEADS
D_GATE_COL = N_MAIN + N_M_GATES
W_TILE = 512


def _wprep_kernel(a_ref, b_ref, o_ref):
    n = pl.program_id(1)

    @pl.when(n < M_GATE_COL // W_TILE)
    def _():
        o_ref[...] = a_ref[...]

    @pl.when(n >= M_GATE_COL // W_TILE)
    def _():
        o_ref[...] = jnp.concatenate([a_ref[:, N_M_GATES:], b_ref[:, :N_M_GATES]], axis=1)


def _reorder_w_in(w_in):
    depth, d, _ = w_in.shape
    assert M_GATE_COL % W_TILE == 0
    wb = w_in.astype(BF16)
    w_main = pl.pallas_call(
        _wprep_kernel,
        grid=(depth, N_MAIN // W_TILE),
        in_specs=[pl.BlockSpec((None, d, W_TILE), lambda l, n: (l, 0, n)),
                  pl.BlockSpec((None, d, LANES), lambda l, n: (l, 0, (n + 1) * (W_TILE // LANES)))],
        out_specs=pl.BlockSpec((None, d, W_TILE), lambda l, n: (l, 0, n)),
        out_shape=jax.ShapeDtypeStruct((depth, d, N_MAIN), BF16),
        compiler_params=_params(("parallel", "parallel")),
        name="w_in_prep",
    )(wb, wb)
    gates = jnp.concatenate([wb[:, :, M_GATE_COL:M_GATE_COL + N_M_GATES], wb[:, :, D_GATE_COL:]], axis=2)
    w_gate = jnp.pad(gates, ((0, 0), (0, 0), (0, LANES - gates.shape[2])))
    return w_main, w_gate


def _rope_tables(seq, ctx_len):
    rows = seq // GRID_W
    row = jnp.repeat(jnp.arange(rows), GRID_W)
    col = jnp.tile(jnp.arange(GRID_W), rows)
    n_freq = HEAD_DIM // 4
    inv_freq = ROPE_BASE ** (-jnp.arange(n_freq, dtype=F32) / n_freq)
    ar = row[:, None] * inv_freq
    ac = col[:, None] * inv_freq
    cos = jnp.concatenate([jnp.cos(ar), jnp.cos(ar), jnp.cos(ac), jnp.cos(ac)], axis=1)
    sin = jnp.concatenate([-jnp.sin(ar), jnp.sin(ar), -jnp.sin(ac), jnp.sin(ac)], axis=1)
    cos = jnp.concatenate([jnp.ones((ctx_len, HEAD_DIM), F32), cos], axis=0)
    sin = jnp.concatenate([jnp.zeros((ctx_len, HEAD_DIM), F32), sin], axis=0)
    return cos, sin


def kernel(x, c, ctx, c_ctx, ada_w, ada_b, pre_norm_w, post_norm_w, w_in, w_out, mlstm_i_bias, mlstm_f_bias,
           ret_log_gamma, attn_q_norm_w, attn_k_norm_w, dn_conv_w, dn_a_log, dn_dt_bias, head_norm_w):
    bsz, seq, d = x.shape
    ctx_len = ctx.shape[1]
    t = ctx_len + seq
    depth = ada_w.shape[0]
    assert bsz <= 4 and ctx_len % CH == 0 and seq % CH == 0

    xa = jnp.concatenate([ctx, x], axis=1)
    c8 = jnp.zeros((8, d), F32).at[:bsz].set(c).at[4].set(c_ctx)
    mod = _modulation(c8, ada_w, ada_b)
    cos_t, sin_t = _rope_tables(seq, ctx_len)

    w_main, w_gate = _reorder_w_in(w_in)
    w_out_b = w_out.astype(BF16)

    for l in range(depth):
        u, g = _in_projection(xa, mod[l], pre_norm_w[l], w_main, w_gate, l, ctx_len)
        zeros8 = jnp.zeros((8,), F32)
        bias_row = jnp.pad(jnp.concatenate([mlstm_i_bias[l].reshape(-1), mlstm_f_bias[l].reshape(-1),
                                            dn_dt_bias[l].reshape(-1), zeros8]), (0, LANES - 32))
        alog_row = jnp.pad(jnp.concatenate([zeros8, zeros8, dn_a_log[l].reshape(-1), zeros8]), (0, LANES - 32))
        a, at_ch, at_dc = _gate_prep(g, bias_row.reshape(1, LANES), alog_row.reshape(1, LANES))
        y_m = _mlstm(u, a, at_ch, ctx_len)
        y_r = _retention(u, ret_log_gamma[l], cos_t, sin_t, ctx_len)
        y_a = _attention(u, cos_t, sin_t, attn_q_norm_w[l], attn_k_norm_w[l], ctx_len, with_ctx=(l < depth - 1))
        y_d = _deltanet(u, dn_conv_w[l], a, at_dc, ctx_len)
        xa = _out_projection(xa, (y_m, y_r, y_a, y_d), u, head_norm_w[l], w_out_b, l, mod[l],
                             post_norm_w[l], ctx_len, latent_only=(l == depth - 1))
    return xa
```

```python
import functools
import math

import jax
import jax.numpy as jnp
from jax import lax
from jax.experimental import pallas as pl
from jax.experimental.pallas import tpu as pltpu

F32 = jnp.float32
BF16 = jnp.bfloat16

D_MODEL = 2048
GROUP_W = 512
N_HEADS = 4
HEAD_DIM = 128
KV_HEADS = 2
GRID_W = 64
CONV_K = 5
ROPE_BASE = 10000.0
EPS = 1e-6
SCALE = HEAD_DIM ** -0.5
LOG2E = 1.4426950408889634
LOG2_SCALE = -0.5 * math.log2(HEAD_DIM)
NEG = -1e30

LANES = 128
CH = 256
DC = 64
MLSTM_HEADS = 2
RET_HEADS = 2
PACK = 4
PRE_GROUPS = 8
N_MAIN = 64 * LANES
VMEM_LIMIT = 56 * 1024 * 1024
IN_PROJ_MAX_TM = 1088

CB = dict(m_q=0, m_k=4, m_v=8, m_o=12, m_z=16, r_q=20, r_k=24, r_v=28, r_z=32,
          a_q=36, a_k=40, a_v=42, a_z=44, d_q=48, d_k=52, d_v=56, d_z=60)


def _dot(a, b, prec=None):
    return jnp.dot(a, b, preferred_element_type=F32, precision=prec)


def _dot_nt(a, b, prec=None):
    return lax.dot_general(a, b, (((1,), (1,)), ((), ())), preferred_element_type=F32, precision=prec)


def _dot_tn(a, b, prec=None):
    return lax.dot_general(a, b, (((0,), (0,)), ((), ())), preferred_element_type=F32, precision=prec)


def _split(a):
    hi = a.astype(BF16)
    return hi, (a - hi.astype(F32)).astype(BF16)


def _dot3(a, b):
    n = a[0].shape[0]
    x = _dot(jnp.concatenate([a[0], a[1]], axis=0), b[0])
    return (x[:n] + x[n:]) + _dot(a[0], b[1])


def _sigmoid(x):
    return 1.0 / (1.0 + jnp.exp(-x))


def _lane_col(blk, idx):
    lane = lax.broadcasted_iota(jnp.int32, blk.shape, 1)
    return jnp.sum(jnp.where(lane == idx, blk, 0.0), axis=1, keepdims=True)


def _sub_row(blk, idx):
    sub = lax.broadcasted_iota(jnp.int32, blk.shape, 0)
    return jnp.sum(jnp.where(sub == idx, blk, 0.0), axis=0, keepdims=True)


def _rope(t, cos, sin_signed):
    lane = lax.broadcasted_iota(jnp.int32, t.shape, 1)
    partner = jnp.where((lane // 32) % 2 == 0, pltpu.roll(t, 96, 1), pltpu.roll(t, 32, 1))
    return t * cos + partner * sin_signed


def _params(sem):
    return pltpu.CompilerParams(dimension_semantics=sem, vmem_limit_bytes=VMEM_LIMIT)


def _mod_kernel(c_ref, w_ref, b_ref, o_ref):
    c = c_ref[...]
    o_ref[...] = _dot3(_split(c * _sigmoid(c)), _split(w_ref[...])) + b_ref[...]


def _modulation(c8, ada_w, ada_b):
    depth, d, n3 = ada_w.shape
    tn = 768
    return pl.pallas_call(
        _mod_kernel,
        grid=(depth, n3 // tn),
        in_specs=[pl.BlockSpec((8, d), lambda l, n: (0, 0)),
                  pl.BlockSpec((None, d, tn), lambda l, n: (l, 0, n)),
                  pl.BlockSpec((None, 1, tn), lambda l, n: (l, 0, n))],
        out_specs=pl.BlockSpec((None, 8, tn), lambda l, n: (l, 0, n)),
        out_shape=jax.ShapeDtypeStruct((depth, 8, n3), F32),
        compiler_params=_params(("parallel", "parallel")),
        name="adaln_mod",
    )(c8, ada_w, ada_b.reshape(depth, 1, n3))


def _inproj_kernel(x_ref, mod_ref, pw_ref, wm_ref, wg_ref, u_ref, g_ref, h_scr, *, tm, ctx_len, per_batch,
                   n_blocks, n_tiles):
    m = pl.program_id(0)
    n = pl.program_id(1)

    d = D_MODEL

    def modulated(x, row):
        y = x * lax.rsqrt(jnp.mean(x * x, axis=-1, keepdims=True) + EPS) * pw_ref[...]
        return (y * (1.0 + mod_ref[pl.ds(row, 1), d:2 * d]) + mod_ref[pl.ds(row, 1), 0:d]).astype(BF16)

    def normalise(blk, slot):
        b = blk // per_batch
        head = modulated(x_ref[0:ctx_len, :], jnp.where(blk % per_batch == 0, 4, b))
        hh = jnp.concatenate([head, modulated(x_ref[ctx_len:, :], b)], axis=0)
        h_scr[slot] = hh
        g_ref[...] = _dot(hh, wg_ref[...])

    @pl.when((m == 0) & (n == 0))
    def _():
        normalise(0, 0)

    @pl.when(n < n_tiles - 1)
    def _():
        u_ref[...] = _dot(h_scr[m % 2], wm_ref[...]).astype(BF16)

    @pl.when(n == n_tiles - 1)
    def _():
        u_ref[...] = _dot(h_scr[m % 2], wm_ref[...]).astype(BF16)
        normalise(jnp.minimum(m + 1, n_blocks - 1), (m + 1) % 2)


def _in_projection(xa, mod, pre_w, w_main, w_gate, layer, ctx_len):
    bsz, t, d = xa.shape
    tm = max(c for c in range(16, IN_PROJ_MAX_TM + 1, 16) if t % c == 0)
    assert tm > ctx_len
    tn = 1024
    per_batch = t // tm
    n_blocks = bsz * per_batch
    n_tiles = N_MAIN // tn
    kern = functools.partial(_inproj_kernel, tm=tm, ctx_len=ctx_len, per_batch=per_batch, n_blocks=n_blocks,
                             n_tiles=n_tiles)
    nxt = lambda m: jnp.minimum(m + 1, n_blocks - 1)
    u, g = pl.pallas_call(
        kern,
        grid=(n_blocks, n_tiles),
        in_specs=[pl.BlockSpec((None, tm, d), lambda m, n: (jnp.where((m == 0) & (n == 0), 0, nxt(m)), 0, 0)),
                  pl.BlockSpec((8, 3 * d), lambda m, n: (0, 0)),
                  pl.BlockSpec((1, d), lambda m, n: (0, 0)),
                  pl.BlockSpec((None, d, tn), lambda m, n: (layer, 0, n)),
                  pl.BlockSpec((None, d, LANES), lambda m, n: (layer, 0, 0))],
        out_specs=[pl.BlockSpec((None, tm, tn), lambda m, n: (m, 0, n)),
                   pl.BlockSpec((None, tm, LANES), lambda m, n: (jnp.where(n == n_tiles - 1, nxt(m), m), 0, 0))],
        out_shape=[jax.ShapeDtypeStruct((n_blocks, tm, N_MAIN), BF16),
                   jax.ShapeDtypeStruct((n_blocks, tm, LANES), F32)],
        scratch_shapes=[pltpu.VMEM((2, tm, d), BF16)],
        compiler_params=_params(("arbitrary", "arbitrary")),
        name="in_proj",
    )(xa.reshape(n_blocks, tm, d), mod, pre_w.reshape(1, d), w_main, w_gate)
    return u.reshape(bsz, t, N_MAIN), g.reshape(bsz, t, LANES)


def _gate_kernel(g_ref, bias_ref, alog_ref, a_ref, at_ref, atp_ref, *, n_chunks):
    lane = lax.broadcasted_iota(jnp.int32, (CH, LANES), 1)
    s_idx = lax.broadcasted_iota(jnp.int32, (CH, 4 * CH), 0)
    col = lax.broadcasted_iota(jnp.int32, (CH, 4 * CH), 1)
    kind = col // CH
    t_idx = col % CH
    ahead = jnp.where(kind % 2 == 0, t_idx - s_idx, s_idx - t_idx)
    other_sub = jnp.where(kind < 2, 0, jnp.abs(t_idx // DC - s_idx // DC))
    sums = jnp.where(ahead - CH * other_sub >= 0, 1.0, 0.0).astype(BF16)
    gate = lax.broadcasted_iota(jnp.int32, (32, CH), 0)
    fwd = (gate % 8) < 4
    bias = bias_ref[...]
    neg_ea = -jnp.exp(alog_ref[...])

    def chunk(ci, carry):
        rows = pl.ds(pl.multiple_of(ci * CH, CH), CH)
        x = g_ref[rows, :] + bias
        l1p = jnp.log(1.0 + jnp.exp(-jnp.abs(x)))
        log_f = jnp.minimum(x, 0.0) - l1p
        log_a = neg_ea * (jnp.maximum(x, 0.0) + l1p)
        v = jnp.where(lane < 8, x, jnp.where(lane < 16, log_f, jnp.where(lane < 24, log_a, _sigmoid(x))))
        v_t = v.T[0:32, :]
        v_t = jnp.where(gate < 24, v_t * LOG2E, v_t)
        v1 = v_t.astype(BF16)
        res1 = v_t - v1.astype(F32)
        v2 = res1.astype(BF16)
        v3 = (res1 - v2.astype(F32)).astype(BF16)
        cums = _dot(jnp.concatenate([v1, v2, v3], axis=0), sums)
        cums = cums[0:32] + (cums[32:64] + cums[64:96])
        cum_ch = jnp.where(fwd, cums[:, 0:CH], cums[:, CH:2 * CH])
        cum_dc = jnp.where(fwd, cums[:, 2 * CH:3 * CH], cums[:, 3 * CH:])
        a_t = jnp.where((gate >= 8) & (gate < 16), cum_ch, jnp.where((gate >= 16) & (gate < 24), cum_dc, v_t))
        a_ref[rows, :] = jnp.concatenate([a_t, jnp.zeros((LANES - 32, CH), F32)], axis=0).T
        at_ref[ci] = a_t
        pairs = CH // (2 * DC)
        for p in range(pairs):
            c0 = a_t[:, 2 * p * DC:(2 * p + 1) * DC]
            c1 = a_t[:, (2 * p + 1) * DC:(2 * p + 2) * DC]
            atp_ref[ci * pairs + p] = jnp.concatenate([c0, c0, c1, c1], axis=1)
        return carry

    lax.fori_loop(0, n_chunks, chunk, 0)


def _gate_prep(g, bias_row, alog_row):
    bsz, t, _ = g.shape
    n_chunks = t // CH
    return pl.pallas_call(
        functools.partial(_gate_kernel, n_chunks=n_chunks),
        grid=(bsz,),
        in_specs=[pl.BlockSpec((None, t, LANES), lambda b: (b, 0, 0)),
                  pl.BlockSpec((1, LANES), lambda b: (0, 0)),
                  pl.BlockSpec((1, LANES), lambda b: (0, 0))],
        out_specs=[pl.BlockSpec((None, t, LANES), lambda b: (b, 0, 0)),
                   pl.BlockSpec((None, n_chunks, 32, CH), lambda b: (b, 0, 0, 0)),
                   pl.BlockSpec((None, t // (2 * DC), 32, PACK * DC), lambda b: (b, 0, 0, 0))],
        out_shape=[jax.ShapeDtypeStruct((bsz, t, LANES), F32),
                   jax.ShapeDtypeStruct((bsz, n_chunks, 32, CH), F32),
                   jax.ShapeDtypeStruct((bsz, t // (2 * DC), 32, PACK * DC), F32)],
        compiler_params=_params(("parallel",)),
        name="gate_prep",
    )(g, bias_row, alog_row)


def _chunk_order(j, d, n_chunks, n_ctx_chunks):
    if d == 0:
        return j
    return jnp.where(j < n_ctx_chunks, n_ctx_chunks - 1 - j, n_chunks + n_ctx_chunks - 1 - j)


def _head_spec(t, name):
    base = CB[name]
    return pl.BlockSpec((None, t, LANES), lambda b, h: (b, 0, base + h))


def _mlstm_kernel(q_ref, k_ref, v_ref, o_ref, a_ref, at_ref, y_ref, c_scr, y1_scr, *, n_chunks, n_ctx_chunks):
    r = lax.broadcasted_iota(jnp.int32, (CH, CH), 0)
    c = lax.broadcasted_iota(jnp.int32, (CH, CH), 1)
    one_col = (lax.broadcasted_iota(jnp.int32, (CH, LANES), 1) == 0).astype(F32)

    dirs = range(2 * MLSTM_HEADS)
    masks = [(c <= r) if i % 2 == 0 else (c >= r) for i in dirs]
    lasts = [CH - 1 if i % 2 == 0 else 0 for i in dirs]
    lanes = [slice((i // 2) * LANES, (i // 2 + 1) * LANES) for i in dirs]
    gate = [(i % 2) * 4 + pl.program_id(1) * MLSTM_HEADS + i // 2 for i in dirs]
    c_scr[...] = jnp.zeros_like(c_scr)

    def step(j, ms):
        cis = [_chunk_order(j, i % 2, n_chunks, n_ctx_chunks) for i in dirs]
        rows = [pl.ds(pl.multiple_of(ci * CH, CH), CH) for ci in cis]
        qs = [q_ref[rows[i], lanes[i]] for i in dirs]
        ks = [k_ref[rows[i], lanes[i]] for i in dirs]
        qk = [_dot_nt(qs[d], ks[d]) for d in dirs]
        pc = [_dot(qs[d], c_scr[d].astype(BF16)) for d in dirs]
        a_blk = [a_ref[rw, :] for rw in rows]
        at_blk = [at_ref[ci] for ci in cis]
        li_col = [_lane_col(a_blk[d], gate[d]) for d in dirs]
        b_col = [_lane_col(a_blk[d], 8 + gate[d]) for d in dirs]
        li_row = [_sub_row(at_blk[d], gate[d]) for d in dirs]
        b_row = [_sub_row(at_blk[d], 8 + gate[d]) for d in dirs]
        dm = [jnp.where(masks[d], b_col[d] + (li_row[d] - b_row[d]), NEG) for d in dirs]
        inter = [ms[d] + b_col[d] for d in dirs]
        m_t = [jnp.maximum(inter[d], jnp.max(dm[d], axis=1, keepdims=True)) for d in dirs]
        m_s = [m_t[d] - LOG2_SCALE for d in dirs]
        s = [(qk[d] * jnp.exp2(dm[d] - m_s[d])).astype(BF16) for d in dirs]
        vs = [v_ref[rows[i], lanes[i]] for i in dirs]
        sv = [_dot(s[d], jnp.concatenate([vs[d], one_col.astype(BF16)], axis=1)) for d in dirs]

        b_end = [b_col[d][lasts[d]:lasts[d] + 1, :] for d in dirs]
        dec = [(b_end[d] - b_col[d]) + li_col[d] for d in dirs]
        m_new = [jnp.maximum(ms[d] + b_end[d], jnp.max(dec[d], axis=0, keepdims=True)) for d in dirs]
        w_state = [jnp.exp2(dec[d] - m_new[d]) for d in dirs]
        wv = [jnp.concatenate([(w_state[d] * vs[d].astype(F32)).astype(BF16), (w_state[d] * one_col).astype(BF16)],
                              axis=1) for d in dirs]
        upd = [_dot_tn(ks[d], wv[d]) for d in dirs]
        for d in dirs:
            c_scr[d] = jnp.exp2(ms[d] + b_end[d] - m_new[d]) * c_scr[d] + upd[d]

        for d in dirs:
            nd = jnp.exp2(inter[d] - m_s[d]) * pc[d] + sv[d]
            h_out = nd[:, :LANES] / jnp.maximum(jnp.abs(nd[:, LANES:LANES + 1]), jnp.exp2(-m_t[d]))
            if d % 2 == 0:
                y_ref[rows[d], lanes[d]] = h_out
            else:
                y1_scr[rows[d], lanes[d]] = h_out
        return tuple(m_new)

    zero = jnp.zeros((1, 1), F32)
    lax.fori_loop(0, n_chunks, step, (zero,) * len(dirs))

    def finish(ci, carry):
        rw = pl.ds(pl.multiple_of(ci * CH, CH), CH)
        y_ref[rw, :] = (y_ref[rw, :] + y1_scr[rw, :]) * _sigmoid(o_ref[rw, :].astype(F32))
        return carry

    lax.fori_loop(0, n_chunks, finish, 0)


def _mlstm(u, a, at_ch, ctx_len):
    bsz, t, _ = u.shape
    n_chunks = t // CH
    kern = functools.partial(_mlstm_kernel, n_chunks=n_chunks, n_ctx_chunks=ctx_len // CH)
    hp = MLSTM_HEADS
    wide = hp * LANES

    def spec(name):
        blk = CB[name] // hp
        return pl.BlockSpec((None, t, wide), lambda b, h: (b, 0, blk + h))

    return pl.pallas_call(
        kern,
        grid=(bsz, N_HEADS // hp),
        in_specs=[spec("m_q"), spec("m_k"), spec("m_v"), spec("m_o"),
                  pl.BlockSpec((None, t, LANES), lambda b, h: (b, 0, 0)),
                  pl.BlockSpec((None, n_chunks, 32, CH), lambda b, h: (b, 0, 0, 0))],
        out_specs=pl.BlockSpec((None, t, wide), lambda b, h: (b, 0, h)),
        out_shape=jax.ShapeDtypeStruct((bsz, t, GROUP_W), F32),
        scratch_shapes=[pltpu.VMEM((2 * hp, HEAD_DIM, 2 * LANES), F32), pltpu.VMEM((t, wide), F32)],
        compiler_params=_params(("parallel", "parallel")),
        name="mlstm",
    )(u, u, u, u, a, at_ch)


def _retention_kernel(lg_ref, q_ref, k_ref, v_ref, cos_ref, sin_ref, y_ref, qp_scr, kp_scr, r_scr, y1_scr, *,
                      n_chunks, n_ctx_chunks):
    def prep(ci, carry):
        rows = pl.ds(pl.multiple_of(ci * CH, CH), CH)
        cos = cos_ref[rows, :]
        sin = sin_ref[rows, :]
        for hh in range(RET_HEADS):
            ln = slice(hh * LANES, (hh + 1) * LANES)
            qp_scr[rows, ln] = _rope(q_ref[rows, ln].astype(F32), cos, sin).astype(BF16)
            kp_scr[rows, ln] = (_rope(k_ref[rows, ln].astype(F32), cos, sin) * SCALE).astype(BF16)
        return carry

    lax.fori_loop(0, n_chunks, prep, 0)

    r = lax.broadcasted_iota(jnp.int32, (CH, CH), 0).astype(F32)
    c = lax.broadcasted_iota(jnp.int32, (CH, CH), 1).astype(F32)
    pos = lax.broadcasted_iota(jnp.int32, (CH, 1), 0).astype(F32)

    dirs = range(2 * RET_HEADS)
    lanes = [slice((i // 2) * LANES, (i // 2 + 1) * LANES) for i in dirs]
    decay, q_decay, k_decay, chunk_decay = [], [], [], []
    for i in dirs:
        d = i % 2
        lg = lg_ref[d * N_HEADS + pl.program_id(1) * RET_HEADS + i // 2]
        rel = (r - c) if d == 0 else (c - r)
        decay.append(jnp.where(rel >= 0, jnp.exp(lg * jnp.maximum(rel, 0.0)), 0.0))
        p_vis = pos if d == 0 else (CH - 1.0) - pos
        q_decay.append(jnp.exp(lg * (p_vis + 1.0)))
        k_decay.append(jnp.exp(lg * ((CH - 1.0) - p_vis)))
        chunk_decay.append(jnp.exp(lg * CH))
    r_scr[...] = jnp.zeros_like(r_scr)

    def step(j, carry):
        cis = [_chunk_order(j, i % 2, n_chunks, n_ctx_chunks) for i in dirs]
        rows = [pl.ds(pl.multiple_of(ci * CH, CH), CH) for ci in cis]
        qs = [qp_scr[rows[i], lanes[i]] for i in dirs]
        ks = [kp_scr[rows[i], lanes[i]] for i in dirs]
        vs = [v_ref[rows[i], lanes[i]] for i in dirs]
        qk = [_dot_nt(qs[d], ks[d]) for d in dirs]
        qr = [_dot(qs[d], r_scr[d].astype(BF16)) for d in dirs]
        sv = [_dot((qk[d] * decay[d]).astype(BF16), vs[d]) for d in dirs]
        upd = [_dot_tn((ks[d].astype(F32) * k_decay[d]).astype(BF16), vs[d]) for d in dirs]
        for d in dirs:
            r_scr[d] = chunk_decay[d] * r_scr[d] + upd[d]
        for d in dirs:
            o = sv[d] + q_decay[d] * qr[d]
            if d % 2 == 0:
                y_ref[rows[d], lanes[d]] = o
            else:
                y1_scr[rows[d], lanes[d]] = o
        return carry

    lax.fori_loop(0, n_chunks, step, 0)

    def finish(ci, carry):
        rw = pl.ds(pl.multiple_of(ci * CH, CH), CH)
        y_ref[rw, :] = y_ref[rw, :] + y1_scr[rw, :]
        return carry

    lax.fori_loop(0, n_chunks, finish, 0)


def _retention(u, log_gamma, cos_t, sin_t, ctx_len):
    bsz, t, _ = u.shape
    n_chunks = t // CH
    kern = functools.partial(_retention_kernel, n_chunks=n_chunks, n_ctx_chunks=ctx_len // CH)
    hp = RET_HEADS
    wide = hp * LANES
    tab = pl.BlockSpec((t, LANES), lambda b, h: (0, 0), pipeline_mode=pl.Buffered(1))

    def spec(name):
        blk = CB[name] // hp
        return pl.BlockSpec((None, t, wide), lambda b, h: (b, 0, blk + h))

    return pl.pallas_call(
        kern,
        grid=(bsz, N_HEADS // hp),
        in_specs=[pl.BlockSpec(memory_space=pltpu.SMEM), spec("r_q"), spec("r_k"), spec("r_v"), tab, tab],
        out_specs=pl.BlockSpec((None, t, wide), lambda b, h: (b, 0, h)),
        out_shape=jax.ShapeDtypeStruct((bsz, t, GROUP_W), F32),
        scratch_shapes=[pltpu.VMEM((t, wide), BF16), pltpu.VMEM((t, wide), BF16),
                        pltpu.VMEM((2 * hp, HEAD_DIM, HEAD_DIM), F32), pltpu.VMEM((t, wide), F32)],
        compiler_params=_params(("parallel", "parallel")),
        name="retention",
    )(log_gamma.reshape(2 * N_HEADS), u, u, u, cos_t, sin_t)


def _attn_kernel(q_ref, k_ref, v_ref, cos_ref, sin_ref, cosq_ref, sinq_ref, qw_ref, kw_ref, o_ref,
                 kt_scr, s_scr, m_scr, *, n_blocks, with_ctx):
    j = pl.program_id(2)

    def load_q():
        qs = []
        for g in range(2):
            q = q_ref[:, g * LANES:(g + 1) * LANES].astype(F32)
            q = q * lax.rsqrt(jnp.mean(q * q, axis=-1, keepdims=True) + EPS) * qw_ref[...]
            qs.append((_rope(q, cosq_ref[...], sinq_ref[...]) * (SCALE * LOG2E)).astype(BF16))
        return jnp.concatenate(qs, axis=0)

    def finish(o, l_part):
        o = o / jnp.sum(l_part, axis=-1, keepdims=True)
        o_ref[:, 0:LANES] = o[:CH]
        o_ref[:, LANES:2 * LANES] = o[CH:]

    @pl.when(j == 0)
    def _():
        def prep(ci, carry):
            rows = pl.ds(pl.multiple_of(ci * CH, CH), CH)
            k = k_ref[rows, :].astype(F32)
            k = k * lax.rsqrt(jnp.mean(k * k, axis=-1, keepdims=True) + EPS) * kw_ref[...]
            kt_scr[ci] = _rope(k, cos_ref[rows, :], sin_ref[rows, :]).T.astype(BF16)
            return carry

        lax.fori_loop(0, n_blocks, prep, 0)
        if with_ctx:
            s = _dot(load_q(), kt_scr[0])
            p = jnp.exp2(s - jnp.max(s, axis=-1, keepdims=True))
            finish(_dot(p.astype(BF16), v_ref[0:CH, :]), p[:, :LANES] + p[:, LANES:])
        else:
            o_ref[...] = jnp.zeros_like(o_ref)

    @pl.when(j >= 1)
    def _():
        qq = load_q()
        mx = None
        for kb in range(n_blocks):
            s = _dot(qq, kt_scr[kb])
            s_scr[:, kb * CH:(kb + 1) * CH] = s
            m2 = jnp.maximum(s[:, :LANES], s[:, LANES:])
            mx = m2 if mx is None else jnp.maximum(mx, m2)
        m_scr[...] = jnp.broadcast_to(jnp.max(mx, axis=-1, keepdims=True), (2 * CH, LANES))

    def value_pass(_, carry):
        m = m_scr[...]
        one_col = (lax.broadcasted_iota(jnp.int32, (CH, LANES), 1) == 0).astype(BF16)
        o = None
        for kb in range(n_blocks):
            cols = slice(kb * CH, (kb + 1) * CH)
            p_lo = jnp.exp2(s_scr[:, kb * CH:kb * CH + LANES] - m)
            p_hi = jnp.exp2(s_scr[:, kb * CH + LANES:(kb + 1) * CH] - m)
            pv = _dot(jnp.concatenate([p_lo, p_hi], axis=1).astype(BF16),
                      jnp.concatenate([v_ref[cols, :], one_col], axis=1))
            o = pv if o is None else o + pv
        finish(o[:, :LANES], o[:, LANES:])
        return carry

    lax.fori_loop(0, jnp.minimum(j, 1), value_pass, 0)


def _attention(u, cos_t, sin_t, qn_w, kn_w, ctx_len, with_ctx):
    bsz, t, _ = u.shape
    assert ctx_len == CH
    n_blocks = t // CH
    kern = functools.partial(_attn_kernel, n_blocks=n_blocks, with_ctx=with_ctx)
    qb = CB["a_q"] // 2
    kb = CB["a_k"]
    vb = CB["a_v"]
    tab = pl.BlockSpec((t, LANES), lambda b, kv, j: (0, 0))
    tabq = pl.BlockSpec((CH, LANES), lambda b, kv, j: (j, 0))
    vec = pl.BlockSpec((1, LANES), lambda b, kv, j: (0, 0))
    return pl.pallas_call(
        kern,
        grid=(bsz, KV_HEADS, n_blocks),
        in_specs=[pl.BlockSpec((None, CH, 2 * LANES), lambda b, kv, j: (b, j, qb + kv)),
                  pl.BlockSpec((None, t, LANES), lambda b, kv, j: (b, 0, kb + kv)),
                  pl.BlockSpec((None, t, LANES), lambda b, kv, j: (b, 0, vb + kv)),
                  tab, tab, tabq, tabq, vec, vec],
        out_specs=pl.BlockSpec((None, CH, 2 * LANES), lambda b, kv, j: (b, j, kv)),
        out_shape=jax.ShapeDtypeStruct((bsz, t, GROUP_W), F32),
        scratch_shapes=[pltpu.VMEM((n_blocks, HEAD_DIM, CH), BF16), pltpu.VMEM((2 * CH, t), F32),
                        pltpu.VMEM((2 * CH, LANES), F32)],
        compiler_params=_params(("parallel", "parallel", "arbitrary")),
        name="attention",
    )(u, u, u, cos_t, sin_t, cos_t, sin_t, qn_w.reshape(1, LANES), kn_w.reshape(1, LANES))


def _block_diag(y, lane_masks):
    return jnp.concatenate([y * mk for mk in lane_masks], axis=0)


def _packed_dot3(a, b, lane_masks):
    n = a[0].shape[0]
    x = _dot(jnp.concatenate([a[0], a[1]], axis=0), _block_diag(b[0], lane_masks))
    return (x[:n] + x[n:]) + _dot(a[0], _block_diag(b[1], lane_masks))


def _tri_inverse(n_mats, r, cl, lane_masks):
    eye = (r == cl).astype(F32)
    diag16 = (r // 16) == (cl // 16)
    ms = [-jnp.where(diag16, n, 0.0) for n in n_mats]
    ps = [eye + m for m in ms]
    mps = [_split(m) for m in ms]
    mps = [_split(_packed_dot3(mp, mp, lane_masks)) for mp in mps]
    for _ in range(2):
        both = [_packed_dot3(tuple(jnp.concatenate([a, b], axis=0) for a, b in zip(_split(p), mp)), mp, lane_masks)
                for p, mp in zip(ps, mps)]
        ps = [p + x[:DC] for p, x in zip(ps, both)]
        mps = [_split(x[DC:]) for x in both]
    ps = [p + _packed_dot3(_split(p), mp, lane_masks) for p, mp in zip(ps, mps)]
    for w in (32, 64):
        off = ((r // w) == (cl // w)) & ((r // (w // 2)) != (cl // (w // 2)))
        pbs = [p.astype(BF16) for p in ps]
        pds = [_block_diag(pb, lane_masks) for pb in pbs]
        tmp = [_dot(jnp.where(off, n, 0.0).astype(BF16), pd).astype(BF16) for n, pd in zip(n_mats, pds)]
        ps = [p - _dot(pb, _block_diag(a, lane_masks)) for p, pb, a in zip(ps, pbs, tmp)]
    return ps


def _deltanet_kernel(q_ref, k_ref, v_ref, wq_ref, wk_ref, wv_ref, a_ref, at_ref, y_ref,
                     xs_scr, qd_scr, kd_scr, vd_scr, aq_scr, b_scr, ge_scr, y1_scr, *,
                     t, ctx_len):
    h = pl.program_id(1)
    n_chunks = t // DC
    n_ctx_chunks = ctx_len // DC
    pad = 8

    streams = ((q_ref, wq_ref, qd_scr, True, SCALE), (k_ref, wk_ref, kd_scr, True, 1.0),
               (v_ref, wv_ref, vd_scr, False, 1.0))
    zeros = jnp.zeros((pad, LANES), F32)
    for si in range(len(streams)):
        xs_scr[si, 0:pad, :] = zeros
        xs_scr[si, ctx_len + pad:ctx_len + 2 * pad, :] = zeros
        xs_scr[si, t + 2 * pad:t + 3 * pad, :] = zeros

    def seg_row(ci):
        return ci * CH + pad + jnp.where(ci * CH >= ctx_len, pad, 0)

    def load(ci, carry):
        rows = pl.ds(pl.multiple_of(ci * CH, CH), CH)
        for si, (src_ref, _, _, _, _) in enumerate(streams):
            xs_scr[si, pl.ds(pl.multiple_of(seg_row(ci), 8), CH), :] = src_ref[rows, :].astype(F32)
        return carry

    lax.fori_loop(0, t // CH, load, 0)

    def conv(ci, carry):
        base = seg_row(ci) - CONV_K // 2
        for si, (_, w_ref, dst_scr, l2, scale) in enumerate(streams):
            acc = xs_scr[si, pl.ds(base, CH), :] * w_ref[0:1, :]
            for j in range(1, CONV_K):
                acc = acc + xs_scr[si, pl.ds(base + j, CH), :] * w_ref[j:j + 1, :]
            acc = acc * _sigmoid(acc)
            if l2:
                acc = acc * lax.rsqrt(jnp.sum(acc * acc, axis=-1, keepdims=True) + EPS) * scale
            dst_scr[pl.ds(pl.multiple_of(ci * CH, CH), CH), :] = acc
        return carry

    lax.fori_loop(0, t // CH, conv, 0)

    wp = PACK * DC
    r = lax.broadcasted_iota(jnp.int32, (DC, wp), 0)
    lane = lax.broadcasted_iota(jnp.int32, (DC, wp), 1)
    blk = lane // DC
    cl = lane % DC
    fwd = (blk % 2) == 0
    ahead = jnp.where(fwd, r - cl, cl - r)
    incl = ahead >= 0
    strict = ahead > 0
    lane_masks = [(blk == i).astype(BF16) for i in range(PACK)]

    def pick(vals):
        out = vals[PACK - 1]
        for i in range(PACK - 2, -1, -1):
            out = jnp.where(blk == i, vals[i], out)
        return out

    def pre(first_pair, n_groups):
        groups = []
        n_mats = []
        for gg in range(n_groups):
            pair = first_pair + gg
            at_blk = at_ref[pair]
            g_row = jnp.where(fwd[0:1, :], _sub_row(at_blk, 16 + h), _sub_row(at_blk, 20 + h))
            kks, qks, g_cols, betas, chains = [], [], [], [], []
            for cc in range(2):
                ci = pair * 2 + cc
                rows = pl.ds(pl.multiple_of(ci * DC, DC), DC)
                q = qd_scr[rows, :]
                k = kd_scr[rows, :]
                v = vd_scr[rows, :]
                a_blk = a_ref[rows, :]
                kb = k.astype(BF16)
                kb2 = jnp.concatenate([kb, kb], axis=0)
                kks.append(_dot_nt(kb, kb2))
                qks.append(_dot_nt(q.astype(BF16), kb2))
                for d in range(2):
                    last = DC - 1 if d == 0 else 0
                    g_col = _lane_col(a_blk, 16 + d * 4 + h)
                    beta = _lane_col(a_blk, 24 + d * 4 + h)
                    eg = jnp.exp2(g_col)
                    g_end = g_col[last:last + 1, :]
                    rhs = jnp.concatenate([beta * v, (beta * eg) * k], axis=1).astype(BF16)
                    ke = (jnp.exp2(g_end - g_col) * k).astype(BF16)
                    ge_scr[d, ci] = jnp.broadcast_to(jnp.exp2(g_end), (8, LANES))
                    g_cols.append(g_col)
                    betas.append(beta)
                    chains.append((d, ci, rows, rhs, ke, eg * q))
            decay = jnp.exp2(jnp.where(incl, pick(g_cols) - g_row, NEG))
            n_mats.append(jnp.where(strict, pick(betas) * decay * jnp.concatenate(kks, axis=1), 0.0))
            qkd = (jnp.concatenate(qks, axis=1) * decay).astype(BF16)
            groups.append((chains, qkd, jnp.concatenate([ch[3] for ch in chains], axis=0)))
        t_invs = [t_inv.astype(BF16) for t_inv in _tri_inverse(n_mats, r, cl, lane_masks)]
        ws = [[_dot(t_inv * mk, rhs_all).astype(BF16) for mk in lane_masks]
              for t_inv, (_, _, rhs_all) in zip(t_invs, groups)]
        w_all = [jnp.concatenate(w4, axis=0) for w4 in ws]
        kws = [[_dot_tn(ch[4], w) for ch, w in zip(chains, w4)] for (chains, _, _), w4 in zip(groups, ws)]
        qws = [[_dot(qkd * mk, wa) for mk in lane_masks] for (_, qkd, _), wa in zip(groups, w_all)]
        for (chains, _, _), kw4, qw4 in zip(groups, kws, qws):
            for (d, ci, rows, _, _, egq), kw, qw in zip(chains, kw4, qw4):
                aq_scr[d, ci, 0:HEAD_DIM, :] = (-kw[:, LANES:]).astype(BF16)
                aq_scr[d, ci, HEAD_DIM:HEAD_DIM + DC, :] = (egq - qw[:, LANES:]).astype(BF16)
                b_scr[d, ci] = kw[:, :LANES]
                if d == 0:
                    y_ref[rows, :] = qw[:, :LANES]
                else:
                    y1_scr[rows, :] = qw[:, :LANES]

    n_pairs = n_chunks // 2
    n_trips = max(1, n_pairs // PRE_GROUPS)
    base, extra = divmod(n_pairs, n_trips)

    def trips(first_pair, count, size):
        def body(i, carry):
            pre(first_pair + i * size, size)
            return carry

        if count:
            lax.fori_loop(0, count, body, 0)

    trips(0, extra, base + 1)
    trips(extra * (base + 1), n_trips - extra, base)

    def scan(j, carry):
        cis = [_chunk_order(j, d, n_chunks, n_ctx_chunks) for d in range(2)]
        xs = [_dot(aq_scr[d, cis[d]], carry[d].astype(BF16)) for d in range(2)]
        new = tuple(ge_scr[d, cis[d]][0:1, :] * carry[d] + (xs[d][:HEAD_DIM, :] + b_scr[d, cis[d]])
                    for d in range(2))
        rows = [pl.ds(pl.multiple_of(cis[d] * DC, DC), DC) for d in range(2)]
        y_ref[rows[0], :] = y_ref[rows[0], :] + xs[0][HEAD_DIM:, :]
        y1_scr[rows[1], :] = y1_scr[rows[1], :] + xs[1][HEAD_DIM:, :]
        return new

    zero = jnp.zeros((HEAD_DIM, HEAD_DIM), F32)
    lax.fori_loop(0, n_chunks, scan, (zero, zero))

    def add(ci, carry):
        rows = pl.ds(pl.multiple_of(ci * CH, CH), CH)
        y_ref[rows, :] = y_ref[rows, :] + y1_scr[rows, :]
        return carry

    lax.fori_loop(0, t // CH, add, 0)


def _deltanet(u, conv_w, a, at_dc, ctx_len):
    bsz, t, _ = u.shape
    n_chunks = t // DC
    kern = functools.partial(_deltanet_kernel, t=t, ctx_len=ctx_len)

    def wspec(off):
        return pl.BlockSpec((CONV_K, LANES), lambda b, h: (0, off + h))

    return pl.pallas_call(
        kern,
        grid=(bsz, N_HEADS),
        in_specs=[_head_spec(t, "d_q"), _head_spec(t, "d_k"), _head_spec(t, "d_v"),
                  wspec(0), wspec(N_HEADS), wspec(2 * N_HEADS),
                  pl.BlockSpec((None, t, LANES), lambda b, h: (b, 0, 0)),
                  pl.BlockSpec((None, n_chunks // 2, 32, PACK * DC), lambda b, h: (b, 0, 0, 0))],
        out_specs=pl.BlockSpec((None, t, LANES), lambda b, h: (b, 0, h)),
        out_shape=jax.ShapeDtypeStruct((bsz, t, GROUP_W), F32),
        scratch_shapes=[pltpu.VMEM((3, t + 24, LANES), F32),
                        pltpu.VMEM((t, LANES), F32), pltpu.VMEM((t, LANES), F32), pltpu.VMEM((t, LANES), F32),
                        pltpu.VMEM((2, n_chunks, HEAD_DIM + DC, LANES), BF16),
                        pltpu.VMEM((2, n_chunks, HEAD_DIM, LANES), F32),
                        pltpu.VMEM((2, n_chunks, 8, LANES), F32),
                        pltpu.VMEM((t, LANES), F32)],
        compiler_params=_params(("parallel", "parallel")),
        name="deltanet",
    )(u, u, u, conv_w, conv_w, conv_w, a, at_dc)


def _out_kernel(x_ref, ym_ref, yr_ref, ya_ref, yd_ref, zm_ref, zr_ref, za_ref, zd_ref, hn_ref, w_ref,
                mod_ref, pw_ref, o_ref, *, first_block):
    b = pl.program_id(0)
    i = pl.program_id(1) + first_block

    def silu(z_ref):
        z = z_ref[...].astype(F32)
        return z * _sigmoid(z)

    def head_rms(y_ref, g):
        y = y_ref[...]
        parts = []
        for hh in range(N_HEADS):
            p = y[:, hh * LANES:(hh + 1) * LANES]
            parts.append(p * lax.rsqrt(jnp.mean(p * p, axis=-1, keepdims=True) + EPS))
        return jnp.concatenate(parts, axis=1) * hn_ref[:, g * GROUP_W:(g + 1) * GROUP_W]

    merged = jnp.concatenate([
        (head_rms(ym_ref, 0) * silu(zm_ref)).astype(BF16),
        (head_rms(yr_ref, 1) * silu(zr_ref)).astype(BF16),
        (ya_ref[...] * silu(za_ref)).astype(BF16),
        (head_rms(yd_ref, 2) * silu(zd_ref)).astype(BF16)], axis=1)
    o = _dot(merged, w_ref[...])
    o = o * lax.rsqrt(jnp.mean(o * o, axis=-1, keepdims=True) + EPS) * pw_ref[...]
    row = jnp.where(i == 0, 4, b)
    gate = mod_ref[pl.ds(row, 1), 2 * D_MODEL:3 * D_MODEL]
    o_ref[...] = x_ref[...] + gate * o


def _out_projection(xa, ys, u, hn_w, w_out, layer, mod, post_w, ctx_len, latent_only):
    bsz, t, d = xa.shape
    assert ctx_len == CH
    first = 1 if latent_only else 0
    tok = lambda b, i: (b, i + first, 0)
    yspec = pl.BlockSpec((None, CH, GROUP_W), tok)

    def zspec(name):
        blk = CB[name] // N_HEADS
        return pl.BlockSpec((None, CH, GROUP_W), lambda b, i: (b, i + first, blk))

    return pl.pallas_call(
        functools.partial(_out_kernel, first_block=first),
        grid=(bsz, t // CH - first),
        in_specs=[pl.BlockSpec((None, CH, d), tok), yspec, yspec, yspec, yspec,
                  zspec("m_z"), zspec("r_z"), zspec("a_z"), zspec("d_z"),
                  pl.BlockSpec((1, 3 * GROUP_W), lambda b, i: (0, 0)),
                  pl.BlockSpec((None, d, d), lambda b, i: (layer, 0, 0)),
                  pl.BlockSpec((8, 3 * d), lambda b, i: (0, 0)),
                  pl.BlockSpec((1, d), lambda b, i: (0, 0))],
        out_specs=pl.BlockSpec((None, CH, d), lambda b, i: (b, i, 0)),
        out_shape=jax.ShapeDtypeStruct((bsz, t - first * CH, d), F32),
        compiler_params=_params(("parallel", "parallel")),
        name="out_proj",
    )(xa, *ys, u, u, u, u, hn_w.reshape(1, 3 * GROUP_W), w_out, mod, post_w.reshape(1, d))


M_GATE_COL = 5 * GROUP_W
N_M_GATES = 4 * N_HEADS
D_GATE_COL = N_MAIN + N_M_GATES
W_TILE = 512


def _wprep_kernel(a_ref, b_ref, o_ref):
    n = pl.program_id(1)

    @pl.when(n < M_GATE_COL // W_TILE)
    def _():
        o_ref[...] = a_ref[...]

    @pl.when(n >= M_GATE_COL // W_TILE)
    def _():
        o_ref[...] = jnp.concatenate([a_ref[:, N_M_GATES:], b_ref[:, :N_M_GATES]], axis=1)


def _reorder_w_in(w_in):
    depth, d, _ = w_in.shape
    assert M_GATE_COL % W_TILE == 0
    wb = w_in.astype(BF16)
    w_main = pl.pallas_call(
        _wprep_kernel,
        grid=(depth, N_MAIN // W_TILE),
        in_specs=[pl.BlockSpec((None, d, W_TILE), lambda l, n: (l, 0, n)),
                  pl.BlockSpec((None, d, LANES), lambda l, n: (l, 0, (n + 1) * (W_TILE // LANES)))],
        out_specs=pl.BlockSpec((None, d, W_TILE), lambda l, n: (l, 0, n)),
        out_shape=jax.ShapeDtypeStruct((depth, d, N_MAIN), BF16),
        compiler_params=_params(("parallel", "parallel")),
        name="w_in_prep",
    )(wb, wb)
    gates = jnp.concatenate([wb[:, :, M_GATE_COL:M_GATE_COL + N_M_GATES], wb[:, :, D_GATE_COL:]], axis=2)
    w_gate = jnp.pad(gates, ((0, 0), (0, 0), (0, LANES - gates.shape[2])))
    return w_main, w_gate


def _rope_tables(seq, ctx_len):
    rows = seq // GRID_W
    row = jnp.repeat(jnp.arange(rows), GRID_W)
    col = jnp.tile(jnp.arange(GRID_W), rows)
    n_freq = HEAD_DIM // 4
    inv_freq = ROPE_BASE ** (-jnp.arange(n_freq, dtype=F32) / n_freq)
    ar = row[:, None] * inv_freq
    ac = col[:, None] * inv_freq
    cos = jnp.concatenate([jnp.cos(ar), jnp.cos(ar), jnp.cos(ac), jnp.cos(ac)], axis=1)
    sin = jnp.concatenate([-jnp.sin(ar), jnp.sin(ar), -jnp.sin(ac), jnp.sin(ac)], axis=1)
    cos = jnp.concatenate([jnp.ones((ctx_len, HEAD_DIM), F32), cos], axis=0)
    sin = jnp.concatenate([jnp.zeros((ctx_len, HEAD_DIM), F32), sin], axis=0)
    return cos, sin


def kernel(x, c, ctx, c_ctx, ada_w, ada_b, pre_norm_w, post_norm_w, w_in, w_out, mlstm_i_bias, mlstm_f_bias,
           ret_log_gamma, attn_q_norm_w, attn_k_norm_w, dn_conv_w, dn_a_log, dn_dt_bias, head_norm_w):
    bsz, seq, d = x.shape
    ctx_len = ctx.shape[1]
    t = ctx_len + seq
    depth = ada_w.shape[0]
    assert bsz <= 4 and ctx_len % CH == 0 and seq % CH == 0

    xa = jnp.concatenate([ctx, x], axis=1)
    c8 = jnp.zeros((8, d), F32).at[:bsz].set(c).at[4].set(c_ctx)
    mod = _modulation(c8, ada_w, ada_b)
    cos_t, sin_t = _rope_tables(seq, ctx_len)

    w_main, w_gate = _reorder_w_in(w_in)
    w_out_b = w_out.astype(BF16)

    for l in range(depth):
        u, g = _in_projection(xa, mod[l], pre_norm_w[l], w_main, w_gate, l, ctx_len)
        zeros8 = jnp.zeros((8,), F32)
        bias_row = jnp.pad(jnp.concatenate([mlstm_i_bias[l].reshape(-1), mlstm_f_bias[l].reshape(-1),
                                            dn_dt_bias[l].reshape(-1), zeros8]), (0, LANES - 32))
        alog_row = jnp.pad(jnp.concatenate([zeros8, zeros8, dn_a_log[l].reshape(-1), zeros8]), (0, LANES - 32))
        a, at_ch, at_dc = _gate_prep(g, bias_row.reshape(1, LANES), alog_row.reshape(1, LANES))
        y_m = _mlstm(u, a, at_ch, ctx_len)
        y_r = _retention(u, ret_log_gamma[l], cos_t, sin_t, ctx_len)
        y_a = _attention(u, cos_t, sin_t, attn_q_norm_w[l], attn_k_norm_w[l], ctx_len, with_ctx=(l < depth - 1))
        y_d = _deltanet(u, dn_conv_w[l], a, at_dc, ctx_len)
        xa = _out_projection(xa, (y_m, y_r, y_a, y_d), u, head_norm_w[l], w_out_b, l, mod[l],
                             post_norm_w[l], ctx_len, latent_only=(l == depth - 1))
    return xa
```
